```python
import math
import jax, jax.numpy as jnp
from jax import lax
import numpy as np

D_MODEL = 1024
BATCH = 4
SEQ = 4096
DEPTH = 4
DEC_BATCH = 128
DEC_SEQ = 4
PAST_LEN = 8192
PAGE_SIZE = 128

DN_HEADS = 4
DN_DK = 128
DN_DV = 128
DN_CONV = 4
DN_CHUNK = 64
SW_QHEADS = 8
SW_KVHEADS = 2
SW_HD = 64
SW_GROUP = SW_QHEADS // SW_KVHEADS
WINDOW = 128
DN_QK_W = DN_HEADS * DN_DK
DN_V_W = DN_HEADS * DN_DV
CONV_DIM = 2 * DN_QK_W + DN_V_W
SW_Q_W = SW_QHEADS * SW_HD
SW_KV_W = SW_KVHEADS * SW_HD
MIX_W = DN_V_W + SW_Q_W
OFF_Z = CONV_DIM
OFF_B = OFF_Z + DN_V_W
OFF_A = OFF_B + DN_HEADS
OFF_SQ = OFF_A + DN_HEADS
OFF_SK = OFF_SQ + SW_Q_W
OFF_SV = OFF_SK + SW_KV_W
IN_DIM = OFF_SV + SW_KV_W
PK_HEADS = 8
N_KEYS = 128
N_EXPERTS = N_KEYS * N_KEYS
PK_QDIM = 256
PK_HALF = PK_QDIM // 2
PK_TOPK = 16
PEER_BLOCK = 128
EPS = 1e-6

kernel_name = 'hymba_deltanet_swa_sink_peer_adaln_step'


def _rmsnorm(x, g):
    xf = x.astype(jnp.float32)
    y = xf * lax.rsqrt(jnp.mean(xf * xf, axis=-1, keepdims=True) + EPS)
    return (y * g.astype(jnp.float32)).astype(x.dtype)


def _l2norm(x):
    return x * lax.rsqrt(jnp.sum(x * x, axis=-1, keepdims=True) + EPS)


def _short_conv(u, buf, w):
    L = u.shape[1]
    up = jnp.concatenate([buf.astype(u.dtype), u], axis=1)
    y = sum(up[:, i:i + L] * w[i] for i in range(DN_CONV))
    return jax.nn.silu(y), up[:, L:]


def _gated_delta(q, k, v, beta, g, s0):
    B, L = q.shape[:2]
    C = DN_CHUNK if L % DN_CHUNK == 0 else L
    N = L // C

    def blocks(t):
        t = t.reshape((B, N, C) + t.shape[2:])
        return jnp.moveaxis(t, 3, 1)

    q, k, v, beta, g = (blocks(t) for t in (q, k, v, beta, g))
    gc = jnp.cumsum(g, axis=-1)
    diff = gc[..., :, None] - gc[..., None, :]
    idx = jnp.arange(C)
    causal = idx[:, None] >= idx[None, :]
    strict = idx[:, None] > idx[None, :]
    decay = jnp.where(causal, jnp.exp(jnp.where(causal, diff, 0.0)), 0.0)
    kk = jnp.einsum('bhncd,bhnsd->bhncs', k, k)
    a_mat = jnp.where(strict, beta[..., :, None] * kk * decay, 0.0) + jnp.eye(C, dtype=q.dtype)
    rhs = jnp.concatenate([beta[..., None] * v, (beta * jnp.exp(gc))[..., None] * k], axis=-1)
    sol = lax.linalg.triangular_solve(a_mat, rhs, left_side=True, lower=True, unit_diagonal=True)
    u0, w = sol[..., :DN_DV], sol[..., DN_DV:]
    qk = jnp.einsum('bhncd,bhnsd->bhncs', q, k) * decay
    q_dec = q * jnp.exp(gc)[..., None]
    k_dec = k * jnp.exp(gc[..., -1:] - gc)[..., None]
    c_dec = jnp.exp(gc[..., -1])

    def step(s, xs):
        u0_n, w_n, qk_n, qd_n, kd_n, cd_n = xs
        u = u0_n - jnp.einsum('bhck,bhkv->bhcv', w_n, s)
        o = jnp.einsum('bhck,bhkv->bhcv', qd_n, s) + jnp.einsum('bhcs,bhsv->bhcv', qk_n, u)
        s = cd_n[..., None, None] * s + jnp.einsum('bhck,bhcv->bhkv', kd_n, u)
        return s, o

    xs = tuple(jnp.moveaxis(t, 2, 0) for t in (u0, w, qk, q_dec, k_dec, c_dec))
    s_fin, o = lax.scan(step, s0, xs)
    o = jnp.transpose(o, (1, 0, 3, 2, 4)).reshape(B, L, DN_HEADS, DN_DV)
    return o, s_fin


def _deltanet(proj, conv_buf, s0, conv_w, a_log, dt_bias, norm_g):
    B, L = proj.shape[:2]
    f32 = jnp.float32
    qkv, new_buf = _short_conv(proj[..., :CONV_DIM], conv_buf, conv_w)
    qkv = qkv.astype(f32)
    q = _l2norm(qkv[..., :DN_QK_W].reshape(B, L, DN_HEADS, DN_DK)) * (DN_DK ** -0.5)
    k = _l2norm(qkv[..., DN_QK_W:2 * DN_QK_W].reshape(B, L, DN_HEADS, DN_DK))
    v = qkv[..., 2 * DN_QK_W:].reshape(B, L, DN_HEADS, DN_DV)
    z = proj[..., OFF_Z:OFF_B].astype(f32).reshape(B, L, DN_HEADS, DN_DV)
    beta = jax.nn.sigmoid(proj[..., OFF_B:OFF_A].astype(f32))
    g = -jnp.exp(a_log.astype(f32)) * jax.nn.softplus(proj[..., OFF_A:OFF_SQ].astype(f32) + dt_bias.astype(f32))
    o, s_new = _gated_delta(q, k, v, beta, g, s0.astype(f32))
    o = _rmsnorm(o, norm_g) * jax.nn.silu(z)
    return o.reshape(B, L, DN_V_W).astype(proj.dtype), s_new.astype(s0.dtype), new_buf


def _sink_attend(q, k, v, mask, sinks):
    f32 = jnp.float32
    s = jnp.einsum('...qhgd,...khd->...hgqk', q.astype(f32), k.astype(f32)) * (SW_HD ** -0.5)
    s = jnp.where(mask, s, -jnp.inf)
    sink = jnp.broadcast_to(sinks.astype(f32)[:, :, None, None], s.shape[:-1] + (1,))
    p = jax.nn.softmax(jnp.concatenate([s, sink], axis=-1), axis=-1)[..., :-1]
    return jnp.einsum('...hgqk,...khd->...qhgd', p, v.astype(f32))


def _swa_prompt(q, k, v, sinks):
    B, L = q.shape[:2]
    nb = L // WINDOW
    qb = q.reshape(B, nb, WINDOW, SW_KVHEADS, SW_GROUP, SW_HD)
    kb = k.reshape(B, nb, WINDOW, SW_KVHEADS, SW_HD)
    vb = v.reshape(B, nb, WINDOW, SW_KVHEADS, SW_HD)
    kk = jnp.concatenate([jnp.concatenate([jnp.zeros_like(kb[:, :1]), kb[:, :-1]], axis=1), kb], axis=2)
    vv = jnp.concatenate([jnp.concatenate([jnp.zeros_like(vb[:, :1]), vb[:, :-1]], axis=1), vb], axis=2)
    qpos = WINDOW + jnp.arange(WINDOW)
    kpos = jnp.arange(2 * WINDOW)
    d = qpos[:, None] - kpos[None, :]
    band = (d >= 0) & (d < WINDOW)
    valid = (jnp.arange(nb) > 0)[:, None, None] | (kpos >= WINDOW)[None, None, :]
    mask = (band[None] & valid)[:, None, None]
    o = _sink_attend(qb, kk, vv, mask, sinks)
    return o.reshape(B, L, SW_Q_W)


def _swa_decode(q, k, v, k_buf, v_buf, sinks):
    B, L = q.shape[:2]
    P = k_buf.shape[1]
    kk = jnp.concatenate([k_buf.astype(k.dtype), k], axis=1)
    vv = jnp.concatenate([v_buf.astype(v.dtype), v], axis=1)
    qpos = P + jnp.arange(L)
    kpos = jnp.arange(P + L)
    d = qpos[:, None] - kpos[None, :]
    mask = ((d >= 0) & (d < WINDOW))[None, None, None]
    o = _sink_attend(q.reshape(B, L, SW_KVHEADS, SW_GROUP, SW_HD), kk, vv, mask, sinks)
    return o.reshape(B, L, SW_Q_W), kk[:, L:], vv[:, L:]


def _peer(h, wq, sub_keys, u_tab, v_tab):
    B, L, D = h.shape
    T = B * L
    f32 = jnp.float32
    xf = h.reshape(T, D)
    q = (xf @ wq).reshape(T, PK_HEADS, 2, PK_HALF).astype(f32)
    s = jnp.einsum('thpd,hpnd->thpn', q, sub_keys.astype(f32))
    s1, i1 = lax.top_k(s[:, :, 0], PK_TOPK)
    s2, i2 = lax.top_k(s[:, :, 1], PK_TOPK)
    cand = (s1[..., :, None] + s2[..., None, :]).reshape(T, PK_HEADS, PK_TOPK * PK_TOPK)
    cidx = (i1[..., :, None] * N_KEYS + i2[..., None, :]).reshape(T, PK_HEADS, PK_TOPK * PK_TOPK)
    top_s, pos = lax.top_k(cand, PK_TOPK)
    eidx = jnp.take_along_axis(cidx, pos, axis=-1).reshape(T, PK_HEADS * PK_TOPK)
    gate = jax.nn.softmax(top_s, axis=-1).reshape(T, PK_HEADS * PK_TOPK)
    nblk = -(-T // PEER_BLOCK)
    pad = nblk * PEER_BLOCK - T
    xb = jnp.pad(xf, ((0, pad), (0, 0))).reshape(nblk, PEER_BLOCK, D)
    ib = jnp.pad(eidx, ((0, pad), (0, 0))).reshape(nblk, PEER_BLOCK, PK_HEADS * PK_TOPK)
    gb = jnp.pad(gate, ((0, pad), (0, 0))).reshape(nblk, PEER_BLOCK, PK_HEADS * PK_TOPK)

    def block(args):
        xt, it, gt = args
        a = jnp.einsum('td,tkd->tk', xt.astype(f32), u_tab[it].astype(f32))
        a = jax.nn.gelu(a, approximate=False) * gt
        return jnp.einsum('tk,tkd->td', a, v_tab[it].astype(f32))

    out = lax.map(block, (xb, ib, gb)).reshape(nblk * PEER_BLOCK, D)[:T]
    return out.reshape(B, L, D).astype(h.dtype)


def _layer(x, c, prior, norm1_g, norm2_g, w_ada, b_ada, w_in, conv_w, dn_a_log,
           dn_dt_bias, dn_norm_g, sw_sinks, w_out, peer_wq, peer_keys, peer_u, peer_v):
    B, L, _ = x.shape
    mod = jax.nn.silu(c.astype(jnp.float32)) @ w_ada.astype(jnp.float32) + b_ada.astype(jnp.float32)
    sh1, sc1, g1, sh2, sc2, g2 = [m[:, None, :].astype(x.dtype) for m in jnp.split(mod, 6, axis=-1)]
    h = _rmsnorm(x, norm1_g) * (1 + sc1) + sh1
    proj = h @ w_in
    if prior is None:
        conv_buf = jnp.zeros((B, DN_CONV - 1, CONV_DIM), proj.dtype)
        s0 = jnp.zeros((B, DN_HEADS, DN_DK, DN_DV), x.dtype)
    else:
        s0, conv_buf, k_buf, v_buf = prior
    o_dn, new_s, new_conv = _deltanet(proj, conv_buf, s0, conv_w, dn_a_log, dn_dt_bias, dn_norm_g)
    q = proj[..., OFF_SQ:OFF_SK]
    k = proj[..., OFF_SK:OFF_SV].reshape(B, L, SW_KVHEADS, SW_HD)
    v = proj[..., OFF_SV:IN_DIM].reshape(B, L, SW_KVHEADS, SW_HD)
    sinks = sw_sinks.reshape(SW_KVHEADS, SW_GROUP)
    if prior is None:
        o_sw = _swa_prompt(q, k, v, sinks)
        n_buf = min(WINDOW, PAST_LEN)
        new_k, new_v = k[:, L - n_buf:], v[:, L - n_buf:]
    else:
        o_sw, new_k, new_v = _swa_decode(q, k, v, k_buf, v_buf, sinks)
    mix = jnp.concatenate([o_dn, o_sw.astype(x.dtype)], axis=-1) @ w_out
    x = x + g1 * mix
    h2 = _rmsnorm(x, norm2_g) * (1 + sc2) + sh2
    x = x + g2 * _peer(h2, peer_wq, peer_keys, peer_u, peer_v)
    return x, (new_s, new_conv, new_k, new_v)


def _trunk(x, c, prior_stack, layer_params, final_norm_g):
    new = []
    for l in range(DEPTH):
        prior = None if prior_stack is None else tuple(s[l] for s in prior_stack)
        x, st = _layer(x, c, prior, *[a[l] for a in layer_params])
        new.append(st)
    y = _rmsnorm(x, final_norm_g)
    return y, [jnp.stack(z) for z in zip(*new)]


def setup_inputs(seed: int = 0) -> dict:
    key = jax.random.key(seed)
    ks = jax.random.split(key, 26)
    f32 = jnp.float32

    def nrm(k, shape, scale):
        return jax.random.normal(k, shape, f32) * scale

    n_buf = min(WINDOW, PAST_LEN)
    dt = jnp.exp(jax.random.uniform(ks[20], (DEPTH, DN_HEADS), f32, math.log(1e-3), math.log(1e-1)))
    return {
        'x_prompt': nrm(ks[0], (BATCH, SEQ, D_MODEL), 1.0),
        'x_sample': nrm(ks[1], (DEC_BATCH, DEC_SEQ, D_MODEL), 1.0),
        'state_delta': nrm(ks[2], (DEPTH, DEC_BATCH, DN_HEADS, DN_DK, DN_DV), 0.05),
        'state_conv': nrm(ks[3], (DEPTH, DEC_BATCH, DN_CONV - 1, CONV_DIM), 1.0),
        'cache_swa_k': nrm(ks[4], (DEPTH, DEC_BATCH, n_buf, SW_KVHEADS, SW_HD), 1.0),
        'cache_swa_v': nrm(ks[5], (DEPTH, DEC_BATCH, n_buf, SW_KVHEADS, SW_HD), 1.0),
        'c_prompt': nrm(ks[6], (BATCH, D_MODEL), 1.0),
        'c_sample': nrm(ks[7], (DEC_BATCH, D_MODEL), 1.0),
        'norm1_g': 1.0 + nrm(ks[8], (DEPTH, D_MODEL), 0.02),
        'norm2_g': 1.0 + nrm(ks[9], (DEPTH, D_MODEL), 0.02),
        'final_norm_g': 1.0 + nrm(ks[10], (D_MODEL,), 0.02),
        'w_ada': nrm(ks[11], (DEPTH, D_MODEL, 6 * D_MODEL), 0.5 * D_MODEL ** -0.5),
        'b_ada': nrm(ks[12], (DEPTH, 6 * D_MODEL), 0.01),
        'w_in': nrm(ks[13], (DEPTH, D_MODEL, IN_DIM), D_MODEL ** -0.5),
        'conv_w': nrm(ks[14], (DEPTH, DN_CONV, CONV_DIM), DN_CONV ** -0.5),
        'dn_a_log': jnp.log(jax.random.uniform(ks[15], (DEPTH, DN_HEADS), f32, 1.0, 16.0)),
        'dn_dt_bias': dt + jnp.log(-jnp.expm1(-dt)),
        'dn_norm_g': 1.0 + nrm(ks[16], (DEPTH, DN_DV), 0.02),
        'sw_sinks': nrm(ks[17], (DEPTH, SW_QHEADS), 0.5),
        'w_out': nrm(ks[18], (DEPTH, MIX_W, D_MODEL), MIX_W ** -0.5),
        'peer_wq': nrm(ks[19], (DEPTH, D_MODEL, PK_HEADS * PK_QDIM), D_MODEL ** -0.5),
        'peer_keys': nrm(ks[21], (DEPTH, PK_HEADS, 2, N_KEYS, PK_HALF), PK_HALF ** -0.5),
        'peer_u': nrm(ks[22], (DEPTH, N_EXPERTS, D_MODEL), D_MODEL ** -0.5),
        'peer_v': nrm(ks[23], (DEPTH, N_EXPERTS, D_MODEL), 0.5),
    }


def reference(x_prompt, x_sample, state_delta, state_conv, cache_swa_k, cache_swa_v,
              c_prompt, c_sample, norm1_g, norm2_g, final_norm_g, w_ada, b_ada, w_in,
              conv_w, dn_a_log, dn_dt_bias, dn_norm_g, sw_sinks, w_out, peer_wq,
              peer_keys, peer_u, peer_v):
    layer_params = (norm1_g, norm2_g, w_ada, b_ada, w_in, conv_w, dn_a_log, dn_dt_bias,
                    dn_norm_g, sw_sinks, w_out, peer_wq, peer_keys, peer_u, peer_v)
    y_prompt, (delta_p, conv_p, k_p, v_p) = _trunk(x_prompt, c_prompt, None, layer_params, final_norm_g)
    y_sample, (delta_s, conv_s, k_s, v_s) = _trunk(
        x_sample, c_sample, (state_delta, state_conv, cache_swa_k, cache_swa_v), layer_params, final_norm_g)
    return (y_prompt, y_sample, delta_p, conv_p, k_p, v_p, delta_s, conv_s, k_s, v_s)
```

```python
import functools

import jax
import jax.numpy as jnp
from jax import lax
from jax.experimental import pallas as pl
from jax.experimental.pallas import tpu as pltpu

F32 = jnp.float32
BF16 = jnp.bfloat16

D_MODEL = 1024
DEPTH = 4
DN_HEADS = 4
DN_DK = 128
DN_DV = 128
DN_CONV = 4
DN_CHUNK = 64
SW_QHEADS = 8
SW_KVHEADS = 2
SW_HD = 64
SW_GROUP = SW_QHEADS // SW_KVHEADS
WINDOW = 128
DN_QK_W = DN_HEADS * DN_DK
DN_V_W = DN_HEADS * DN_DV
CONV_DIM = 2 * DN_QK_W + DN_V_W
SW_Q_W = SW_QHEADS * SW_HD
SW_KV_W = SW_KVHEADS * SW_HD
OFF_Z = CONV_DIM
OFF_B = OFF_Z + DN_V_W
OFF_A = OFF_B + DN_HEADS
OFF_SQ = OFF_A + DN_HEADS
PK_HEADS = 8
N_KEYS = 128
N_EXPERTS = N_KEYS * N_KEYS
PK_QDIM = 256
PK_HALF = PK_QDIM // 2
PK_TOPK = 16
EPS = 1e-6

LANES = 128
SUBLANES = 8
VMEM_LIMIT = 56 * 1024 * 1024
PROJ_W = CONV_DIM + DN_V_W + 2 * LANES + SW_Q_W + 2 * SW_KV_W
DEC_PAD = SUBLANES
N_TOP = PK_TOPK + 1
NEG_INF = float("-inf")


def _cparams(sem):
    return pltpu.CompilerParams(dimension_semantics=sem, vmem_limit_bytes=VMEM_LIMIT)


def _silu(x):
    return x * jax.nn.sigmoid(x)


def _softplus(x):
    return jnp.maximum(x, 0.0) + jnp.log1p(jnp.exp(-jnp.abs(x)))


def _rms(x, g):
    return x * lax.rsqrt(jnp.mean(x * x, axis=-1, keepdims=True) + EPS) * g


def _dot(a, b):
    return jnp.dot(a, b, preferred_element_type=F32)


def _dot_nt(a, b):
    return lax.dot_general(a, b, (((1,), (1,)), ((), ())), preferred_element_type=F32)


def _dot_tn(a, b):
    return lax.dot_general(a, b, (((0,), (0,)), ((), ())), preferred_element_type=F32)


def _ada_kernel(c_ref, w_ref, b_ref, o_ref):
    a = _silu(c_ref[...]).astype(BF16)
    o_ref[0] = _dot(a, w_ref[0].astype(BF16)) + b_ref[0]


def _ada_mod(c_all, w_ada, b_ada):
    rows = c_all.shape[0]
    tn = 1536
    return pl.pallas_call(
        _ada_kernel,
        grid=(DEPTH, 6 * D_MODEL // tn),
        in_specs=[
            pl.BlockSpec((rows, D_MODEL), lambda l, j: (0, 0)),
            pl.BlockSpec((1, D_MODEL, tn), lambda l, j: (l, 0, j)),
            pl.BlockSpec((1, 1, tn), lambda l, j: (l, 0, j)),
        ],
        out_specs=pl.BlockSpec((1, rows, tn), lambda l, j: (l, 0, j)),
        out_shape=jax.ShapeDtypeStruct((DEPTH, rows, 6 * D_MODEL), F32),
        compiler_params=_cparams(("arbitrary", "arbitrary")),
        name="ada_mod",
    )(c_all, w_ada, b_ada.reshape(DEPTH, 1, 6 * D_MODEL))


def _mod_spec(k, tiles_per_group, rows):
    return pl.BlockSpec((1, rows, D_MODEL), lambda i: (i // tiles_per_group, 0, k))


def _inproj_kernel(x_ref, sh_ref, sc_ref, g_ref, w_ref,
                   qkv_ref, z_ref, b_ref, a_ref, sq_ref, skv_ref):
    h = _rms(x_ref[...], g_ref[...]) * (1.0 + sc_ref[0]) + sh_ref[0]
    p = _dot(h.astype(BF16), w_ref[...])
    o = 0
    for ref in (qkv_ref, z_ref, b_ref, a_ref, sq_ref, skv_ref):
        w = ref.shape[-1]
        ref[...] = p[:, o:o + w]
        o += w


def _inproj(x, mod, tiles_per_group, norm_g, w_r, tt):
    t = x.shape[0]
    rows = mod.shape[1]
    widths = (CONV_DIM, DN_V_W, LANES, LANES, SW_Q_W, 2 * SW_KV_W)
    return pl.pallas_call(
        _inproj_kernel,
        grid=(t // tt,),
        in_specs=[
            pl.BlockSpec((tt, D_MODEL), lambda i: (i, 0)),
            _mod_spec(0, tiles_per_group, rows),
            _mod_spec(1, tiles_per_group, rows),
            pl.BlockSpec((1, D_MODEL), lambda i: (0, 0)),
            pl.BlockSpec((D_MODEL, PROJ_W), lambda i: (0, 0)),
        ],
        out_specs=[pl.BlockSpec((tt, w), lambda i: (i, 0)) for w in widths],
        out_shape=[jax.ShapeDtypeStruct((t, w), F32) for w in widths],
        compiler_params=_cparams(("arbitrary",)),
        name="inproj",
    )(x, mod, mod, norm_g.reshape(1, D_MODEL), w_r)


def _dn_kernel(u_ref, z_ref, b_ref, a_ref, cw_ref, alog_ref, dtb_ref, ng_ref, cbuf_ref, s0_ref,
               o_ref, sfin_ref, ubuf, qkv, beta_s, gc_s, gct_s, state,
               *, nb_blk, lb, chunk, valid_len):
    l = pl.program_id(1)
    nc = lb // chunk
    n_scan = chunk.bit_length() - 1

    @pl.when(l == 0)
    def _():
        state[...] = s0_ref[...]
        ubuf[:, SUBLANES - (DN_CONV - 1):SUBLANES, :] = cbuf_ref[...]

    ubuf[:, SUBLANES:SUBLANES + lb, :] = u_ref[...]
    y = None
    for i in range(DN_CONV):
        start = SUBLANES - (DN_CONV - 1) + i
        term = ubuf[:, start:start + lb, :] * cw_ref[i:i + 1, :]
        y = term if y is None else y + term
    qkv[...] = _silu(y)
    ubuf[:, SUBLANES - (DN_CONV - 1):SUBLANES, :] = ubuf[:, lb + SUBLANES - (DN_CONV - 1):lb + SUBLANES, :]

    row = lax.broadcasted_iota(jnp.int32, (lb, LANES), 0)
    for nb in range(nb_blk):
        beta = jax.nn.sigmoid(b_ref[nb])
        g = -jnp.exp(alog_ref[...]) * _softplus(a_ref[nb] + dtb_ref[...])
        if valid_len < lb:
            beta = jnp.where(row < valid_len, beta, 0.0)
            g = jnp.where(row < valid_len, g, 0.0)
        gc = g
        for s in range(n_scan):
            sh = 1 << s
            gc = gc + jnp.where((row % chunk) >= sh, pltpu.roll(gc, sh, 0), 0.0)
        beta_s[nb] = beta
        gc_s[nb] = gc
        for r0 in range(0, lb, LANES):
            nrow = min(LANES, lb - r0)
            blk = gc[r0:r0 + nrow]
            if nrow < LANES:
                blk = jnp.concatenate([blk, jnp.zeros((LANES - nrow, LANES), F32)], axis=0)
            blk_t = blk.T
            for c0 in range(0, nrow, chunk):
                gct_s[nb, (r0 + c0) // chunk] = blk_t[:, c0:c0 + chunk]

    ii = lax.broadcasted_iota(jnp.int32, (chunk, chunk), 0)
    jj = lax.broadcasted_iota(jnp.int32, (chunk, chunk), 1)
    causal = ii >= jj
    strict = ii > jj

    def chunk_body(idx, carry):
        if nc == 1:
            nb, c, rows = idx, 0, pl.ds(0, chunk)
        else:
            nb = idx // nc
            c = idx % nc
            rows = pl.ds(pl.multiple_of(c * chunk, chunk), chunk)
        beta = beta_s[nb, rows, :]
        gc = gc_s[nb, rows, :]
        gct = gct_s[nb, c]
        for h in range(DN_HEADS):
            hs = slice(h * DN_DK, (h + 1) * DN_DK)
            q = qkv[nb, rows, hs]
            k = qkv[nb, rows, pl.ds(DN_QK_W + h * DN_DK, DN_DK)]
            v = qkv[nb, rows, pl.ds(2 * DN_QK_W + h * DN_DV, DN_DV)]
            q = q * lax.rsqrt(jnp.sum(q * q, axis=-1, keepdims=True) + EPS) * (DN_DK ** -0.5)
            k = k * lax.rsqrt(jnp.sum(k * k, axis=-1, keepdims=True) + EPS)
            col = gc[:, h:h + 1]
            rw = gct[h:h + 1, :]
            bcol = beta[:, h:h + 1]
            decay = jnp.where(causal, jnp.exp(jnp.where(causal, col - rw, 0.0)), 0.0)
            kk = _dot_nt(k, k)
            n_mat = jnp.where(strict, -(bcol * kk * decay), 0.0)
            t_mat = jnp.where(ii == jj, 1.0, 0.0) + n_mat
            p_mat = n_mat
            for _ in range(n_scan - 1):
                p_mat = _dot(p_mat, p_mat)
                t_mat = t_mat + _dot(t_mat, p_mat)
            e_col = jnp.exp(col)
            rhs = jnp.concatenate([bcol * v, (bcol * e_col) * k], axis=-1)
            sol = _dot(t_mat, rhs)
            u0 = sol[:, :DN_DV]
            w = sol[:, DN_DV:]
            qk = _dot_nt(q, k) * decay
            g_last = gc[chunk - 1:chunk, h:h + 1]
            q_dec = q * e_col
            k_dec = k * jnp.exp(g_last - col)
            s_old = state[nb, h]
            ws = _dot(jnp.concatenate([w, q_dec], axis=0), s_old)
            uu = u0 - ws[:chunk]
            o = ws[chunk:] + _dot(qk, uu)
            state[nb, h] = jnp.exp(g_last) * s_old + _dot_tn(k_dec, uu)
            o = _rms(o, ng_ref[...])
            zz = z_ref[nb, rows, pl.ds(h * DN_DV, DN_DV)]
            o_ref[nb, rows, pl.ds(h * DN_DV, DN_DV)] = (o * _silu(zz)).astype(o_ref.dtype)
        return carry

    lax.fori_loop(0, nb_blk * nc, chunk_body, 0)

    @pl.when(l == pl.num_programs(1) - 1)
    def _():
        sfin_ref[...] = state[...]


def _deltanet(u, z, b, a, conv_w, alog, dtb, norm_g, conv_buf, s0, *, nb_blk, lb, chunk, valid_len):
    bsz, seq, _ = u.shape
    nc = lb // chunk
    kern = functools.partial(_dn_kernel, nb_blk=nb_blk, lb=lb, chunk=chunk, valid_len=valid_len)
    tok = lambda w: pl.BlockSpec((nb_blk, lb, w), lambda i, l: (i, l, 0))
    full2 = lambda r, w: pl.BlockSpec((r, w), lambda i, l: (0, 0))
    return pl.pallas_call(
        kern,
        grid=(bsz // nb_blk, seq // lb),
        in_specs=[
            tok(CONV_DIM), tok(DN_V_W), tok(LANES), tok(LANES),
            full2(DN_CONV, CONV_DIM), full2(1, LANES), full2(1, LANES), full2(1, DN_DV),
            pl.BlockSpec((nb_blk, DN_CONV - 1, CONV_DIM), lambda i, l: (i, 0, 0)),
            pl.BlockSpec((nb_blk, DN_HEADS, DN_DK, DN_DV), lambda i, l: (i, 0, 0, 0)),
        ],
        out_specs=[
            tok(DN_V_W),
            pl.BlockSpec((nb_blk, DN_HEADS, DN_DK, DN_DV), lambda i, l: (i, 0, 0, 0)),
        ],
        out_shape=[
            jax.ShapeDtypeStruct((bsz, seq, DN_V_W), F32),
            jax.ShapeDtypeStruct((bsz, DN_HEADS, DN_DK, DN_DV), F32),
        ],
        scratch_shapes=[
            pltpu.VMEM((nb_blk, lb + SUBLANES, CONV_DIM), F32),
            pltpu.VMEM((nb_blk, lb, CONV_DIM), F32),
            pltpu.VMEM((nb_blk, lb, LANES), F32),
            pltpu.VMEM((nb_blk, lb, LANES), F32),
            pltpu.VMEM((nb_blk, nc, LANES, chunk), F32),
            pltpu.VMEM((nb_blk, DN_HEADS, DN_DK, DN_DV), F32),
        ],
        compiler_params=_cparams(("arbitrary", "arbitrary")),
        name="deltanet",
    )(u, z, b, a, conv_w, alog, dtb, norm_g, conv_buf, s0)


def _swa_prompt_kernel(sink_ref, q_ref, kvp_ref, kvc_ref, o_ref):
    i = pl.program_id(1)
    q = q_ref[0]
    kvp = kvp_ref[0]
    kvc = kvc_ref[0]
    kcat = jnp.concatenate([kvp[:, :SW_KV_W], kvc[:, :SW_KV_W]], axis=0).astype(BF16)
    vcat = jnp.concatenate([kvp[:, SW_KV_W:], kvc[:, SW_KV_W:]], axis=0).astype(BF16)
    row = lax.broadcasted_iota(jnp.int32, (WINDOW, 2 * WINDOW), 0)
    col = lax.broadcasted_iota(jnp.int32, (WINDOW, 2 * WINDOW), 1)
    valid = (col > row) & (col <= row + WINDOW) & ((col >= WINDOW) | (i > 0))
    for qh in range(SW_QHEADS):
        kv = qh // SW_GROUP
        qs = q[:, qh * SW_HD:(qh + 1) * SW_HD].astype(BF16)
        ks = kcat[:, kv * SW_HD:(kv + 1) * SW_HD]
        vs = vcat[:, kv * SW_HD:(kv + 1) * SW_HD]
        s = _dot_nt(qs, ks) * (SW_HD ** -0.5)
        s = jnp.where(valid, s, NEG_INF)
        sink = sink_ref[qh]
        m = jnp.maximum(jnp.max(s, axis=-1, keepdims=True), sink)
        p = jnp.exp(s - m)
        den = jnp.sum(p, axis=-1, keepdims=True) + jnp.exp(sink - m)
        o = _dot(p.astype(BF16), vs) / den
        o_ref[0, :, qh * SW_HD:(qh + 1) * SW_HD] = o.astype(o_ref.dtype)


def _swa_prompt(sq, skv, sinks):
    bsz, seq, _ = sq.shape
    nblk = seq // WINDOW
    return pl.pallas_call(
        _swa_prompt_kernel,
        grid=(bsz, nblk),
        in_specs=[
            pl.BlockSpec(memory_space=pltpu.SMEM),
            pl.BlockSpec((1, WINDOW, SW_Q_W), lambda b, i: (b, i, 0)),
            pl.BlockSpec((1, WINDOW, 2 * SW_KV_W), lambda b, i: (b, jnp.maximum(i - 1, 0), 0)),
            pl.BlockSpec((1, WINDOW, 2 * SW_KV_W), lambda b, i: (b, i, 0)),
        ],
        out_specs=pl.BlockSpec((1, WINDOW, SW_Q_W), lambda b, i: (b, i, 0)),
        out_shape=jax.ShapeDtypeStruct((bsz, seq, SW_Q_W), F32),
        compiler_params=_cparams(("arbitrary", "arbitrary")),
        name="swa_prompt",
    )(sinks, sq, skv, skv)


def _swa_decode_kernel(sink_ref, q_ref, kvn_ref, kb_ref, vb_ref, o_ref, ko_ref, vo_ref, *, n_new):
    q = q_ref[...]
    kvn = kvn_ref[...]
    kb = kb_ref[...]
    vb = vb_ref[...]
    nb = q.shape[0]
    p_len = kb.shape[1]
    kn = kvn[:, :, :SW_KV_W]
    vn = kvn[:, :, SW_KV_W:]
    ko_ref[:, :p_len - n_new, :] = kb[:, n_new:, :]
    ko_ref[:, p_len - n_new:, :] = kn[:, :n_new, :]
    vo_ref[:, :p_len - n_new, :] = vb[:, n_new:, :]
    vo_ref[:, p_len - n_new:, :] = vn[:, :n_new, :]
    qi = lax.broadcasted_iota(jnp.int32, (nb, DEC_PAD, p_len), 1)
    kj = lax.broadcasted_iota(jnp.int32, (nb, DEC_PAD, p_len), 2)
    valid_buf = (p_len + qi - kj) < WINDOW
    qi2 = lax.broadcasted_iota(jnp.int32, (nb, DEC_PAD, DEC_PAD), 1)
    kj2 = lax.broadcasted_iota(jnp.int32, (nb, DEC_PAD, DEC_PAD), 2)
    valid_new = (kj2 <= qi2) & (kj2 < n_new)
    for qh in range(SW_QHEADS):
        kv = qh // SW_GROUP
        hs = slice(kv * SW_HD, (kv + 1) * SW_HD)
        qs = q[:, :, qh * SW_HD:(qh + 1) * SW_HD]
        sb = jnp.einsum("bqd,bkd->bqk", qs, kb[:, :, hs], preferred_element_type=F32) * (SW_HD ** -0.5)
        sn = jnp.einsum("bqd,bkd->bqk", qs, kn[:, :, hs], preferred_element_type=F32) * (SW_HD ** -0.5)
        sb = jnp.where(valid_buf, sb, NEG_INF)
        sn = jnp.where(valid_new, sn, NEG_INF)
        sink = sink_ref[qh]
        m = jnp.maximum(jnp.maximum(jnp.max(sb, axis=-1, keepdims=True),
                                    jnp.max(sn, axis=-1, keepdims=True)), sink)
        pb = jnp.exp(sb - m)
        pn = jnp.exp(sn - m)
        den = jnp.sum(pb, axis=-1, keepdims=True) + jnp.sum(pn, axis=-1, keepdims=True) + jnp.exp(sink - m)
        o = (jnp.einsum("bqk,bkd->bqd", pb, vb[:, :, hs], preferred_element_type=F32)
             + jnp.einsum("bqk,bkd->bqd", pn, vn[:, :, hs], preferred_element_type=F32))
        o_ref[:, :, qh * SW_HD:(qh + 1) * SW_HD] = (o / den).astype(o_ref.dtype)


def _swa_decode(sq, skv, k_buf, v_buf, sinks, *, nb_blk, n_new):
    bsz = sq.shape[0]
    p_len = k_buf.shape[1]
    tok = lambda w: pl.BlockSpec((nb_blk, DEC_PAD, w), lambda i: (i, 0, 0))
    cache = pl.BlockSpec((nb_blk, p_len, SW_KV_W), lambda i: (i, 0, 0))
    return pl.pallas_call(
        functools.partial(_swa_decode_kernel, n_new=n_new),
        grid=(bsz // nb_blk,),
        in_specs=[pl.BlockSpec(memory_space=pltpu.SMEM), tok(SW_Q_W), tok(2 * SW_KV_W), cache, cache],
        out_specs=[tok(SW_Q_W), cache, cache],
        out_shape=[
            jax.ShapeDtypeStruct((bsz, DEC_PAD, SW_Q_W), F32),
            jax.ShapeDtypeStruct((bsz, p_len, SW_KV_W), F32),
            jax.ShapeDtypeStruct((bsz, p_len, SW_KV_W), F32),
        ],
        compiler_params=_cparams(("arbitrary",)),
        name="swa_decode",
    )(sinks, sq, skv, k_buf, v_buf)


def _outproj_kernel(x_ref, odn_ref, osw_ref, g1_ref, sh2_ref, sc2_ref, n2_ref, w_ref, xo_ref, h2_ref):
    mix = (_dot(odn_ref[...].astype(BF16), w_ref[:DN_V_W, :])
           + _dot(osw_ref[...].astype(BF16), w_ref[DN_V_W:, :]))
    x = x_ref[...] + g1_ref[0] * mix
    xo_ref[...] = x
    h2 = _rms(x, n2_ref[...]) * (1.0 + sc2_ref[0]) + sh2_ref[0]
    h2_ref[...] = h2.astype(h2_ref.dtype)


def _outproj(x, o_dn, o_sw, mod, tiles_per_group, norm_g, w_out, tt):
    t = x.shape[0]
    rows = mod.shape[1]
    return pl.pallas_call(
        _outproj_kernel,
        grid=(t // tt,),
        in_specs=[
            pl.BlockSpec((tt, D_MODEL), lambda i: (i, 0)),
            pl.BlockSpec((tt, DN_V_W), lambda i: (i, 0)),
            pl.BlockSpec((tt, SW_Q_W), lambda i: (i, 0)),
            _mod_spec(2, tiles_per_group, rows),
            _mod_spec(3, tiles_per_group, rows),
            _mod_spec(4, tiles_per_group, rows),
            pl.BlockSpec((1, D_MODEL), lambda i: (0, 0)),
            pl.BlockSpec((D_MODEL, D_MODEL), lambda i: (0, 0)),
        ],
        out_specs=[pl.BlockSpec((tt, D_MODEL), lambda i: (i, 0))] * 2,
        out_shape=[jax.ShapeDtypeStruct((t, D_MODEL), F32), jax.ShapeDtypeStruct((t, D_MODEL), BF16)],
        compiler_params=_cparams(("arbitrary",)),
        name="outproj",
    )(x, o_dn, o_sw, mod, mod, mod, norm_g.reshape(1, D_MODEL), w_out)


def _top_values(work, n):
    out = []
    for it in range(n):
        m = jnp.max(work, axis=0, keepdims=True)
        out.append(m)
        if it + 1 < n:
            work = jnp.where(work == m, NEG_INF, work)
    return out


def _peer_kernel(h2_ref, x_ref, g2_ref, wq_ref, keys_ref, u_ref, vt_ref, fg_ref, o_ref,
                 s_s, thr_s, e1_s, e2_s, tv_s, acc, *, tt, et, final_norm):
    j = pl.program_id(1)
    lg_n = tt // LANES

    @pl.when(j == 0)
    def _():
        acc[...] = jnp.zeros_like(acc)
        tv_s[...] = jnp.full(tv_s.shape, NEG_INF, F32)
        q = _dot(h2_ref[...], wq_ref[...]).astype(BF16)
        for hp in range(2 * PK_HEADS):
            s_t = _dot_nt(keys_ref[hp], q[:, hp * PK_HALF:(hp + 1) * PK_HALF])
            for lg in range(lg_n):
                s_s[hp, lg] = s_t[:, lg * LANES:(lg + 1) * LANES]

        def head_body(idx, carry):
            h = idx // lg_n
            lg = idx % lg_n
            s1 = s_s[2 * h, lg]
            s2 = s_s[2 * h + 1, lg]
            t1 = _top_values(s1, N_TOP)
            t2 = _top_values(s2, N_TOP)
            for it in range(N_TOP):
                tv_s[it:it + 1, :] = t2[it]
            rid = lax.broadcasted_iota(jnp.int32, (SUBLANES, LANES), 0)
            cands = []
            for i in range(N_TOP):
                n_i = N_TOP // (i + 1)
                for r0 in range(0, n_i, SUBLANES):
                    blk = t1[i] + tv_s[r0:r0 + SUBLANES, :]
                    if n_i - r0 < SUBLANES:
                        blk = jnp.where(rid < n_i - r0, blk, NEG_INF)
                    cands.append(blk)
            cand = jnp.concatenate(cands, axis=0)
            tops = _top_values(cand, N_TOP)
            thr = 0.5 * (tops[PK_TOPK - 1] + tops[PK_TOPK])
            m1 = t1[0]
            m2 = t2[0]
            zsum = jnp.sum(jnp.where(cand >= thr, jnp.exp(cand - (m1 + m2)), 0.0), axis=0, keepdims=True)
            thr_s[h, lg] = thr - s1
            e1_s[h, lg] = jnp.exp(s1 - m1) / zsum
            e2_s[h, lg] = jnp.exp(s2 - m2)
            return carry

        lax.fori_loop(0, PK_HEADS * lg_n, head_body, 0)

    a_t = _dot_nt(u_ref[...], h2_ref[...])
    act = 0.5 * a_t * (1.0 + lax.erf(a_t * (2.0 ** -0.5)))
    blocks = []
    for rr in range(et // N_KEYS):
        r = j * (et // N_KEYS) + rr
        row_blocks = []
        for lg in range(lg_n):
            g = None
            for h in range(PK_HEADS):
                thr_r = thr_s[h, lg, pl.ds(r, 1), :]
                e1_r = e1_s[h, lg, pl.ds(r, 1), :]
                term = jnp.where(s_s[2 * h + 1, lg] >= thr_r, e2_s[h, lg], 0.0) * e1_r
                g = term if g is None else g + term
            blk = act[rr * N_KEYS:(rr + 1) * N_KEYS, lg * LANES:(lg + 1) * LANES] * g
            row_blocks.append(blk.astype(BF16))
        blocks.append(jnp.concatenate(row_blocks, axis=1) if lg_n > 1 else row_blocks[0])
    p_t = jnp.concatenate(blocks, axis=0) if len(blocks) > 1 else blocks[0]
    acc[...] += _dot(vt_ref[...], p_t)

    @pl.when(j == pl.num_programs(1) - 1)
    def _():
        x = x_ref[...] + g2_ref[0] * acc[...].T
        if final_norm:
            x = _rms(x, fg_ref[...])
        o_ref[...] = x


def _peer(h2, x, mod, tiles_per_group, wq, keys, u_tab, vt_tab, final_g, *, tt, et, final_norm):
    t = x.shape[0]
    rows = mod.shape[1]
    lg_n = tt // LANES
    kern = functools.partial(_peer_kernel, tt=tt, et=et, final_norm=final_norm)
    return pl.pallas_call(
        kern,
        grid=(t // tt, N_EXPERTS // et),
        in_specs=[
            pl.BlockSpec((tt, D_MODEL), lambda i, j: (i, 0)),
            pl.BlockSpec((tt, D_MODEL), lambda i, j: (i, 0)),
            pl.BlockSpec((1, rows, D_MODEL), lambda i, j: (i // tiles_per_group, 0, 5)),
            pl.BlockSpec((D_MODEL, PK_HEADS * PK_QDIM), lambda i, j: (0, 0)),
            pl.BlockSpec((2 * PK_HEADS, N_KEYS, PK_HALF), lambda i, j: (0, 0, 0)),
            pl.BlockSpec((et, D_MODEL), lambda i, j: (j, 0)),
            pl.BlockSpec((D_MODEL, et), lambda i, j: (0, j)),
            pl.BlockSpec((1, D_MODEL), lambda i, j: (0, 0)),
        ],
        out_specs=pl.BlockSpec((tt, D_MODEL), lambda i, j: (i, 0)),
        out_shape=jax.ShapeDtypeStruct((t, D_MODEL), F32),
        scratch_shapes=[
            pltpu.VMEM((2 * PK_HEADS, lg_n, N_KEYS, LANES), F32),
            pltpu.VMEM((PK_HEADS, lg_n, N_KEYS, LANES), F32),
            pltpu.VMEM((PK_HEADS, lg_n, N_KEYS, LANES), F32),
            pltpu.VMEM((PK_HEADS, lg_n, N_KEYS, LANES), F32),
            pltpu.VMEM((3 * SUBLANES, LANES), F32),
            pltpu.VMEM((D_MODEL, tt), F32),
        ],
        compiler_params=_cparams(("arbitrary", "arbitrary")),
        name="peer",
    )(h2, x, mod, wq, keys, u_tab, vt_tab, final_g.reshape(1, D_MODEL))


def _pad_lanes(v):
    return jnp.pad(v, ((0, 0), (0, LANES - v.shape[-1])))


def _prep_w_in(w):
    return jnp.concatenate(
        [w[:, :OFF_B], _pad_lanes(w[:, OFF_B:OFF_A]), _pad_lanes(w[:, OFF_A:OFF_SQ]), w[:, OFF_SQ:]],
        axis=1).astype(BF16)


def _layer(l, x, mod, tiles_per_group, prior, p, *, seq, tt, final_norm, final_g):
    t = x.shape[0]
    bsz = t // seq
    qkv, z, bcol, acol, sq, skv = _inproj(x, mod, tiles_per_group, p["norm1_g"][l], p["w_in"][l], tt)
    r3 = lambda a: a.reshape(bsz, seq, a.shape[-1])
    qkv3 = r3(qkv)
    if prior is None:
        conv_buf = jnp.zeros((bsz, DN_CONV - 1, CONV_DIM), F32)
        s0 = jnp.zeros((bsz, DN_HEADS, DN_DK, DN_DV), F32)
        dn_cfg = dict(nb_blk=1, lb=256, chunk=DN_CHUNK, valid_len=256)
    else:
        s0, conv_buf, k_buf, v_buf, n_new = prior
        dn_cfg = dict(nb_blk=8, lb=DEC_PAD, chunk=DEC_PAD, valid_len=n_new)
    o_dn, s_new = _deltanet(qkv3, r3(z), r3(bcol), r3(acol), p["conv_w"][l], p["alog"][l], p["dtb"][l],
                            p["dn_norm_g"][l], conv_buf, s0, **dn_cfg)
    if prior is None:
        o_sw = _swa_prompt(r3(sq), r3(skv), p["sw_sinks"][l])
        new_conv = qkv3[:, seq - (DN_CONV - 1):, :]
        n_buf = WINDOW
        new_k = r3(skv)[:, seq - n_buf:, :SW_KV_W].reshape(bsz, n_buf, SW_KVHEADS, SW_HD)
        new_v = r3(skv)[:, seq - n_buf:, SW_KV_W:].reshape(bsz, n_buf, SW_KVHEADS, SW_HD)
    else:
        p_len = k_buf.shape[1]
        o_sw, new_k, new_v = _swa_decode(r3(sq), r3(skv), k_buf.reshape(bsz, p_len, SW_KV_W),
                                         v_buf.reshape(bsz, p_len, SW_KV_W), p["sw_sinks"][l],
                                         nb_blk=16, n_new=n_new)
        new_conv = jnp.concatenate([conv_buf, qkv3[:, :n_new]], axis=1)[:, n_new:]
        new_k = new_k.reshape(bsz, p_len, SW_KVHEADS, SW_HD)
        new_v = new_v.reshape(bsz, p_len, SW_KVHEADS, SW_HD)
    x, h2 = _outproj(x, o_dn.reshape(t, DN_V_W), o_sw.reshape(t, SW_Q_W), mod, tiles_per_group,
                     p["norm2_g"][l], p["w_out"][l], tt)
    x = _peer(h2, x, mod, tiles_per_group, p["wq"][l], p["keys"][l], p["u"][l], p["vt"][l], final_g,
              tt=tt, et=512, final_norm=final_norm)
    return x, (s_new, new_conv, new_k, new_v)


def _trunk(x, mod_all, tiles_per_group, prior_stack, p, final_g, *, seq, tt):
    new = []
    for l in range(DEPTH):
        prior = None if prior_stack is None else tuple(s[l] for s in prior_stack[:4]) + (prior_stack[4],)
        x, st = _layer(l, x, mod_all[l], tiles_per_group, prior, p, seq=seq, tt=tt,
                       final_norm=(l == DEPTH - 1), final_g=final_g)
        new.append(st)
    return x, [jnp.stack(zz) for zz in zip(*new)]


def kernel(x_prompt, x_sample, state_delta, state_conv, cache_swa_k, cache_swa_v, c_prompt, c_sample,
           norm1_g, norm2_g, final_norm_g, w_ada, b_ada, w_in, conv_w, dn_a_log, dn_dt_bias, dn_norm_g,
           sw_sinks, w_out, peer_wq, peer_keys, peer_u, peer_v):
    batch, seq, _ = x_prompt.shape
    dec_b, dec_l, _ = x_sample.shape
    tt = 512

    p = dict(
        norm1_g=norm1_g, norm2_g=norm2_g,
        w_in=jax.vmap(_prep_w_in)(w_in),
        conv_w=conv_w,
        alog=_pad_lanes(dn_a_log).reshape(DEPTH, 1, LANES),
        dtb=_pad_lanes(dn_dt_bias).reshape(DEPTH, 1, LANES),
        dn_norm_g=dn_norm_g.reshape(DEPTH, 1, DN_DV),
        sw_sinks=sw_sinks,
        w_out=w_out.astype(BF16),
        wq=peer_wq.astype(BF16),
        keys=peer_keys.reshape(DEPTH, 2 * PK_HEADS, N_KEYS, PK_HALF).astype(BF16),
        u=peer_u.astype(BF16),
        vt=jnp.swapaxes(peer_v, 1, 2).astype(BF16),
    )

    c_dec = jnp.repeat(c_sample, DEC_PAD, axis=0)
    n_rows = batch + dec_b * DEC_PAD
    pad_rows = (-n_rows) % SUBLANES
    c_all = jnp.concatenate([c_prompt, c_dec, jnp.zeros((pad_rows, D_MODEL), F32)], axis=0)
    mod = _ada_mod(c_all, w_ada, b_ada)
    mod_p = mod[:, :batch].reshape(DEPTH, batch, 1, 6 * D_MODEL)
    mod_s = mod[:, batch:n_rows].reshape(DEPTH, dec_b * DEC_PAD // tt, tt, 6 * D_MODEL)

    xp = x_prompt.reshape(batch * seq, D_MODEL)
    yp, (delta_p, conv_p, k_p, v_p) = _trunk(xp, mod_p, seq // tt, None, p, final_norm_g, seq=seq, tt=tt)

    xs = jnp.pad(x_sample, ((0, 0), (0, DEC_PAD - dec_l), (0, 0))).reshape(dec_b * DEC_PAD, D_MODEL)
    prior = (state_delta, state_conv, cache_swa_k, cache_swa_v, dec_l)
    ys, (delta_s, conv_s, k_s, v_s) = _trunk(xs, mod_s, 1, prior, p, final_norm_g, seq=DEC_PAD, tt=tt)

    y_prompt = yp.reshape(batch, seq, D_MODEL)
    y_sample = ys.reshape(dec_b, DEC_PAD, D_MODEL)[:, :dec_l]
    return (y_prompt, y_sample, delta_p, conv_p, k_p, v_p, delta_s, conv_s, k_s, v_s)
```

```python
import functools

import jax
import jax.numpy as jnp
from jax import lax
from jax.experimental import pallas as pl
from jax.experimental.pallas import tpu as pltpu

F32 = jnp.float32
BF16 = jnp.bfloat16

D_MODEL = 1024
DEPTH = 4
DN_HEADS = 4
DN_DK = 128
DN_DV = 128
DN_CONV = 4
DN_CHUNK = 64
SW_QHEADS = 8
SW_KVHEADS = 2
SW_HD = 64
SW_GROUP = SW_QHEADS // SW_KVHEADS
WINDOW = 128
DN_QK_W = DN_HEADS * DN_DK
DN_V_W = DN_HEADS * DN_DV
CONV_DIM = 2 * DN_QK_W + DN_V_W
SW_Q_W = SW_QHEADS * SW_HD
SW_KV_W = SW_KVHEADS * SW_HD
OFF_Z = CONV_DIM
OFF_B = OFF_Z + DN_V_W
OFF_A = OFF_B + DN_HEADS
OFF_SQ = OFF_A + DN_HEADS
PK_HEADS = 8
N_KEYS = 128
N_EXPERTS = N_KEYS * N_KEYS
PK_QDIM = 256
PK_HALF = PK_QDIM // 2
PK_TOPK = 16
EPS = 1e-6

LANES = 128
SUBLANES = 8
VMEM_LIMIT = 56 * 1024 * 1024
PROJ_W = CONV_DIM + DN_V_W + 2 * LANES + SW_Q_W + 2 * SW_KV_W
DEC_PAD = SUBLANES
N_TOP = PK_TOPK + 1
PEER_SUB = 512
PEER_MM = 256
PEER_CHUNK = 32
NEG_INF = float("-inf")


def _cparams(sem):
    return pltpu.CompilerParams(dimension_semantics=sem, vmem_limit_bytes=VMEM_LIMIT)


def _silu(x):
    return x * jax.nn.sigmoid(x)


def _softplus(x):
    return jnp.maximum(x, 0.0) + jnp.log1p(jnp.exp(-jnp.abs(x)))


def _rms(x, g):
    return x * lax.rsqrt(jnp.mean(x * x, axis=-1, keepdims=True) + EPS) * g


def _dot(a, b):
    return jnp.dot(a, b, preferred_element_type=F32)


def _dot_nt(a, b):
    return lax.dot_general(a, b, (((1,), (1,)), ((), ())), preferred_element_type=F32)


def _dot_tn(a, b):
    return lax.dot_general(a, b, (((0,), (0,)), ((), ())), preferred_element_type=F32)


def _ada_kernel(c_ref, w_ref, b_ref, o_ref):
    a = _silu(c_ref[...]).astype(BF16)
    o_ref[0] = _dot(a, w_ref[0].astype(BF16)) + b_ref[0]


def _ada_mod(c_all, w_ada, b_ada):
    rows = c_all.shape[0]
    tn = 1536
    return pl.pallas_call(
        _ada_kernel,
        grid=(DEPTH, 6 * D_MODEL // tn),
        in_specs=[
            pl.BlockSpec((rows, D_MODEL), lambda l, j: (0, 0)),
            pl.BlockSpec((1, D_MODEL, tn), lambda l, j: (l, 0, j)),
            pl.BlockSpec((1, 1, tn), lambda l, j: (l, 0, j)),
        ],
        out_specs=pl.BlockSpec((1, rows, tn), lambda l, j: (l, 0, j)),
        out_shape=jax.ShapeDtypeStruct((DEPTH, rows, 6 * D_MODEL), F32),
        compiler_params=_cparams(("arbitrary", "arbitrary")),
        name="ada_mod",
    )(c_all, w_ada, b_ada.reshape(DEPTH, 1, 6 * D_MODEL))


def _mod_spec(k, tiles_per_group, rows):
    return pl.BlockSpec((1, rows, D_MODEL), lambda i: (i // tiles_per_group, 0, k))


def _inproj_kernel(x_ref, sh_ref, sc_ref, g_ref, w_ref,
                   qkv_ref, z_ref, b_ref, a_ref, sq_ref, skv_ref):
    h = _rms(x_ref[...], g_ref[...]) * (1.0 + sc_ref[0]) + sh_ref[0]
    p = _dot(h.astype(BF16), w_ref[...])
    o = 0
    for ref in (qkv_ref, z_ref, b_ref, a_ref, sq_ref, skv_ref):
        w = ref.shape[-1]
        ref[...] = p[:, o:o + w]
        o += w


def _inproj(x, mod, tiles_per_group, norm_g, w_r, tt):
    t = x.shape[0]
    rows = mod.shape[1]
    widths = (CONV_DIM, DN_V_W, LANES, LANES, SW_Q_W, 2 * SW_KV_W)
    return pl.pallas_call(
        _inproj_kernel,
        grid=(t // tt,),
        in_specs=[
            pl.BlockSpec((tt, D_MODEL), lambda i: (i, 0)),
            _mod_spec(0, tiles_per_group, rows),
            _mod_spec(1, tiles_per_group, rows),
            pl.BlockSpec((1, D_MODEL), lambda i: (0, 0)),
            pl.BlockSpec((D_MODEL, PROJ_W), lambda i: (0, 0)),
        ],
        out_specs=[pl.BlockSpec((tt, w), lambda i: (i, 0)) for w in widths],
        out_shape=[jax.ShapeDtypeStruct((t, w), F32) for w in widths],
        compiler_params=_cparams(("arbitrary",)),
        name="inproj",
    )(x, mod, mod, norm_g.reshape(1, D_MODEL), w_r)


def _dn_kernel(u_ref, z_ref, b_ref, a_ref, cw_ref, alog_ref, dtb_ref, ng_ref, cbuf_ref, s0_ref,
               o_ref, sfin_ref, ubuf, qkv, beta_s, gc_s, gct_s, state,
               *, nb_blk, lb, chunk, valid_len):
    l = pl.program_id(1)
    nc = lb // chunk
    n_scan = chunk.bit_length() - 1

    @pl.when(l == 0)
    def _():
        state[...] = s0_ref[...]
        ubuf[:, SUBLANES - (DN_CONV - 1):SUBLANES, :] = cbuf_ref[...]

    ubuf[:, SUBLANES:SUBLANES + lb, :] = u_ref[...]
    y = None
    for i in range(DN_CONV):
        start = SUBLANES - (DN_CONV - 1) + i
        term = ubuf[:, start:start + lb, :] * cw_ref[i:i + 1, :]
        y = term if y is None else y + term
    qkv[...] = _silu(y)
    ubuf[:, SUBLANES - (DN_CONV - 1):SUBLANES, :] = ubuf[:, lb + SUBLANES - (DN_CONV - 1):lb + SUBLANES, :]

    row = lax.broadcasted_iota(jnp.int32, (lb, LANES), 0)
    for nb in range(nb_blk):
        beta = jax.nn.sigmoid(b_ref[nb])
        g = -jnp.exp(alog_ref[...]) * _softplus(a_ref[nb] + dtb_ref[...])
        if valid_len < lb:
            beta = jnp.where(row < valid_len, beta, 0.0)
            g = jnp.where(row < valid_len, g, 0.0)
        gc = g
        for s in range(n_scan):
            sh = 1 << s
            gc = gc + jnp.where((row % chunk) >= sh, pltpu.roll(gc, sh, 0), 0.0)
        beta_s[nb] = beta
        gc_s[nb] = gc
        for r0 in range(0, lb, LANES):
            nrow = min(LANES, lb - r0)
            blk = gc[r0:r0 + nrow]
            if nrow < LANES:
                blk = jnp.concatenate([blk, jnp.zeros((LANES - nrow, LANES), F32)], axis=0)
            blk_t = blk.T
            for c0 in range(0, nrow, chunk):
                gct_s[nb, (r0 + c0) // chunk] = blk_t[:, c0:c0 + chunk]

    ii = lax.broadcasted_iota(jnp.int32, (chunk, chunk), 0)
    jj = lax.broadcasted_iota(jnp.int32, (chunk, chunk), 1)
    causal = ii >= jj
    strict = ii > jj

    def chunk_body(idx, carry):
        if nc == 1:
            nb, c, rows = idx, 0, pl.ds(0, chunk)
        else:
            nb = idx // nc
            c = idx % nc
            rows = pl.ds(pl.multiple_of(c * chunk, chunk), chunk)
        beta = beta_s[nb, rows, :]
        gc = gc_s[nb, rows, :]
        gct = gct_s[nb, c]
        for h in range(DN_HEADS):
            hs = slice(h * DN_DK, (h + 1) * DN_DK)
            q = qkv[nb, rows, hs]
            k = qkv[nb, rows, pl.ds(DN_QK_W + h * DN_DK, DN_DK)]
            v = qkv[nb, rows, pl.ds(2 * DN_QK_W + h * DN_DV, DN_DV)]
            q = q * lax.rsqrt(jnp.sum(q * q, axis=-1, keepdims=True) + EPS) * (DN_DK ** -0.5)
            k = k * lax.rsqrt(jnp.sum(k * k, axis=-1, keepdims=True) + EPS)
            col = gc[:, h:h + 1]
            rw = gct[h:h + 1, :]
            bcol = beta[:, h:h + 1]
            decay = jnp.where(causal, jnp.exp(jnp.where(causal, col - rw, 0.0)), 0.0)
            kk = _dot_nt(k, k)
            n_mat = jnp.where(strict, -(bcol * kk * decay), 0.0)
            t_mat = jnp.where(ii == jj, 1.0, 0.0) + n_mat
            p_mat = n_mat
            for _ in range(n_scan - 1):
                p_mat = _dot(p_mat, p_mat)
                t_mat = t_mat + _dot(t_mat, p_mat)
            e_col = jnp.exp(col)
            rhs = jnp.concatenate([bcol * v, (bcol * e_col) * k], axis=-1)
            sol = _dot(t_mat, rhs)
            u0 = sol[:, :DN_DV]
            w = sol[:, DN_DV:]
            qk = _dot_nt(q, k) * decay
            g_last = gc[chunk - 1:chunk, h:h + 1]
            q_dec = q * e_col
            k_dec = k * jnp.exp(g_last - col)
            s_old = state[nb, h]
            ws = _dot(jnp.concatenate([w, q_dec], axis=0), s_old)
            uu = u0 - ws[:chunk]
            o = ws[chunk:] + _dot(qk, uu)
            state[nb, h] = jnp.exp(g_last) * s_old + _dot_tn(k_dec, uu)
            o = _rms(o, ng_ref[...])
            zz = z_ref[nb, rows, pl.ds(h * DN_DV, DN_DV)]
            o_ref[nb, rows, pl.ds(h * DN_DV, DN_DV)] = (o * _silu(zz)).astype(o_ref.dtype)
        return carry

    lax.fori_loop(0, nb_blk * nc, chunk_body, 0)

    @pl.when(l == pl.num_programs(1) - 1)
    def _():
        sfin_ref[...] = state[...]


def _deltanet(u, z, b, a, conv_w, alog, dtb, norm_g, conv_buf, s0, *, nb_blk, lb, chunk, valid_len):
    bsz, seq, _ = u.shape
    nc = lb // chunk
    kern = functools.partial(_dn_kernel, nb_blk=nb_blk, lb=lb, chunk=chunk, valid_len=valid_len)
    tok = lambda w: pl.BlockSpec((nb_blk, lb, w), lambda i, l: (i, l, 0))
    full2 = lambda r, w: pl.BlockSpec((r, w), lambda i, l: (0, 0))
    return pl.pallas_call(
        kern,
        grid=(bsz // nb_blk, seq // lb),
        in_specs=[
            tok(CONV_DIM), tok(DN_V_W), tok(LANES), tok(LANES),
            full2(DN_CONV, CONV_DIM), full2(1, LANES), full2(1, LANES), full2(1, DN_DV),
            pl.BlockSpec((nb_blk, DN_CONV - 1, CONV_DIM), lambda i, l: (i, 0, 0)),
            pl.BlockSpec((nb_blk, DN_HEADS, DN_DK, DN_DV), lambda i, l: (i, 0, 0, 0)),
        ],
        out_specs=[
            tok(DN_V_W),
            pl.BlockSpec((nb_blk, DN_HEADS, DN_DK, DN_DV), lambda i, l: (i, 0, 0, 0)),
        ],
        out_shape=[
            jax.ShapeDtypeStruct((bsz, seq, DN_V_W), F32),
            jax.ShapeDtypeStruct((bsz, DN_HEADS, DN_DK, DN_DV), F32),
        ],
        scratch_shapes=[
            pltpu.VMEM((nb_blk, lb + SUBLANES, CONV_DIM), F32),
            pltpu.VMEM((nb_blk, lb, CONV_DIM), F32),
            pltpu.VMEM((nb_blk, lb, LANES), F32),
            pltpu.VMEM((nb_blk, lb, LANES), F32),
            pltpu.VMEM((nb_blk, nc, LANES, chunk), F32),
            pltpu.VMEM((nb_blk, DN_HEADS, DN_DK, DN_DV), F32),
        ],
        compiler_params=_cparams(("arbitrary", "arbitrary")),
        name="deltanet",
    )(u, z, b, a, conv_w, alog, dtb, norm_g, conv_buf, s0)


def _swa_prompt_kernel(sink_ref, q_ref, kvp_ref, kvc_ref, o_ref):
    i = pl.program_id(1)
    q = q_ref[0]
    kvp = kvp_ref[0]
    kvc = kvc_ref[0]
    kcat = jnp.concatenate([kvp[:, :SW_KV_W], kvc[:, :SW_KV_W]], axis=0).astype(BF16)
    vcat = jnp.concatenate([kvp[:, SW_KV_W:], kvc[:, SW_KV_W:]], axis=0).astype(BF16)
    row = lax.broadcasted_iota(jnp.int32, (WINDOW, 2 * WINDOW), 0)
    col = lax.broadcasted_iota(jnp.int32, (WINDOW, 2 * WINDOW), 1)
    valid = (col > row) & (col <= row + WINDOW) & ((col >= WINDOW) | (i > 0))
    for qh in range(SW_QHEADS):
        kv = qh // SW_GROUP
        qs = q[:, qh * SW_HD:(qh + 1) * SW_HD].astype(BF16)
        ks = kcat[:, kv * SW_HD:(kv + 1) * SW_HD]
        vs = vcat[:, kv * SW_HD:(kv + 1) * SW_HD]
        s = _dot_nt(qs, ks) * (SW_HD ** -0.5)
        s = jnp.where(valid, s, NEG_INF)
        sink = sink_ref[qh]
        m = jnp.maximum(jnp.max(s, axis=-1, keepdims=True), sink)
        p = jnp.exp(s - m)
        den = jnp.sum(p, axis=-1, keepdims=True) + jnp.exp(sink - m)
        o = _dot(p.astype(BF16), vs) / den
        o_ref[0, :, qh * SW_HD:(qh + 1) * SW_HD] = o.astype(o_ref.dtype)


def _swa_prompt(sq, skv, sinks):
    bsz, seq, _ = sq.shape
    nblk = seq // WINDOW
    return pl.pallas_call(
        _swa_prompt_kernel,
        grid=(bsz, nblk),
        in_specs=[
            pl.BlockSpec(memory_space=pltpu.SMEM),
            pl.BlockSpec((1, WINDOW, SW_Q_W), lambda b, i: (b, i, 0)),
            pl.BlockSpec((1, WINDOW, 2 * SW_KV_W), lambda b, i: (b, jnp.maximum(i - 1, 0), 0)),
            pl.BlockSpec((1, WINDOW, 2 * SW_KV_W), lambda b, i: (b, i, 0)),
        ],
        out_specs=pl.BlockSpec((1, WINDOW, SW_Q_W), lambda b, i: (b, i, 0)),
        out_shape=jax.ShapeDtypeStruct((bsz, seq, SW_Q_W), F32),
        compiler_params=_cparams(("arbitrary", "arbitrary")),
        name="swa_prompt",
    )(sinks, sq, skv, skv)


def _swa_decode_kernel(sink_ref, q_ref, kvn_ref, kb_ref, vb_ref, o_ref, ko_ref, vo_ref, *, n_new):
    q = q_ref[...]
    kvn = kvn_ref[...]
    kb = kb_ref[...]
    vb = vb_ref[...]
    nb = q.shape[0]
    p_len = kb.shape[1]
    kn = kvn[:, :, :SW_KV_W]
    vn = kvn[:, :, SW_KV_W:]
    ko_ref[:, :p_len - n_new, :] = kb[:, n_new:, :]
    ko_ref[:, p_len - n_new:, :] = kn[:, :n_new, :]
    vo_ref[:, :p_len - n_new, :] = vb[:, n_new:, :]
    vo_ref[:, p_len - n_new:, :] = vn[:, :n_new, :]
    qi = lax.broadcasted_iota(jnp.int32, (nb, DEC_PAD, p_len), 1)
    kj = lax.broadcasted_iota(jnp.int32, (nb, DEC_PAD, p_len), 2)
    valid_buf = (p_len + qi - kj) < WINDOW
    qi2 = lax.broadcasted_iota(jnp.int32, (nb, DEC_PAD, DEC_PAD), 1)
    kj2 = lax.broadcasted_iota(jnp.int32, (nb, DEC_PAD, DEC_PAD), 2)
    valid_new = (kj2 <= qi2) & (kj2 < n_new)
    for qh in range(SW_QHEADS):
        kv = qh // SW_GROUP
        hs = slice(kv * SW_HD, (kv + 1) * SW_HD)
        qs = q[:, :, qh * SW_HD:(qh + 1) * SW_HD]
        sb = jnp.einsum("bqd,bkd->bqk", qs, kb[:, :, hs], preferred_element_type=F32) * (SW_HD ** -0.5)
        sn = jnp.einsum("bqd,bkd->bqk", qs, kn[:, :, hs], preferred_element_type=F32) * (SW_HD ** -0.5)
        sb = jnp.where(valid_buf, sb, NEG_INF)
        sn = jnp.where(valid_new, sn, NEG_INF)
        sink = sink_ref[qh]
        m = jnp.maximum(jnp.maximum(jnp.max(sb, axis=-1, keepdims=True),
                                    jnp.max(sn, axis=-1, keepdims=True)), sink)
        pb = jnp.exp(sb - m)
        pn = jnp.exp(sn - m)
        den = jnp.sum(pb, axis=-1, keepdims=True) + jnp.sum(pn, axis=-1, keepdims=True) + jnp.exp(sink - m)
        o = (jnp.einsum("bqk,bkd->bqd", pb, vb[:, :, hs], preferred_element_type=F32)
             + jnp.einsum("bqk,bkd->bqd", pn, vn[:, :, hs], preferred_element_type=F32))
        o_ref[:, :, qh * SW_HD:(qh + 1) * SW_HD] = (o / den).astype(o_ref.dtype)


def _swa_decode(sq, skv, k_buf, v_buf, sinks, *, nb_blk, n_new):
    bsz = sq.shape[0]
    p_len = k_buf.shape[1]
    tok = lambda w: pl.BlockSpec((nb_blk, DEC_PAD, w), lambda i: (i, 0, 0))
    cache = pl.BlockSpec((nb_blk, p_len, SW_KV_W), lambda i: (i, 0, 0))
    return pl.pallas_call(
        functools.partial(_swa_decode_kernel, n_new=n_new),
        grid=(bsz // nb_blk,),
        in_specs=[pl.BlockSpec(memory_space=pltpu.SMEM), tok(SW_Q_W), tok(2 * SW_KV_W), cache, cache],
        out_specs=[tok(SW_Q_W), cache, cache],
        out_shape=[
            jax.ShapeDtypeStruct((bsz, DEC_PAD, SW_Q_W), F32),
            jax.ShapeDtypeStruct((bsz, p_len, SW_KV_W), F32),
            jax.ShapeDtypeStruct((bsz, p_len, SW_KV_W), F32),
        ],
        compiler_params=_cparams(("arbitrary",)),
        name="swa_decode",
    )(sinks, sq, skv, k_buf, v_buf)


def _outproj_kernel(x_ref, odn_ref, osw_ref, g1_ref, sh2_ref, sc2_ref, n2_ref, w_ref, xo_ref, h2_ref):
    mix = (_dot(odn_ref[...].astype(BF16), w_ref[:DN_V_W, :])
           + _dot(osw_ref[...].astype(BF16), w_ref[DN_V_W:, :]))
    x = x_ref[...] + g1_ref[0] * mix
    xo_ref[...] = x
    h2 = _rms(x, n2_ref[...]) * (1.0 + sc2_ref[0]) + sh2_ref[0]
    h2_ref[...] = h2.astype(h2_ref.dtype)


def _outproj(x, o_dn, o_sw, mod, tiles_per_group, norm_g, w_out, tt):
    t = x.shape[0]
    rows = mod.shape[1]
    return pl.pallas_call(
        _outproj_kernel,
        grid=(t // tt,),
        in_specs=[
            pl.BlockSpec((tt, D_MODEL), lambda i: (i, 0)),
            pl.BlockSpec((tt, DN_V_W), lambda i: (i, 0)),
            pl.BlockSpec((tt, SW_Q_W), lambda i: (i, 0)),
            _mod_spec(2, tiles_per_group, rows),
            _mod_spec(3, tiles_per_group, rows),
            _mod_spec(4, tiles_per_group, rows),
            pl.BlockSpec((1, D_MODEL), lambda i: (0, 0)),
            pl.BlockSpec((D_MODEL, D_MODEL), lambda i: (0, 0)),
        ],
        out_specs=[pl.BlockSpec((tt, D_MODEL), lambda i: (i, 0))] * 2,
        out_shape=[jax.ShapeDtypeStruct((t, D_MODEL), F32), jax.ShapeDtypeStruct((t, D_MODEL), BF16)],
        compiler_params=_cparams(("arbitrary",)),
        name="outproj",
    )(x, o_dn, o_sw, mod, mod, mod, norm_g.reshape(1, D_MODEL), w_out)


def _top_values(work, n):
    out = []
    for it in range(n):
        m = jnp.max(work, axis=0, keepdims=True)
        out.append(m)
        if it + 1 < n:
            work = jnp.where(work == m, NEG_INF, work)
    return out


def _peer_kernel(h2_ref, x_ref, g2_ref, wq_ref, keys_ref, u_ref, vt_ref, fg_ref, o_ref,
                 s_s, thr_s, e1_s, e2_s, tv_s, acc, a_s0, a_s1, p_s0, p_s1, *, tt, et, final_norm):
    j = pl.program_id(1)
    lg_n = tt // LANES

    @pl.when(j == 0)
    def _():
        acc[...] = jnp.zeros_like(acc)
        tv_s[...] = jnp.full(tv_s.shape, NEG_INF, F32)
        q = _dot(h2_ref[...], wq_ref[...]).astype(BF16)
        for hp in range(2 * PK_HEADS):
            s_t = _dot_nt(keys_ref[hp], q[:, hp * PK_HALF:(hp + 1) * PK_HALF])
            for lg in range(lg_n):
                s_s[hp, lg] = s_t[:, lg * LANES:(lg + 1) * LANES]

        def head_body(idx, carry):
            h = idx // lg_n
            lg = idx % lg_n
            s1 = s_s[2 * h, lg]
            s2 = s_s[2 * h + 1, lg]
            t1 = _top_values(s1, N_TOP)
            t2 = _top_values(s2, N_TOP)
            for it in range(N_TOP):
                tv_s[it:it + 1, :] = t2[it]
            rid = lax.broadcasted_iota(jnp.int32, (SUBLANES, LANES), 0)
            cands = []
            for i in range(N_TOP):
                n_i = N_TOP // (i + 1)
                for r0 in range(0, n_i, SUBLANES):
                    blk = t1[i] + tv_s[r0:r0 + SUBLANES, :]
                    if n_i - r0 < SUBLANES:
                        blk = jnp.where(rid < n_i - r0, blk, NEG_INF)
                    cands.append(blk)
            cand = jnp.concatenate(cands, axis=0)
            tops = _top_values(cand, N_TOP)
            thr = 0.5 * (tops[PK_TOPK - 1] + tops[PK_TOPK])
            m1 = t1[0]
            m2 = t2[0]
            zsum = jnp.sum(jnp.where(cand >= thr, jnp.exp(cand - (m1 + m2)), 0.0), axis=0, keepdims=True)
            thr_s[h, lg] = thr - s1
            e1_s[h, lg] = jnp.exp(s1 - m1) * (0.5 / zsum)
            e2_s[h, lg] = jnp.exp(s2 - m2)
            return carry

        lax.fori_loop(0, PK_HEADS * lg_n, head_body, 0)

    n_sb = et // PEER_SUB

    a_bufs = (a_s0, a_s1)
    p_bufs = (p_s0, p_s1)

    n_rr = PEER_SUB // N_KEYS

    def score_piece(sb, mm):
        def run():
            rows = slice(mm * PEER_MM, (mm + 1) * PEER_MM)
            a_bufs[sb % 2][rows, :] = _dot_nt(
                u_ref[sb * PEER_SUB + mm * PEER_MM:sb * PEER_SUB + (mm + 1) * PEER_MM, :], h2_ref[...])
        return run

    def mix_piece(sb, dd):
        def run():
            rows = slice(dd * PEER_MM, (dd + 1) * PEER_MM)
            acc[rows, :] += _dot(vt_ref[rows, sb * PEER_SUB:(sb + 1) * PEER_SUB], p_bufs[sb % 2][...])
        return run

    def gate_block(sb, lg, ch):
        def run():
            c_rows = slice(ch * PEER_CHUNK, (ch + 1) * PEER_CHUNK)
            cols = slice(lg * LANES, (lg + 1) * LANES)
            g = [None] * n_rr
            for h in range(PK_HEADS):
                s2 = s_s[2 * h + 1, lg, c_rows, :]
                e2 = e2_s[h, lg, c_rows, :]
                for rr in range(n_rr):
                    r = j * (et // N_KEYS) + sb * n_rr + rr
                    thr_r = thr_s[h, lg, pl.ds(r, 1), :]
                    e1_r = e1_s[h, lg, pl.ds(r, 1), :]
                    term = jnp.where(s2 >= thr_r, e2, 0.0) * e1_r
                    g[rr] = term if g[rr] is None else g[rr] + term
            for rr in range(n_rr):
                rows = slice(rr * N_KEYS + ch * PEER_CHUNK, rr * N_KEYS + (ch + 1) * PEER_CHUNK)
                a_t = a_bufs[sb % 2][rows, cols]
                act = a_t * (1.0 + lax.erf(a_t * (2.0 ** -0.5)))
                p_bufs[sb % 2][rows, cols] = (act * g[rr]).astype(BF16)
        return run

    for mm in range(PEER_SUB // PEER_MM):
        score_piece(0, mm)()
    for sb in range(n_sb + 1):
        pieces = []
        if sb + 1 < n_sb:
            pieces += [score_piece(sb + 1, mm) for mm in range(PEER_SUB // PEER_MM)]
        if sb >= 1:
            pieces += [mix_piece(sb - 1, dd) for dd in range(D_MODEL // PEER_MM)]
        blocks = ([gate_block(sb, lg, ch) for lg in range(lg_n) for ch in range(N_KEYS // PEER_CHUNK)]
                  if sb < n_sb else [])
        n_slots = max(len(blocks), 1)
        for i in range(n_slots):
            lo = (i * len(pieces)) // n_slots
            hi = ((i + 1) * len(pieces)) // n_slots
            for piece in pieces[lo:hi]:
                piece()
            if blocks:
                blocks[i]()

    @pl.when(j == pl.num_programs(1) - 1)
    def _():
        x = x_ref[...] + g2_ref[0] * acc[...].T
        if final_norm:
            x = _rms(x, fg_ref[...])
        o_ref[...] = x


def _peer(h2, x, mod, tiles_per_group, wq, keys, u_tab, vt_tab, final_g, *, tt, et, final_norm):
    t = x.shape[0]
    rows = mod.shape[1]
    lg_n = tt // LANES
    kern = functools.partial(_peer_kernel, tt=tt, et=et, final_norm=final_norm)
    return pl.pallas_call(
        kern,
        grid=(t // tt, N_EXPERTS // et),
        in_specs=[
            pl.BlockSpec((tt, D_MODEL), lambda i, j: (i, 0)),
            pl.BlockSpec((tt, D_MODEL), lambda i, j: (i, 0)),
            pl.BlockSpec((1, rows, D_MODEL), lambda i, j: (i // tiles_per_group, 0, 5)),
            pl.BlockSpec((D_MODEL, PK_HEADS * PK_QDIM), lambda i, j: (0, 0)),
            pl.BlockSpec((2 * PK_HEADS, N_KEYS, PK_HALF), lambda i, j: (0, 0, 0)),
            pl.BlockSpec((et, D_MODEL), lambda i, j: (j, 0)),
            pl.BlockSpec((D_MODEL, et), lambda i, j: (0, j)),
            pl.BlockSpec((1, D_MODEL), lambda i, j: (0, 0)),
        ],
        out_specs=pl.BlockSpec((tt, D_MODEL), lambda i, j: (i, 0)),
        out_shape=jax.ShapeDtypeStruct((t, D_MODEL), F32),
        scratch_shapes=[
            pltpu.VMEM((2 * PK_HEADS, lg_n, N_KEYS, LANES), F32),
            pltpu.VMEM((PK_HEADS, lg_n, N_KEYS, LANES), F32),
            pltpu.VMEM((PK_HEADS, lg_n, N_KEYS, LANES), F32),
            pltpu.VMEM((PK_HEADS, lg_n, N_KEYS, LANES), F32),
            pltpu.VMEM((3 * SUBLANES, LANES), F32),
            pltpu.VMEM((D_MODEL, tt), F32),
            pltpu.VMEM((PEER_SUB, tt), F32),
            pltpu.VMEM((PEER_SUB, tt), F32),
            pltpu.VMEM((PEER_SUB, tt), BF16),
            pltpu.VMEM((PEER_SUB, tt), BF16),
        ],
        compiler_params=_cparams(("arbitrary", "arbitrary")),
        name="peer",
    )(h2, x, mod, wq, keys, u_tab, vt_tab, final_g.reshape(1, D_MODEL))


def _pad_lanes(v):
    return jnp.pad(v, ((0, 0), (0, LANES - v.shape[-1])))


def _prep_w_in(w):
    return jnp.concatenate(
        [w[:, :OFF_B], _pad_lanes(w[:, OFF_B:OFF_A]), _pad_lanes(w[:, OFF_A:OFF_SQ]), w[:, OFF_SQ:]],
        axis=1).astype(BF16)


def _layer(l, x, mod, tiles_per_group, prior, p, *, seq, tt, final_norm, final_g):
    t = x.shape[0]
    bsz = t // seq
    qkv, z, bcol, acol, sq, skv = _inproj(x, mod, tiles_per_group, p["norm1_g"][l], p["w_in"][l], tt)
    r3 = lambda a: a.reshape(bsz, seq, a.shape[-1])
    qkv3 = r3(qkv)
    if prior is None:
        conv_buf = jnp.zeros((bsz, DN_CONV - 1, CONV_DIM), F32)
        s0 = jnp.zeros((bsz, DN_HEADS, DN_DK, DN_DV), F32)
        dn_cfg = dict(nb_blk=1, lb=256, chunk=DN_CHUNK, valid_len=256)
    else:
        s0, conv_buf, k_buf, v_buf, n_new = prior
        dn_cfg = dict(nb_blk=8, lb=DEC_PAD, chunk=DEC_PAD, valid_len=n_new)
    o_dn, s_new = _deltanet(qkv3, r3(z), r3(bcol), r3(acol), p["conv_w"][l], p["alog"][l], p["dtb"][l],
                            p["dn_norm_g"][l], conv_buf, s0, **dn_cfg)
    if prior is None:
        o_sw = _swa_prompt(r3(sq), r3(skv), p["sw_sinks"][l])
        new_conv = qkv3[:, seq - (DN_CONV - 1):, :]
        n_buf = WINDOW
        new_k = r3(skv)[:, seq - n_buf:, :SW_KV_W].reshape(bsz, n_buf, SW_KVHEADS, SW_HD)
        new_v = r3(skv)[:, seq - n_buf:, SW_KV_W:].reshape(bsz, n_buf, SW_KVHEADS, SW_HD)
    else:
        p_len = k_buf.shape[1]
        o_sw, new_k, new_v = _swa_decode(r3(sq), r3(skv), k_buf.reshape(bsz, p_len, SW_KV_W),
                                         v_buf.reshape(bsz, p_len, SW_KV_W), p["sw_sinks"][l],
                                         nb_blk=16, n_new=n_new)
        new_conv = jnp.concatenate([conv_buf, qkv3[:, :n_new]], axis=1)[:, n_new:]
        new_k = new_k.reshape(bsz, p_len, SW_KVHEADS, SW_HD)
        new_v = new_v.reshape(bsz, p_len, SW_KVHEADS, SW_HD)
    x, h2 = _outproj(x, o_dn.reshape(t, DN_V_W), o_sw.reshape(t, SW_Q_W), mod, tiles_per_group,
                     p["norm2_g"][l], p["w_out"][l], tt)
    x = _peer(h2, x, mod, tiles_per_group, p["wq"][l], p["keys"][l], p["u"][l], p["vt"][l], final_g,
              tt=tt, et=1024, final_norm=final_norm)
    return x, (s_new, new_conv, new_k, new_v)


def _trunk(x, mod_all, tiles_per_group, prior_stack, p, final_g, *, seq, tt):
    new = []
    for l in range(DEPTH):
        prior = None if prior_stack is None else tuple(s[l] for s in prior_stack[:4]) + (prior_stack[4],)
        x, st = _layer(l, x, mod_all[l], tiles_per_group, prior, p, seq=seq, tt=tt,
                       final_norm=(l == DEPTH - 1), final_g=final_g)
        new.append(st)
    return x, [jnp.stack(zz) for zz in zip(*new)]


def kernel(x_prompt, x_sample, state_delta, state_conv, cache_swa_k, cache_swa_v, c_prompt, c_sample,
           norm1_g, norm2_g, final_norm_g, w_ada, b_ada, w_in, conv_w, dn_a_log, dn_dt_bias, dn_norm_g,
           sw_sinks, w_out, peer_wq, peer_keys, peer_u, peer_v):
    batch, seq, _ = x_prompt.shape
    dec_b, dec_l, _ = x_sample.shape
    tt = 512

    p = dict(
        norm1_g=norm1_g, norm2_g=norm2_g,
        w_in=jax.vmap(_prep_w_in)(w_in),
        conv_w=conv_w,
        alog=_pad_lanes(dn_a_log).reshape(DEPTH, 1, LANES),
        dtb=_pad_lanes(dn_dt_bias).reshape(DEPTH, 1, LANES),
        dn_norm_g=dn_norm_g.reshape(DEPTH, 1, DN_DV),
        sw_sinks=sw_sinks,
        w_out=w_out.astype(BF16),
        wq=peer_wq.astype(BF16),
        keys=peer_keys.reshape(DEPTH, 2 * PK_HEADS, N_KEYS, PK_HALF).astype(BF16),
        u=peer_u.astype(BF16),
        vt=jnp.swapaxes(peer_v, 1, 2).astype(BF16),
    )

    c_dec = jnp.repeat(c_sample, DEC_PAD, axis=0)
    n_rows = batch + dec_b * DEC_PAD
    pad_rows = (-n_rows) % SUBLANES
    c_all = jnp.concatenate([c_prompt, c_dec, jnp.zeros((pad_rows, D_MODEL), F32)], axis=0)
    mod = _ada_mod(c_all, w_ada, b_ada)
    mod_p = mod[:, :batch].reshape(DEPTH, batch, 1, 6 * D_MODEL)
    mod_s = mod[:, batch:n_rows].reshape(DEPTH, dec_b * DEC_PAD // tt, tt, 6 * D_MODEL)

    xp = x_prompt.reshape(batch * seq, D_MODEL)
    yp, (delta_p, conv_p, k_p, v_p) = _trunk(xp, mod_p, seq // tt, None, p, final_norm_g, seq=seq, tt=tt)

    xs = jnp.pad(x_sample, ((0, 0), (0, DEC_PAD - dec_l), (0, 0))).reshape(dec_b * DEC_PAD, D_MODEL)
    prior = (state_delta, state_conv, cache_swa_k, cache_swa_v, dec_l)
    ys, (delta_s, conv_s, k_s, v_s) = _trunk(xs, mod_s, 1, prior, p, final_norm_g, seq=DEC_PAD, tt=tt)

    y_prompt = yp.reshape(batch, seq, D_MODEL)
    y_sample = ys.reshape(dec_b, DEC_PAD, D_MODEL)[:, :dec_l]
    return (y_prompt, y_sample, delta_p, conv_p, k_p, v_p, delta_s, conv_s, k_s, v_s)
```

```python
import functools

import jax
import jax.numpy as jnp
from jax import lax
from jax.experimental import pallas as pl
from jax.experimental.pallas import tpu as pltpu

F32 = jnp.float32
BF16 = jnp.bfloat16

D_MODEL = 1024
DEPTH = 4
DN_HEADS = 4
DN_DK = 128
DN_DV = 128
DN_CONV = 4
DN_CHUNK = 64
SW_QHEADS = 8
SW_KVHEADS = 2
SW_HD = 64
SW_GROUP = SW_QHEADS // SW_KVHEADS
WINDOW = 128
DN_QK_W = DN_HEADS * DN_DK
DN_V_W = DN_HEADS * DN_DV
CONV_DIM = 2 * DN_QK_W + DN_V_W
SW_Q_W = SW_QHEADS * SW_HD
SW_KV_W = SW_KVHEADS * SW_HD
OFF_Z = CONV_DIM
OFF_B = OFF_Z + DN_V_W
OFF_A = OFF_B + DN_HEADS
OFF_SQ = OFF_A + DN_HEADS
PK_HEADS = 8
N_KEYS = 128
N_EXPERTS = N_KEYS * N_KEYS
PK_QDIM = 256
PK_HALF = PK_QDIM // 2
PK_TOPK = 16
EPS = 1e-6

LANES = 128
SUBLANES = 8
VMEM_LIMIT = 56 * 1024 * 1024
PROJ_W = CONV_DIM + DN_V_W + 2 * LANES + SW_Q_W + 2 * SW_KV_W
DEC_PAD = SUBLANES
N_TOP = PK_TOPK + 1
DN_INV_BLOCK = SUBLANES
NEG_INF = float("-inf")


def _cparams(sem):
    return pltpu.CompilerParams(dimension_semantics=sem, vmem_limit_bytes=VMEM_LIMIT)


def _silu(x):
    return x * jax.nn.sigmoid(x)


def _softplus(x):
    return jnp.maximum(x, 0.0) + jnp.log1p(jnp.exp(-jnp.abs(x)))


def _rms(x, g):
    return x * lax.rsqrt(jnp.mean(x * x, axis=-1, keepdims=True) + EPS) * g


def _dot(a, b):
    return jnp.dot(a, b, preferred_element_type=F32)


def _dot_nt(a, b):
    return lax.dot_general(a, b, (((1,), (1,)), ((), ())), preferred_element_type=F32)


def _dot_tn(a, b):
    return lax.dot_general(a, b, (((0,), (0,)), ((), ())), preferred_element_type=F32)


def _split_bf16(v):
    hi = v.astype(BF16)
    lo = (v - hi.astype(F32)).astype(BF16)
    return hi, lo


def _dot3(a, b, dot=_dot):
    a_hi, a_lo = _split_bf16(a)
    b_hi, b_lo = _split_bf16(b)
    return dot(a_hi, b_hi) + (dot(a_hi, b_lo) + dot(a_lo, b_hi))


def _ada_kernel(c_ref, w_ref, b_ref, o_ref):
    a = _silu(c_ref[...]).astype(BF16)
    o_ref[0] = _dot(a, w_ref[0].astype(BF16)) + b_ref[0]


def _ada_mod(c_all, w_ada, b_ada):
    rows = c_all.shape[0]
    tn = 1536
    return pl.pallas_call(
        _ada_kernel,
        grid=(DEPTH, 6 * D_MODEL // tn),
        in_specs=[
            pl.BlockSpec((rows, D_MODEL), lambda l, j: (0, 0)),
            pl.BlockSpec((1, D_MODEL, tn), lambda l, j: (l, 0, j)),
            pl.BlockSpec((1, 1, tn), lambda l, j: (l, 0, j)),
        ],
        out_specs=pl.BlockSpec((1, rows, tn), lambda l, j: (l, 0, j)),
        out_shape=jax.ShapeDtypeStruct((DEPTH, rows, 6 * D_MODEL), F32),
        compiler_params=_cparams(("arbitrary", "arbitrary")),
        name="ada_mod",
    )(c_all, w_ada, b_ada.reshape(DEPTH, 1, 6 * D_MODEL))


def _mod_spec(k, tiles_per_group, rows):
    return pl.BlockSpec((1, rows, D_MODEL), lambda i: (i // tiles_per_group, 0, k))


def _inproj_kernel(x_ref, sh_ref, sc_ref, g_ref, w_ref,
                   qkv_ref, z_ref, b_ref, a_ref, sq_ref, skv_ref):
    h = _rms(x_ref[...], g_ref[...]) * (1.0 + sc_ref[0]) + sh_ref[0]
    p = _dot(h.astype(BF16), w_ref[...])
    o = 0
    for ref in (qkv_ref, z_ref, b_ref, a_ref, sq_ref, skv_ref):
        w = ref.shape[-1]
        ref[...] = p[:, o:o + w]
        o += w


def _inproj(x, mod, tiles_per_group, norm_g, w_r, tt):
    t = x.shape[0]
    rows = mod.shape[1]
    widths = (CONV_DIM, DN_V_W, LANES, LANES, SW_Q_W, 2 * SW_KV_W)
    return pl.pallas_call(
        _inproj_kernel,
        grid=(t // tt,),
        in_specs=[
            pl.BlockSpec((tt, D_MODEL), lambda i: (i, 0)),
            _mod_spec(0, tiles_per_group, rows),
            _mod_spec(1, tiles_per_group, rows),
            pl.BlockSpec((1, D_MODEL), lambda i: (0, 0)),
            pl.BlockSpec((D_MODEL, PROJ_W), lambda i: (0, 0)),
        ],
        out_specs=[pl.BlockSpec((tt, w), lambda i: (i, 0)) for w in widths],
        out_shape=[jax.ShapeDtypeStruct((t, w), F32) for w in widths],
        compiler_params=_cparams(("arbitrary",)),
        name="inproj",
    )(x, mod, mod, norm_g.reshape(1, D_MODEL), w_r)


def _dn_kernel(u_ref, z_ref, b_ref, a_ref, cw_ref, alog_ref, dtb_ref, ng_ref, cbuf_ref, s0_ref,
               o_ref, sfin_ref, ubuf, qkv, beta_s, gc_s, gct_s, state,
               *, nb_blk, lb, chunk, valid_len):
    l = pl.program_id(1)
    nc = lb // chunk
    n_scan = chunk.bit_length() - 1

    @pl.when(l == 0)
    def _():
        state[...] = s0_ref[...]
        ubuf[:, SUBLANES - (DN_CONV - 1):SUBLANES, :] = cbuf_ref[...]

    ubuf[:, SUBLANES:SUBLANES + lb, :] = u_ref[...]
    y = None
    for i in range(DN_CONV):
        start = SUBLANES - (DN_CONV - 1) + i
        term = ubuf[:, start:start + lb, :] * cw_ref[i:i + 1, :]
        y = term if y is None else y + term
    qkv[...] = _silu(y)
    ubuf[:, SUBLANES - (DN_CONV - 1):SUBLANES, :] = ubuf[:, lb + SUBLANES - (DN_CONV - 1):lb + SUBLANES, :]

    row = lax.broadcasted_iota(jnp.int32, (lb, LANES), 0)
    for nb in range(nb_blk):
        beta = jax.nn.sigmoid(b_ref[nb])
        g = -jnp.exp(alog_ref[...]) * _softplus(a_ref[nb] + dtb_ref[...])
        if valid_len < lb:
            beta = jnp.where(row < valid_len, beta, 0.0)
            g = jnp.where(row < valid_len, g, 0.0)
        gc = g
        for s in range(n_scan):
            sh = 1 << s
            gc = gc + jnp.where((row % chunk) >= sh, pltpu.roll(gc, sh, 0), 0.0)
        beta_s[nb] = beta
        gc_s[nb] = gc
        for r0 in range(0, lb, LANES):
            nrow = min(LANES, lb - r0)
            blk = gc[r0:r0 + nrow]
            if nrow < LANES:
                blk = jnp.concatenate([blk, jnp.zeros((LANES - nrow, LANES), F32)], axis=0)
            blk_t = blk.T
            for c0 in range(0, nrow, chunk):
                gct_s[nb, (r0 + c0) // chunk] = blk_t[:, c0:c0 + chunk]

    ii = lax.broadcasted_iota(jnp.int32, (chunk, chunk), 0)
    jj = lax.broadcasted_iota(jnp.int32, (chunk, chunk), 1)
    causal = ii >= jj
    strict = ii > jj
    eye = jnp.where(ii == jj, 1.0, 0.0)

    items = [(nb, c, h) for nb in range(nb_blk) for c in range(nc) for h in range(DN_HEADS)]

    def rows_of(c):
        return slice(c * chunk, (c + 1) * chunk)

    def l2n(x):
        return x * lax.rsqrt(jnp.sum(x * x, axis=-1, keepdims=True) + EPS)

    qs = [l2n(qkv[nb, rows_of(c), h * DN_DK:(h + 1) * DN_DK]) * (DN_DK ** -0.5) for nb, c, h in items]
    ks = [l2n(qkv[nb, rows_of(c), DN_QK_W + h * DN_DK:DN_QK_W + (h + 1) * DN_DK]) for nb, c, h in items]
    vs = [qkv[nb, rows_of(c), 2 * DN_QK_W + h * DN_DV:2 * DN_QK_W + (h + 1) * DN_DV] for nb, c, h in items]
    cols = [gc_s[nb, rows_of(c), h:h + 1] for nb, c, h in items]
    rws = [gct_s[nb, c, h:h + 1, :] for nb, c, h in items]
    bcols = [beta_s[nb, rows_of(c), h:h + 1] for nb, c, h in items]
    lasts = [gc_s[nb, (c + 1) * chunk - 1:(c + 1) * chunk, h:h + 1] for nb, c, h in items]
    decays = [jnp.where(causal, jnp.exp(jnp.where(causal, col - rw, 0.0)), 0.0) for col, rw in zip(cols, rws)]
    kks = [_dot3(k, k, _dot_nt) for k in ks]
    a_mats = [jnp.where(strict, bcol * kk * decay, 0.0) for bcol, kk, decay in zip(bcols, kks, decays)]
    base = min(DN_INV_BLOCK, chunk)
    same = lambda b: (ii // b) == (jj // b)
    p_mats = [jnp.where(same(base), -a, 0.0) for a in a_mats]
    t_mats = [eye + n_mat for n_mat in p_mats]
    for _ in range(base.bit_length() - 2):
        p_mats = [_dot3(p_mat, p_mat) for p_mat in p_mats]
        t_mats = [t_mat + _dot3(t_mat, p_mat) for t_mat, p_mat in zip(t_mats, p_mats)]
    b = base
    while b < chunk:
        offs = [jnp.where(same(2 * b) & jnp.logical_not(same(b)), a, 0.0) for a in a_mats]
        t_mats = [t_mat - _dot3(_dot3(t_mat, off), t_mat) for t_mat, off in zip(t_mats, offs)]
        b *= 2
    e_cols = [jnp.exp(col) for col in cols]
    sols = [_dot3(t_mat, jnp.concatenate([bcol * v, (bcol * e_col) * k], axis=-1))
            for t_mat, bcol, v, e_col, k in zip(t_mats, bcols, vs, e_cols, ks)]
    qks = [_dot_nt(q, k) * decay for q, k, decay in zip(qs, ks, decays)]
    wqs = [jnp.concatenate([sol[:, DN_DV:], q * e_col], axis=0) for sol, q, e_col in zip(sols, qs, e_cols)]
    k_decs = [k * jnp.exp(last - col) for k, last, col in zip(ks, lasts, cols)]

    for idx, (nb, c, h) in enumerate(items):
        s_old = state[nb, h]
        ws = _dot(wqs[idx], s_old)
        uu = sols[idx][:, :DN_DV] - ws[:chunk]
        o = ws[chunk:] + _dot(qks[idx], uu)
        state[nb, h] = jnp.exp(lasts[idx]) * s_old + _dot_tn(k_decs[idx], uu)
        o = _rms(o, ng_ref[...])
        zz = z_ref[nb, rows_of(c), h * DN_DV:(h + 1) * DN_DV]
        o_ref[nb, rows_of(c), h * DN_DV:(h + 1) * DN_DV] = (o * _silu(zz)).astype(o_ref.dtype)

    @pl.when(l == pl.num_programs(1) - 1)
    def _():
        sfin_ref[...] = state[...]


def _deltanet(u, z, b, a, conv_w, alog, dtb, norm_g, conv_buf, s0, *, nb_blk, lb, chunk, valid_len):
    bsz, seq, _ = u.shape
    nc = lb // chunk
    kern = functools.partial(_dn_kernel, nb_blk=nb_blk, lb=lb, chunk=chunk, valid_len=valid_len)
    tok = lambda w: pl.BlockSpec((nb_blk, lb, w), lambda i, l: (i, l, 0))
    full2 = lambda r, w: pl.BlockSpec((r, w), lambda i, l: (0, 0))
    return pl.pallas_call(
        kern,
        grid=(bsz // nb_blk, seq // lb),
        in_specs=[
            tok(CONV_DIM), tok(DN_V_W), tok(LANES), tok(LANES),
            full2(DN_CONV, CONV_DIM), full2(1, LANES), full2(1, LANES), full2(1, DN_DV),
            pl.BlockSpec((nb_blk, DN_CONV - 1, CONV_DIM), lambda i, l: (i, 0, 0)),
            pl.BlockSpec((nb_blk, DN_HEADS, DN_DK, DN_DV), lambda i, l: (i, 0, 0, 0)),
        ],
        out_specs=[
            tok(DN_V_W),
            pl.BlockSpec((nb_blk, DN_HEADS, DN_DK, DN_DV), lambda i, l: (i, 0, 0, 0)),
        ],
        out_shape=[
            jax.ShapeDtypeStruct((bsz, seq, DN_V_W), F32),
            jax.ShapeDtypeStruct((bsz, DN_HEADS, DN_DK, DN_DV), F32),
        ],
        scratch_shapes=[
            pltpu.VMEM((nb_blk, lb + SUBLANES, CONV_DIM), F32),
            pltpu.VMEM((nb_blk, lb, CONV_DIM), F32),
            pltpu.VMEM((nb_blk, lb, LANES), F32),
            pltpu.VMEM((nb_blk, lb, LANES), F32),
            pltpu.VMEM((nb_blk, nc, LANES, chunk), F32),
            pltpu.VMEM((nb_blk, DN_HEADS, DN_DK, DN_DV), F32),
        ],
        compiler_params=_cparams(("arbitrary", "arbitrary")),
        name="deltanet",
    )(u, z, b, a, conv_w, alog, dtb, norm_g, conv_buf, s0)


def _swa_prompt_kernel(sink_ref, q_ref, kvp_ref, kvc_ref, o_ref):
    i = pl.program_id(1)
    q = q_ref[0]
    kvp = kvp_ref[0]
    kvc = kvc_ref[0]
    kcat = jnp.concatenate([kvp[:, :SW_KV_W], kvc[:, :SW_KV_W]], axis=0).astype(BF16)
    vcat = jnp.concatenate([kvp[:, SW_KV_W:], kvc[:, SW_KV_W:]], axis=0).astype(BF16)
    row = lax.broadcasted_iota(jnp.int32, (WINDOW, 2 * WINDOW), 0)
    col = lax.broadcasted_iota(jnp.int32, (WINDOW, 2 * WINDOW), 1)
    valid = (col > row) & (col <= row + WINDOW) & ((col >= WINDOW) | (i > 0))
    for qh in range(SW_QHEADS):
        kv = qh // SW_GROUP
        qs = q[:, qh * SW_HD:(qh + 1) * SW_HD].astype(BF16)
        ks = kcat[:, kv * SW_HD:(kv + 1) * SW_HD]
        vs = vcat[:, kv * SW_HD:(kv + 1) * SW_HD]
        s = _dot_nt(qs, ks) * (SW_HD ** -0.5)
        s = jnp.where(valid, s, NEG_INF)
        sink = sink_ref[qh]
        m = jnp.maximum(jnp.max(s, axis=-1, keepdims=True), sink)
        p = jnp.exp(s - m)
        den = jnp.sum(p, axis=-1, keepdims=True) + jnp.exp(sink - m)
        o = _dot(p.astype(BF16), vs) / den
        o_ref[0, :, qh * SW_HD:(qh + 1) * SW_HD] = o.astype(o_ref.dtype)


def _swa_prompt(sq, skv, sinks):
    bsz, seq, _ = sq.shape
    nblk = seq // WINDOW
    return pl.pallas_call(
        _swa_prompt_kernel,
        grid=(bsz, nblk),
        in_specs=[
            pl.BlockSpec(memory_space=pltpu.SMEM),
            pl.BlockSpec((1, WINDOW, SW_Q_W), lambda b, i: (b, i, 0)),
            pl.BlockSpec((1, WINDOW, 2 * SW_KV_W), lambda b, i: (b, jnp.maximum(i - 1, 0), 0)),
            pl.BlockSpec((1, WINDOW, 2 * SW_KV_W), lambda b, i: (b, i, 0)),
        ],
        out_specs=pl.BlockSpec((1, WINDOW, SW_Q_W), lambda b, i: (b, i, 0)),
        out_shape=jax.ShapeDtypeStruct((bsz, seq, SW_Q_W), F32),
        compiler_params=_cparams(("arbitrary", "arbitrary")),
        name="swa_prompt",
    )(sinks, sq, skv, skv)


def _swa_decode_kernel(sink_ref, q_ref, kvn_ref, kb_ref, vb_ref, o_ref, ko_ref, vo_ref, *, n_new):
    q = q_ref[...]
    kvn = kvn_ref[...]
    kb = kb_ref[...]
    vb = vb_ref[...]
    nb = q.shape[0]
    p_len = kb.shape[1]
    kn = kvn[:, :, :SW_KV_W]
    vn = kvn[:, :, SW_KV_W:]
    ko_ref[:, :p_len - n_new, :] = kb[:, n_new:, :]
    ko_ref[:, p_len - n_new:, :] = kn[:, :n_new, :]
    vo_ref[:, :p_len - n_new, :] = vb[:, n_new:, :]
    vo_ref[:, p_len - n_new:, :] = vn[:, :n_new, :]
    qi = lax.broadcasted_iota(jnp.int32, (nb, DEC_PAD, p_len), 1)
    kj = lax.broadcasted_iota(jnp.int32, (nb, DEC_PAD, p_len), 2)
    valid_buf = (p_len + qi - kj) < WINDOW
    qi2 = lax.broadcasted_iota(jnp.int32, (nb, DEC_PAD, DEC_PAD), 1)
    kj2 = lax.broadcasted_iota(jnp.int32, (nb, DEC_PAD, DEC_PAD), 2)
    valid_new = (kj2 <= qi2) & (kj2 < n_new)
    for qh in range(SW_QHEADS):
        kv = qh // SW_GROUP
        hs = slice(kv * SW_HD, (kv + 1) * SW_HD)
        qs = q[:, :, qh * SW_HD:(qh + 1) * SW_HD]
        sb = jnp.einsum("bqd,bkd->bqk", qs, kb[:, :, hs], preferred_element_type=F32) * (SW_HD ** -0.5)
        sn = jnp.einsum("bqd,bkd->bqk", qs, kn[:, :, hs], preferred_element_type=F32) * (SW_HD ** -0.5)
        sb = jnp.where(valid_buf, sb, NEG_INF)
        sn = jnp.where(valid_new, sn, NEG_INF)
        sink = sink_ref[qh]
        m = jnp.maximum(jnp.maximum(jnp.max(sb, axis=-1, keepdims=True),
                                    jnp.max(sn, axis=-1, keepdims=True)), sink)
        pb = jnp.exp(sb - m)
        pn = jnp.exp(sn - m)
        den = jnp.sum(pb, axis=-1, keepdims=True) + jnp.sum(pn, axis=-1, keepdims=True) + jnp.exp(sink - m)
        o = (jnp.einsum("bqk,bkd->bqd", pb, vb[:, :, hs], preferred_element_type=F32)
             + jnp.einsum("bqk,bkd->bqd", pn, vn[:, :, hs], preferred_element_type=F32))
        o_ref[:, :, qh * SW_HD:(qh + 1) * SW_HD] = (o / den).astype(o_ref.dtype)


def _swa_decode(sq, skv, k_buf, v_buf, sinks, *, nb_blk, n_new):
    bsz = sq.shape[0]
    p_len = k_buf.shape[1]
    tok = lambda w: pl.BlockSpec((nb_blk, DEC_PAD, w), lambda i: (i, 0, 0))
    cache = pl.BlockSpec((nb_blk, p_len, SW_KV_W), lambda i: (i, 0, 0))
    return pl.pallas_call(
        functools.partial(_swa_decode_kernel, n_new=n_new),
        grid=(bsz // nb_blk,),
        in_specs=[pl.BlockSpec(memory_space=pltpu.SMEM), tok(SW_Q_W), tok(2 * SW_KV_W), cache, cache],
        out_specs=[tok(SW_Q_W), cache, cache],
        out_shape=[
            jax.ShapeDtypeStruct((bsz, DEC_PAD, SW_Q_W), F32),
            jax.ShapeDtypeStruct((bsz, p_len, SW_KV_W), F32),
            jax.ShapeDtypeStruct((bsz, p_len, SW_KV_W), F32),
        ],
        compiler_params=_cparams(("arbitrary",)),
        name="swa_decode",
    )(sinks, sq, skv, k_buf, v_buf)


def _outproj_kernel(x_ref, odn_ref, osw_ref, g1_ref, sh2_ref, sc2_ref, n2_ref, w_ref, xo_ref, h2_ref):
    mix = (_dot(odn_ref[...].astype(BF16), w_ref[:DN_V_W, :])
           + _dot(osw_ref[...].astype(BF16), w_ref[DN_V_W:, :]))
    x = x_ref[...] + g1_ref[0] * mix
    xo_ref[...] = x
    h2 = _rms(x, n2_ref[...]) * (1.0 + sc2_ref[0]) + sh2_ref[0]
    h2_ref[...] = h2.astype(h2_ref.dtype)


def _outproj(x, o_dn, o_sw, mod, tiles_per_group, norm_g, w_out, tt):
    t = x.shape[0]
    rows = mod.shape[1]
    return pl.pallas_call(
        _outproj_kernel,
        grid=(t // tt,),
        in_specs=[
            pl.BlockSpec((tt, D_MODEL), lambda i: (i, 0)),
            pl.BlockSpec((tt, DN_V_W), lambda i: (i, 0)),
            pl.BlockSpec((tt, SW_Q_W), lambda i: (i, 0)),
            _mod_spec(2, tiles_per_group, rows),
            _mod_spec(3, tiles_per_group, rows),
            _mod_spec(4, tiles_per_group, rows),
            pl.BlockSpec((1, D_MODEL), lambda i: (0, 0)),
            pl.BlockSpec((D_MODEL, D_MODEL), lambda i: (0, 0)),
        ],
        out_specs=[pl.BlockSpec((tt, D_MODEL), lambda i: (i, 0))] * 2,
        out_shape=[jax.ShapeDtypeStruct((t, D_MODEL), F32), jax.ShapeDtypeStruct((t, D_MODEL), F32)],
        compiler_params=_cparams(("arbitrary",)),
        name="outproj",
    )(x, o_dn, o_sw, mod, mod, mod, norm_g.reshape(1, D_MODEL), w_out)


def _top_values(work, n):
    out = []
    for it in range(n):
        m = jnp.max(work, axis=0, keepdims=True)
        out.append(m)
        if it + 1 < n:
            work = jnp.where(work == m, NEG_INF, work)
    return out


def _peer_kernel(h2_ref, x_ref, g2_ref, wq_ref, wql_ref, keys_ref, keysl_ref, u_ref, vt_ref, fg_ref, o_ref,
                 s_s, thr_s, e1_s, e2_s, tv_s, acc, hb_s, *, tt, et, final_norm):
    j = pl.program_id(1)
    lg_n = tt // LANES
    n_rr = et // N_KEYS

    @pl.when(j == 0)
    def _():
        acc[...] = jnp.zeros_like(acc)
        tv_s[...] = jnp.full(tv_s.shape, NEG_INF, F32)
        h2 = h2_ref[...]
        h_hi, h_lo = _split_bf16(h2)
        hb_s[...] = h_hi
        q = _dot(h_hi, wq_ref[...]) + (_dot(h_hi, wql_ref[...]) + _dot(h_lo, wq_ref[...]))
        q_hi, q_lo = _split_bf16(q)
        for hp in range(2 * PK_HEADS):
            cs = slice(hp * PK_HALF, (hp + 1) * PK_HALF)
            s_t = _dot_nt(keys_ref[hp], q_hi[:, cs]) + (
                _dot_nt(keys_ref[hp], q_lo[:, cs]) + _dot_nt(keysl_ref[hp], q_hi[:, cs]))
            for lg in range(lg_n):
                s_s[hp, lg] = s_t[:, lg * LANES:(lg + 1) * LANES]

        def head_body(idx, carry):
            h = idx // lg_n
            lg = idx % lg_n
            s1 = s_s[2 * h, lg]
            s2 = s_s[2 * h + 1, lg]
            t1 = _top_values(s1, N_TOP)
            t2 = _top_values(s2, N_TOP)
            for it in range(N_TOP):
                tv_s[it:it + 1, :] = t2[it]
            rid = lax.broadcasted_iota(jnp.int32, (SUBLANES, LANES), 0)
            cands = []
            for i in range(N_TOP):
                n_i = N_TOP // (i + 1)
                for r0 in range(0, n_i, SUBLANES):
                    blk = t1[i] + tv_s[r0:r0 + SUBLANES, :]
                    if n_i - r0 < SUBLANES:
                        blk = jnp.where(rid < n_i - r0, blk, NEG_INF)
                    cands.append(blk)
            cand = jnp.concatenate(cands, axis=0)
            tops = _top_values(cand, N_TOP)
            thr = 0.5 * (tops[PK_TOPK - 1] + tops[PK_TOPK])
            m1 = t1[0]
            m2 = t2[0]
            zsum = jnp.sum(jnp.where(cand >= thr, jnp.exp(cand - (m1 + m2)), 0.0), axis=0, keepdims=True)
            thr_s[h, lg] = thr - s1
            e1_s[h, lg] = jnp.exp(s1 - m1) * (0.5 / zsum)
            e2_s[h, lg] = jnp.exp(s2 - m2)
            return carry

        lax.fori_loop(0, PK_HEADS * lg_n, head_body, 0)

    a_t = _dot_nt(u_ref[...], hb_s[...])
    act = a_t * (1.0 + lax.erf(a_t * (2.0 ** -0.5)))
    blocks = []
    for rr in range(n_rr):
        r = j * n_rr + rr
        row_blocks = []
        for lg in range(lg_n):
            g = None
            for h in range(PK_HEADS):
                thr_r = thr_s[h, lg, pl.ds(r, 1), :]
                e1_r = e1_s[h, lg, pl.ds(r, 1), :]
                term = jnp.where(s_s[2 * h + 1, lg] >= thr_r, e2_s[h, lg], 0.0) * e1_r
                g = term if g is None else g + term
            blk = act[rr * N_KEYS:(rr + 1) * N_KEYS, lg * LANES:(lg + 1) * LANES] * g
            row_blocks.append(blk.astype(BF16))
        blocks.append(jnp.concatenate(row_blocks, axis=1) if lg_n > 1 else row_blocks[0])
    p_t = jnp.concatenate(blocks, axis=0) if len(blocks) > 1 else blocks[0]
    acc[...] += _dot(vt_ref[...], p_t)

    @pl.when(j == pl.num_programs(1) - 1)
    def _():
        x = x_ref[...] + g2_ref[0] * acc[...].T
        if final_norm:
            x = _rms(x, fg_ref[...])
        o_ref[...] = x


def _peer(h2, x, mod, tiles_per_group, wq, wq_lo, keys, keys_lo, u_tab, vt_tab, final_g, *, tt, et, final_norm):
    t = x.shape[0]
    rows = mod.shape[1]
    lg_n = tt // LANES
    n_j = N_EXPERTS // et
    kern = functools.partial(_peer_kernel, tt=tt, et=et, final_norm=final_norm)
    const = lambda shape: pl.BlockSpec(shape, lambda i, j: (0,) * len(shape), pipeline_mode=pl.Buffered(1))
    return pl.pallas_call(
        kern,
        grid=(t // tt, n_j),
        in_specs=[
            pl.BlockSpec((tt, D_MODEL), lambda i, j: (i, 0)),
            pl.BlockSpec((tt, D_MODEL), lambda i, j: (i, 0)),
            pl.BlockSpec((1, rows, D_MODEL), lambda i, j: (i // tiles_per_group, 0, 5)),
            const((D_MODEL, PK_HEADS * PK_QDIM)),
            const((D_MODEL, PK_HEADS * PK_QDIM)),
            const((2 * PK_HEADS, N_KEYS, PK_HALF)),
            const((2 * PK_HEADS, N_KEYS, PK_HALF)),
            pl.BlockSpec((et, D_MODEL), lambda i, j: (j, 0)),
            pl.BlockSpec((D_MODEL, et), lambda i, j: (0, j)),
            const((1, D_MODEL)),
        ],
        out_specs=pl.BlockSpec((tt, D_MODEL), lambda i, j: (i, 0)),
        out_shape=jax.ShapeDtypeStruct((t, D_MODEL), F32),
        scratch_shapes=[
            pltpu.VMEM((2 * PK_HEADS, lg_n, N_KEYS, LANES), F32),
            pltpu.VMEM((PK_HEADS, lg_n, N_KEYS, LANES), F32),
            pltpu.VMEM((PK_HEADS, lg_n, N_KEYS, LANES), F32),
            pltpu.VMEM((PK_HEADS, lg_n, N_KEYS, LANES), F32),
            pltpu.VMEM((3 * SUBLANES, LANES), F32),
            pltpu.VMEM((D_MODEL, tt), F32),
            pltpu.VMEM((tt, D_MODEL), BF16),
        ],
        compiler_params=_cparams(("arbitrary", "arbitrary")),
        name="peer",
    )(h2, x, mod, wq, wq_lo, keys, keys_lo, u_tab, vt_tab, final_g.reshape(1, D_MODEL))


def _pad_lanes(v):
    return jnp.pad(v, ((0, 0), (0, LANES - v.shape[-1])))


def _prep_w_in(w):
    return jnp.concatenate(
        [w[:, :OFF_B], _pad_lanes(w[:, OFF_B:OFF_A]), _pad_lanes(w[:, OFF_A:OFF_SQ]), w[:, OFF_SQ:]],
        axis=1).astype(BF16)


def _layer(l, x, mod, tiles_per_group, prior, p, *, seq, tt, final_norm, final_g):
    t = x.shape[0]
    bsz = t // seq
    qkv, z, bcol, acol, sq, skv = _inproj(x, mod, tiles_per_group, p["norm1_g"][l], p["w_in"][l], tt)
    r3 = lambda a: a.reshape(bsz, seq, a.shape[-1])
    qkv3 = r3(qkv)
    if prior is None:
        conv_buf = jnp.zeros((bsz, DN_CONV - 1, CONV_DIM), F32)
        s0 = jnp.zeros((bsz, DN_HEADS, DN_DK, DN_DV), F32)
        dn_cfg = dict(nb_blk=1, lb=256, chunk=DN_CHUNK, valid_len=256)
    else:
        s0, conv_buf, k_buf, v_buf, n_new = prior
        dn_cfg = dict(nb_blk=8, lb=DEC_PAD, chunk=DEC_PAD, valid_len=n_new)
    o_dn, s_new = _deltanet(qkv3, r3(z), r3(bcol), r3(acol), p["conv_w"][l], p["alog"][l], p["dtb"][l],
                            p["dn_norm_g"][l], conv_buf, s0, **dn_cfg)
    if prior is None:
        o_sw = _swa_prompt(r3(sq), r3(skv), p["sw_sinks"][l])
        new_conv = qkv3[:, seq - (DN_CONV - 1):, :]
        n_buf = WINDOW
        new_k = r3(skv)[:, seq - n_buf:, :SW_KV_W].reshape(bsz, n_buf, SW_KVHEADS, SW_HD)
        new_v = r3(skv)[:, seq - n_buf:, SW_KV_W:].reshape(bsz, n_buf, SW_KVHEADS, SW_HD)
    else:
        p_len = k_buf.shape[1]
        o_sw, new_k, new_v = _swa_decode(r3(sq), r3(skv), k_buf.reshape(bsz, p_len, SW_KV_W),
                                         v_buf.reshape(bsz, p_len, SW_KV_W), p["sw_sinks"][l],
                                         nb_blk=16, n_new=n_new)
        new_conv = jnp.concatenate([conv_buf, qkv3[:, :n_new]], axis=1)[:, n_new:]
        new_k = new_k.reshape(bsz, p_len, SW_KVHEADS, SW_HD)
        new_v = new_v.reshape(bsz, p_len, SW_KVHEADS, SW_HD)
    x, h2 = _outproj(x, o_dn.reshape(t, DN_V_W), o_sw.reshape(t, SW_Q_W), mod, tiles_per_group,
                     p["norm2_g"][l], p["w_out"][l], tt)
    x = _peer(h2, x, mod, tiles_per_group, p["wq"][l], p["wq_lo"][l], p["keys"][l], p["keys_lo"][l],
              p["u"][l], p["vt"][l], final_g, tt=tt, et=512, final_norm=final_norm)
    return x, (s_new, new_conv, new_k, new_v)


def _trunk(x, mod_all, tiles_per_group, prior_stack, p, final_g, *, seq, tt):
    new = []
    for l in range(DEPTH):
        prior = None if prior_stack is None else tuple(s[l] for s in prior_stack[:4]) + (prior_stack[4],)
        x, st = _layer(l, x, mod_all[l], tiles_per_group, prior, p, seq=seq, tt=tt,
                       final_norm=(l == DEPTH - 1), final_g=final_g)
        new.append(st)
    return x, [jnp.stack(zz) for zz in zip(*new)]


def kernel(x_prompt, x_sample, state_delta, state_conv, cache_swa_k, cache_swa_v, c_prompt, c_sample,
           norm1_g, norm2_g, final_norm_g, w_ada, b_ada, w_in, conv_w, dn_a_log, dn_dt_bias, dn_norm_g,
           sw_sinks, w_out, peer_wq, peer_keys, peer_u, peer_v):
    batch, seq, _ = x_prompt.shape
    dec_b, dec_l, _ = x_sample.shape
    tt = 512

    wq_hi, wq_lo = _split_bf16(peer_wq)
    keys_hi, keys_lo = _split_bf16(peer_keys.reshape(DEPTH, 2 * PK_HEADS, N_KEYS, PK_HALF))
    p = dict(
        norm1_g=norm1_g, norm2_g=norm2_g,
        w_in=jax.vmap(_prep_w_in)(w_in),
        conv_w=conv_w,
        alog=_pad_lanes(dn_a_log).reshape(DEPTH, 1, LANES),
        dtb=_pad_lanes(dn_dt_bias).reshape(DEPTH, 1, LANES),
        dn_norm_g=dn_norm_g.reshape(DEPTH, 1, DN_DV),
        sw_sinks=sw_sinks,
        w_out=w_out.astype(BF16),
        wq=wq_hi, wq_lo=wq_lo, keys=keys_hi, keys_lo=keys_lo,
        u=peer_u.astype(BF16),
        vt=jnp.swapaxes(peer_v, 1, 2).astype(BF16),
    )

    c_dec = jnp.repeat(c_sample, DEC_PAD, axis=0)
    n_rows = batch + dec_b * DEC_PAD
    pad_rows = (-n_rows) % SUBLANES
    c_all = jnp.concatenate([c_prompt, c_dec, jnp.zeros((pad_rows, D_MODEL), F32)], axis=0)
    mod = _ada_mod(c_all, w_ada, b_ada)
    mod_p = mod[:, :batch].reshape(DEPTH, batch, 1, 6 * D_MODEL)
    mod_s = mod[:, batch:n_rows].reshape(DEPTH, dec_b * DEC_PAD // tt, tt, 6 * D_MODEL)

    xp = x_prompt.reshape(batch * seq, D_MODEL)
    yp, (delta_p, conv_p, k_p, v_p) = _trunk(xp, mod_p, seq // tt, None, p, final_norm_g, seq=seq, tt=tt)

    xs = jnp.pad(x_sample, ((0, 0), (0, DEC_PAD - dec_l), (0, 0))).reshape(dec_b * DEC_PAD, D_MODEL)
    prior = (state_delta, state_conv, cache_swa_k, cache_swa_v, dec_l)
    ys, (delta_s, conv_s, k_s, v_s) = _trunk(xs, mod_s, 1, prior, p, final_norm_g, seq=DEC_PAD, tt=tt)

    y_prompt = yp.reshape(batch, seq, D_MODEL)
    y_sample = ys.reshape(dec_b, DEC_PAD, D_MODEL)[:, :dec_l]
    return (y_prompt, y_sample, delta_p, conv_p, k_p, v_p, delta_s, conv_s, k_s, v_s)
```

```python
import functools
from typing import NamedTuple

import jax
import jax.numpy as jnp
from jax import lax
from jax.experimental import pallas as pl
from jax.experimental.pallas import tpu as pltpu

F32 = jnp.float32
BF16 = jnp.bfloat16

D_MODEL = 1024
DEPTH = 4
DN_HEADS = 4
DN_DK = 128
DN_DV = 128
DN_CONV = 4
DN_CHUNK = 64
SW_QHEADS = 8
SW_KVHEADS = 2
SW_HD = 64
SW_GROUP = SW_QHEADS // SW_KVHEADS
WINDOW = 128
DN_QK_W = DN_HEADS * DN_DK
DN_V_W = DN_HEADS * DN_DV
CONV_DIM = 2 * DN_QK_W + DN_V_W
SW_Q_W = SW_QHEADS * SW_HD
SW_KV_W = SW_KVHEADS * SW_HD
OFF_Z = CONV_DIM
OFF_B = OFF_Z + DN_V_W
OFF_A = OFF_B + DN_HEADS
OFF_SQ = OFF_A + DN_HEADS
PK_HEADS = 8
N_KEYS = 128
N_EXPERTS = N_KEYS * N_KEYS
PK_QDIM = 256
PK_HALF = PK_QDIM // 2
PK_TOPK = 16
EPS = 1e-6

LANES = 128
SUBLANES = 8
VMEM_LIMIT = 56 * 1024 * 1024
PROJ_W = CONV_DIM + DN_V_W + 2 * LANES + SW_Q_W + 2 * SW_KV_W
DEC_PAD = SUBLANES
N_TOP = PK_TOPK + 1
DN_INV_BLOCK = SUBLANES
NEG_INF = float("-inf")


def _cparams(sem):
    return pltpu.CompilerParams(dimension_semantics=sem, vmem_limit_bytes=VMEM_LIMIT)


def _silu(x):
    return x * jax.nn.sigmoid(x)


def _softplus(x):
    return jnp.maximum(x, 0.0) + jnp.log1p(jnp.exp(-jnp.abs(x)))


def _rms(x, g):
    return x * lax.rsqrt(jnp.mean(x * x, axis=-1, keepdims=True) + EPS) * g


def _dot(a, b):
    return jnp.dot(a, b, preferred_element_type=F32)


def _dot_nt(a, b):
    return lax.dot_general(a, b, (((1,), (1,)), ((), ())), preferred_element_type=F32)


def _dot_tn(a, b):
    return lax.dot_general(a, b, (((0,), (0,)), ((), ())), preferred_element_type=F32)


def _split_bf16(v):
    hi = v.astype(BF16)
    lo = (v - hi.astype(F32)).astype(BF16)
    return hi, lo


def _dot3(a, b, dot=_dot):
    a_hi, a_lo = _split_bf16(a)
    b_hi, b_lo = _split_bf16(b)
    return dot(a_hi, b_hi) + (dot(a_hi, b_lo) + dot(a_lo, b_hi))


def _ada_kernel(c_ref, w_ref, b_ref, o_ref):
    a = _silu(c_ref[...]).astype(BF16)
    o_ref[0] = _dot(a, w_ref[0].astype(BF16)) + b_ref[0]


def _ada_mod(c_all, w_ada, b_ada):
    rows = c_all.shape[0]
    tn = 1536
    return pl.pallas_call(
        _ada_kernel,
        grid=(DEPTH, 6 * D_MODEL // tn),
        in_specs=[
            pl.BlockSpec((rows, D_MODEL), lambda l, j: (0, 0)),
            pl.BlockSpec((1, D_MODEL, tn), lambda l, j: (l, 0, j)),
            pl.BlockSpec((1, 1, tn), lambda l, j: (l, 0, j)),
        ],
        out_specs=pl.BlockSpec((1, rows, tn), lambda l, j: (l, 0, j)),
        out_shape=jax.ShapeDtypeStruct((DEPTH, rows, 6 * D_MODEL), F32),
        compiler_params=_cparams(("arbitrary", "arbitrary")),
        name="ada_mod",
    )(c_all, w_ada, b_ada.reshape(DEPTH, 1, 6 * D_MODEL))


class ModSrc(NamedTuple):
    arr: jax.Array
    layer: int
    row0: int
    tiles_per_group: int | None


def _mod_spec(k, ms, tt):
    if ms.tiles_per_group is None:
        return pl.BlockSpec((1, tt, D_MODEL), lambda i, *_: (ms.layer, ms.row0 // tt + i, k))
    return pl.BlockSpec((1, SUBLANES, D_MODEL), lambda i, *_: (ms.layer, ms.row0 // SUBLANES, k))


def _mod_rows(ref, tiles_per_group):
    if tiles_per_group is None:
        return ref[0]
    return ref[0, pl.ds(pl.program_id(0) // tiles_per_group, 1), :]


def _inproj_kernel(x_ref, sh_ref, sc_ref, g_ref, w_ref,
                   qkv_ref, z_ref, b_ref, a_ref, sq_ref, skv_ref, *, tiles_per_group):
    sc = _mod_rows(sc_ref, tiles_per_group)
    sh = _mod_rows(sh_ref, tiles_per_group)
    h = _rms(x_ref[...], g_ref[...]) * (1.0 + sc) + sh
    p = _dot(h.astype(BF16), w_ref[...])
    o = 0
    for ref in (qkv_ref, z_ref, b_ref, a_ref, sq_ref, skv_ref):
        w = ref.shape[-1]
        ref[...] = p[:, o:o + w]
        o += w


def _inproj(x, ms, norm_g, w_r, tt):
    t = x.shape[0]
    widths = (CONV_DIM, DN_V_W, LANES, LANES, SW_Q_W, 2 * SW_KV_W)
    return pl.pallas_call(
        functools.partial(_inproj_kernel, tiles_per_group=ms.tiles_per_group),
        grid=(t // tt,),
        in_specs=[
            pl.BlockSpec((tt, D_MODEL), lambda i: (i, 0)),
            _mod_spec(0, ms, tt),
            _mod_spec(1, ms, tt),
            pl.BlockSpec((1, D_MODEL), lambda i: (0, 0)),
            pl.BlockSpec((D_MODEL, PROJ_W), lambda i: (0, 0)),
        ],
        out_specs=[pl.BlockSpec((tt, w), lambda i: (i, 0)) for w in widths],
        out_shape=[jax.ShapeDtypeStruct((t, w), F32) for w in widths],
        compiler_params=_cparams(("arbitrary",)),
        name="inproj",
    )(x, ms.arr, ms.arr, norm_g.reshape(1, D_MODEL), w_r)


def _dn_kernel(u_ref, z_ref, b_ref, a_ref, cw_ref, alog_ref, dtb_ref, ng_ref, cbuf_ref, s0_ref,
               o_ref, sfin_ref, ubuf, qkv, beta_s, gc_s, gct_s, state,
               *, nb_blk, lb, chunk, valid_len):
    l = pl.program_id(1)
    nc = lb // chunk
    n_scan = chunk.bit_length() - 1

    @pl.when(l == 0)
    def _():
        state[...] = s0_ref[...]
        ubuf[:, SUBLANES - (DN_CONV - 1):SUBLANES, :] = cbuf_ref[...]

    ubuf[:, SUBLANES:SUBLANES + lb, :] = u_ref[...]
    y = None
    for i in range(DN_CONV):
        start = SUBLANES - (DN_CONV - 1) + i
        term = ubuf[:, start:start + lb, :] * cw_ref[i:i + 1, :]
        y = term if y is None else y + term
    qkv[...] = _silu(y)
    ubuf[:, SUBLANES - (DN_CONV - 1):SUBLANES, :] = ubuf[:, lb + SUBLANES - (DN_CONV - 1):lb + SUBLANES, :]

    row = lax.broadcasted_iota(jnp.int32, (lb, LANES), 0)
    for nb in range(nb_blk):
        beta = jax.nn.sigmoid(b_ref[nb])
        g = -jnp.exp(alog_ref[...]) * _softplus(a_ref[nb] + dtb_ref[...])
        if valid_len < lb:
            beta = jnp.where(row < valid_len, beta, 0.0)
            g = jnp.where(row < valid_len, g, 0.0)
        gc = g
        for s in range(n_scan):
            sh = 1 << s
            gc = gc + jnp.where((row % chunk) >= sh, pltpu.roll(gc, sh, 0), 0.0)
        beta_s[nb] = beta
        gc_s[nb] = gc
        for r0 in range(0, lb, LANES):
            nrow = min(LANES, lb - r0)
            blk = gc[r0:r0 + nrow]
            if nrow < LANES:
                blk = jnp.concatenate([blk, jnp.zeros((LANES - nrow, LANES), F32)], axis=0)
            blk_t = blk.T
            for c0 in range(0, nrow, chunk):
                gct_s[nb, (r0 + c0) // chunk] = blk_t[:, c0:c0 + chunk]

    ii = lax.broadcasted_iota(jnp.int32, (chunk, chunk), 0)
    jj = lax.broadcasted_iota(jnp.int32, (chunk, chunk), 1)
    causal = ii >= jj
    strict = ii > jj
    eye = jnp.where(ii == jj, 1.0, 0.0)

    items = [(nb, c, h) for nb in range(nb_blk) for c in range(nc) for h in range(DN_HEADS)]

    def rows_of(c):
        return slice(c * chunk, (c + 1) * chunk)

    def l2n(x):
        return x * lax.rsqrt(jnp.sum(x * x, axis=-1, keepdims=True) + EPS)

    qs = [l2n(qkv[nb, rows_of(c), h * DN_DK:(h + 1) * DN_DK]) * (DN_DK ** -0.5) for nb, c, h in items]
    ks = [l2n(qkv[nb, rows_of(c), DN_QK_W + h * DN_DK:DN_QK_W + (h + 1) * DN_DK]) for nb, c, h in items]
    vs = [qkv[nb, rows_of(c), 2 * DN_QK_W + h * DN_DV:2 * DN_QK_W + (h + 1) * DN_DV] for nb, c, h in items]
    cols = [gc_s[nb, rows_of(c), h:h + 1] for nb, c, h in items]
    rws = [gct_s[nb, c, h:h + 1, :] for nb, c, h in items]
    bcols = [beta_s[nb, rows_of(c), h:h + 1] for nb, c, h in items]
    lasts = [gc_s[nb, (c + 1) * chunk - 1:(c + 1) * chunk, h:h + 1] for nb, c, h in items]
    decays = [jnp.where(causal, jnp.exp(jnp.where(causal, col - rw, 0.0)), 0.0) for col, rw in zip(cols, rws)]
    kks = [_dot3(k, k, _dot_nt) for k in ks]
    a_mats = [jnp.where(strict, bcol * kk * decay, 0.0) for bcol, kk, decay in zip(bcols, kks, decays)]
    base = min(DN_INV_BLOCK, chunk)
    same = lambda b: (ii // b) == (jj // b)
    p_mats = [jnp.where(same(base), -a, 0.0) for a in a_mats]
    t_mats = [eye + n_mat for n_mat in p_mats]
    for _ in range(base.bit_length() - 2):
        p_mats = [_dot3(p_mat, p_mat) for p_mat in p_mats]
        t_mats = [t_mat + _dot3(t_mat, p_mat) for t_mat, p_mat in zip(t_mats, p_mats)]
    b = base
    while b < chunk:
        offs = [jnp.where(same(2 * b) & jnp.logical_not(same(b)), a, 0.0) for a in a_mats]
        t_mats = [t_mat - _dot3(_dot3(t_mat, off), t_mat) for t_mat, off in zip(t_mats, offs)]
        b *= 2
    e_cols = [jnp.exp(col) for col in cols]
    sols = [_dot3(t_mat, jnp.concatenate([bcol * v, (bcol * e_col) * k], axis=-1))
            for t_mat, bcol, v, e_col, k in zip(t_mats, bcols, vs, e_cols, ks)]
    qks = [_dot_nt(q, k) * decay for q, k, decay in zip(qs, ks, decays)]
    wqs = [jnp.concatenate([sol[:, DN_DV:], q * e_col], axis=0) for sol, q, e_col in zip(sols, qs, e_cols)]
    k_decs = [k * jnp.exp(last - col) for k, last, col in zip(ks, lasts, cols)]

    for idx, (nb, c, h) in enumerate(items):
        s_old = state[nb, h]
        ws = _dot(wqs[idx], s_old)
        uu = sols[idx][:, :DN_DV] - ws[:chunk]
        o = ws[chunk:] + _dot(qks[idx], uu)
        state[nb, h] = jnp.exp(lasts[idx]) * s_old + _dot_tn(k_decs[idx], uu)
        o = _rms(o, ng_ref[...])
        zz = z_ref[nb, rows_of(c), h * DN_DV:(h + 1) * DN_DV]
        o_ref[nb, rows_of(c), h * DN_DV:(h + 1) * DN_DV] = (o * _silu(zz)).astype(o_ref.dtype)

    @pl.when(l == pl.num_programs(1) - 1)
    def _():
        sfin_ref[...] = state[...]


def _deltanet(u, z, b, a, conv_w, alog, dtb, norm_g, conv_buf, s0, *, nb_blk, lb, chunk, valid_len):
    bsz, seq, _ = u.shape
    nc = lb // chunk
    kern = functools.partial(_dn_kernel, nb_blk=nb_blk, lb=lb, chunk=chunk, valid_len=valid_len)
    tok = lambda w: pl.BlockSpec((nb_blk, lb, w), lambda i, l: (i, l, 0))
    full2 = lambda r, w: pl.BlockSpec((r, w), lambda i, l: (0, 0))
    return pl.pallas_call(
        kern,
        grid=(bsz // nb_blk, seq // lb),
        in_specs=[
            tok(CONV_DIM), tok(DN_V_W), tok(LANES), tok(LANES),
            full2(DN_CONV, CONV_DIM), full2(1, LANES), full2(1, LANES), full2(1, DN_DV),
            pl.BlockSpec((nb_blk, DN_CONV - 1, CONV_DIM), lambda i, l: (i, 0, 0)),
            pl.BlockSpec((nb_blk, DN_HEADS, DN_DK, DN_DV), lambda i, l: (i, 0, 0, 0)),
        ],
        out_specs=[
            tok(DN_V_W),
            pl.BlockSpec((nb_blk, DN_HEADS, DN_DK, DN_DV), lambda i, l: (i, 0, 0, 0)),
        ],
        out_shape=[
            jax.ShapeDtypeStruct((bsz, seq, DN_V_W), F32),
            jax.ShapeDtypeStruct((bsz, DN_HEADS, DN_DK, DN_DV), F32),
        ],
        scratch_shapes=[
            pltpu.VMEM((nb_blk, lb + SUBLANES, CONV_DIM), F32),
            pltpu.VMEM((nb_blk, lb, CONV_DIM), F32),
            pltpu.VMEM((nb_blk, lb, LANES), F32),
            pltpu.VMEM((nb_blk, lb, LANES), F32),
            pltpu.VMEM((nb_blk, nc, LANES, chunk), F32),
            pltpu.VMEM((nb_blk, DN_HEADS, DN_DK, DN_DV), F32),
        ],
        compiler_params=_cparams(("arbitrary", "arbitrary")),
        name="deltanet",
    )(u, z, b, a, conv_w, alog, dtb, norm_g, conv_buf, s0)


def _swa_prompt_kernel(sink_ref, q_ref, kvp_ref, kvc_ref, o_ref):
    i = pl.program_id(1)
    q = q_ref[0]
    kvp = kvp_ref[0]
    kvc = kvc_ref[0]
    kcat = jnp.concatenate([kvp[:, :SW_KV_W], kvc[:, :SW_KV_W]], axis=0).astype(BF16)
    vcat = jnp.concatenate([kvp[:, SW_KV_W:], kvc[:, SW_KV_W:]], axis=0).astype(BF16)
    row = lax.broadcasted_iota(jnp.int32, (WINDOW, 2 * WINDOW), 0)
    col = lax.broadcasted_iota(jnp.int32, (WINDOW, 2 * WINDOW), 1)
    valid = (col > row) & (col <= row + WINDOW) & ((col >= WINDOW) | (i > 0))
    for qh in range(SW_QHEADS):
        kv = qh // SW_GROUP
        qs = q[:, qh * SW_HD:(qh + 1) * SW_HD].astype(BF16)
        ks = kcat[:, kv * SW_HD:(kv + 1) * SW_HD]
        vs = vcat[:, kv * SW_HD:(kv + 1) * SW_HD]
        s = _dot_nt(qs, ks) * (SW_HD ** -0.5)
        s = jnp.where(valid, s, NEG_INF)
        sink = sink_ref[qh]
        m = jnp.maximum(jnp.max(s, axis=-1, keepdims=True), sink)
        p = jnp.exp(s - m)
        den = jnp.sum(p, axis=-1, keepdims=True) + jnp.exp(sink - m)
        o = _dot(p.astype(BF16), vs) / den
        o_ref[0, :, qh * SW_HD:(qh + 1) * SW_HD] = o.astype(o_ref.dtype)


def _swa_prompt(sq, skv, sinks):
    bsz, seq, _ = sq.shape
    nblk = seq // WINDOW
    return pl.pallas_call(
        _swa_prompt_kernel,
        grid=(bsz, nblk),
        in_specs=[
            pl.BlockSpec(memory_space=pltpu.SMEM),
            pl.BlockSpec((1, WINDOW, SW_Q_W), lambda b, i: (b, i, 0)),
            pl.BlockSpec((1, WINDOW, 2 * SW_KV_W), lambda b, i: (b, jnp.maximum(i - 1, 0), 0)),
            pl.BlockSpec((1, WINDOW, 2 * SW_KV_W), lambda b, i: (b, i, 0)),
        ],
        out_specs=pl.BlockSpec((1, WINDOW, SW_Q_W), lambda b, i: (b, i, 0)),
        out_shape=jax.ShapeDtypeStruct((bsz, seq, SW_Q_W), F32),
        compiler_params=_cparams(("arbitrary", "arbitrary")),
        name="swa_prompt",
    )(sinks, sq, skv, skv)


def _swa_decode_kernel(sink_ref, q_ref, kvn_ref, kb_ref, vb_ref, o_ref, ko_ref, vo_ref, *, n_new):
    q = q_ref[...]
    kvn = kvn_ref[...]
    kb = kb_ref[...]
    vb = vb_ref[...]
    nb = q.shape[0]
    p_len = kb.shape[1]
    kn = kvn[:, :, :SW_KV_W]
    vn = kvn[:, :, SW_KV_W:]
    ko_ref[:, :p_len - n_new, :] = kb[:, n_new:, :]
    ko_ref[:, p_len - n_new:, :] = kn[:, :n_new, :]
    vo_ref[:, :p_len - n_new, :] = vb[:, n_new:, :]
    vo_ref[:, p_len - n_new:, :] = vn[:, :n_new, :]
    qi = lax.broadcasted_iota(jnp.int32, (nb, DEC_PAD, p_len), 1)
    kj = lax.broadcasted_iota(jnp.int32, (nb, DEC_PAD, p_len), 2)
    valid_buf = (p_len + qi - kj) < WINDOW
    qi2 = lax.broadcasted_iota(jnp.int32, (nb, DEC_PAD, DEC_PAD), 1)
    kj2 = lax.broadcasted_iota(jnp.int32, (nb, DEC_PAD, DEC_PAD), 2)
    valid_new = (kj2 <= qi2) & (kj2 < n_new)
    for qh in range(SW_QHEADS):
        kv = qh // SW_GROUP
        hs = slice(kv * SW_HD, (kv + 1) * SW_HD)
        qs = q[:, :, qh * SW_HD:(qh + 1) * SW_HD]
        sb = jnp.einsum("bqd,bkd->bqk", qs, kb[:, :, hs], preferred_element_type=F32) * (SW_HD ** -0.5)
        sn = jnp.einsum("bqd,bkd->bqk", qs, kn[:, :, hs], preferred_element_type=F32) * (SW_HD ** -0.5)
        sb = jnp.where(valid_buf, sb, NEG_INF)
        sn = jnp.where(valid_new, sn, NEG_INF)
        sink = sink_ref[qh]
        m = jnp.maximum(jnp.maximum(jnp.max(sb, axis=-1, keepdims=True),
                                    jnp.max(sn, axis=-1, keepdims=True)), sink)
        pb = jnp.exp(sb - m)
        pn = jnp.exp(sn - m)
        den = jnp.sum(pb, axis=-1, keepdims=True) + jnp.sum(pn, axis=-1, keepdims=True) + jnp.exp(sink - m)
        o = (jnp.einsum("bqk,bkd->bqd", pb, vb[:, :, hs], preferred_element_type=F32)
             + jnp.einsum("bqk,bkd->bqd", pn, vn[:, :, hs], preferred_element_type=F32))
        o_ref[:, :, qh * SW_HD:(qh + 1) * SW_HD] = (o / den).astype(o_ref.dtype)


def _swa_decode(sq, skv, k_buf, v_buf, sinks, *, nb_blk, n_new):
    bsz = sq.shape[0]
    p_len = k_buf.shape[1]
    tok = lambda w: pl.BlockSpec((nb_blk, DEC_PAD, w), lambda i: (i, 0, 0))
    cache = pl.BlockSpec((nb_blk, p_len, SW_KV_W), lambda i: (i, 0, 0))
    return pl.pallas_call(
        functools.partial(_swa_decode_kernel, n_new=n_new),
        grid=(bsz // nb_blk,),
        in_specs=[pl.BlockSpec(memory_space=pltpu.SMEM), tok(SW_Q_W), tok(2 * SW_KV_W), cache, cache],
        out_specs=[tok(SW_Q_W), cache, cache],
        out_shape=[
            jax.ShapeDtypeStruct((bsz, DEC_PAD, SW_Q_W), F32),
            jax.ShapeDtypeStruct((bsz, p_len, SW_KV_W), F32),
            jax.ShapeDtypeStruct((bsz, p_len, SW_KV_W), F32),
        ],
        compiler_params=_cparams(("arbitrary",)),
        name="swa_decode",
    )(sinks, sq, skv, k_buf, v_buf)


def _outproj_kernel(x_ref, odn_ref, osw_ref, g1_ref, sh2_ref, sc2_ref, n2_ref, w_ref, xo_ref, h2_ref,
                    *, tiles_per_group):
    mix = (_dot(odn_ref[...].astype(BF16), w_ref[:DN_V_W, :])
           + _dot(osw_ref[...].astype(BF16), w_ref[DN_V_W:, :]))
    x = x_ref[...] + _mod_rows(g1_ref, tiles_per_group) * mix
    xo_ref[...] = x
    h2 = (_rms(x, n2_ref[...]) * (1.0 + _mod_rows(sc2_ref, tiles_per_group))
          + _mod_rows(sh2_ref, tiles_per_group))
    h2_ref[...] = h2.astype(h2_ref.dtype)


def _outproj(x, o_dn, o_sw, ms, norm_g, w_out, tt):
    t = x.shape[0]
    return pl.pallas_call(
        functools.partial(_outproj_kernel, tiles_per_group=ms.tiles_per_group),
        grid=(t // tt,),
        in_specs=[
            pl.BlockSpec((tt, D_MODEL), lambda i: (i, 0)),
            pl.BlockSpec((tt, DN_V_W), lambda i: (i, 0)),
            pl.BlockSpec((tt, SW_Q_W), lambda i: (i, 0)),
            _mod_spec(2, ms, tt),
            _mod_spec(3, ms, tt),
            _mod_spec(4, ms, tt),
            pl.BlockSpec((1, D_MODEL), lambda i: (0, 0)),
            pl.BlockSpec((D_MODEL, D_MODEL), lambda i: (0, 0)),
        ],
        out_specs=[pl.BlockSpec((tt, D_MODEL), lambda i: (i, 0))] * 2,
        out_shape=[jax.ShapeDtypeStruct((t, D_MODEL), F32), jax.ShapeDtypeStruct((t, D_MODEL), F32)],
        compiler_params=_cparams(("arbitrary",)),
        name="outproj",
    )(x, o_dn, o_sw, ms.arr, ms.arr, ms.arr, norm_g.reshape(1, D_MODEL), w_out)


def _top_values(work, n, with_rank=False):
    out = []
    rank = jnp.full(work.shape, float(n), F32) if with_rank else None
    for it in range(n):
        m = jnp.max(work, axis=0, keepdims=True)
        out.append(m)
        if it + 1 < n or with_rank:
            hit = work == m
            if with_rank:
                rank = jnp.where(hit, float(it), rank)
            work = jnp.where(hit, NEG_INF, work)
    return (out, rank) if with_rank else out


def _peer_kernel(h2_ref, x_ref, g2_ref, wq_ref, wql_ref, keys_ref, keysl_ref, u_ref, vt_ref, fg_ref, o_ref,
                 s_s, cnt_s, e1_s, rk_s, e2_s, tv_s, acc, hb_s, *, tt, et, final_norm, tiles_per_group):
    j = pl.program_id(1)
    lg_n = tt // LANES
    n_rr = et // N_KEYS
    bf16_rows = 2 * SUBLANES

    def row_bf16(row):
        one = jnp.broadcast_to(row, (bf16_rows, LANES)).astype(BF16)
        return pltpu.repeat(one, N_KEYS // bf16_rows, axis=0)

    @pl.when(j == 0)
    def _():
        acc[...] = jnp.zeros_like(acc)
        tv_s[...] = jnp.full(tv_s.shape, NEG_INF, F32)
        h2 = h2_ref[...]
        h_hi, h_lo = _split_bf16(h2)
        hb_s[...] = h_hi
        q = _dot(h_hi, wq_ref[...]) + (_dot(h_hi, wql_ref[...]) + _dot(h_lo, wq_ref[...]))
        q_hi, q_lo = _split_bf16(q)
        for hp in range(2 * PK_HEADS):
            cs = slice(hp * PK_HALF, (hp + 1) * PK_HALF)
            s_t = _dot_nt(keys_ref[hp], q_hi[:, cs]) + (
                _dot_nt(keys_ref[hp], q_lo[:, cs]) + _dot_nt(keysl_ref[hp], q_hi[:, cs]))
            for lg in range(lg_n):
                s_s[hp, lg] = s_t[:, lg * LANES:(lg + 1) * LANES]

        def head_body(idx, carry):
            h = idx // lg_n
            lg = idx % lg_n
            s1 = s_s[2 * h, lg]
            s2 = s_s[2 * h + 1, lg]
            t1 = _top_values(s1, N_TOP)
            t2, rank2 = _top_values(s2, N_TOP, with_rank=True)
            for it in range(N_TOP):
                tv_s[it:it + 1, :] = t2[it]
            rid = lax.broadcasted_iota(jnp.int32, (SUBLANES, LANES), 0)
            cands = []
            for i in range(N_TOP):
                n_i = N_TOP // (i + 1)
                for r0 in range(0, n_i, SUBLANES):
                    blk = t1[i] + tv_s[r0:r0 + SUBLANES, :]
                    if n_i - r0 < SUBLANES:
                        blk = jnp.where(rid < n_i - r0, blk, NEG_INF)
                    cands.append(blk)
            cand = jnp.concatenate(cands, axis=0)
            tops = _top_values(cand, N_TOP)
            thr = 0.5 * (tops[PK_TOPK - 1] + tops[PK_TOPK])
            m1 = t1[0]
            m2 = t2[0]
            zsum = jnp.sum(jnp.where(cand >= thr, jnp.exp(cand - (m1 + m2)), 0.0), axis=0, keepdims=True)
            need = thr - s1
            cnt = jnp.zeros_like(s1)
            for it in range(N_TOP):
                cnt = cnt + jnp.where(t2[it] >= need, 1.0, 0.0)
            cnt_s[h, lg] = cnt
            rk_s[h, lg] = rank2.astype(BF16)
            e1_s[h, lg] = jnp.exp(s1 - m1) * (0.5 / zsum)
            e2_s[h, lg] = jnp.exp(s2 - m2).astype(BF16)
            return carry

        lax.fori_loop(0, PK_HEADS * lg_n, head_body, 0)

    a_t = _dot_nt(u_ref[...], hb_s[...])
    act = a_t * (1.0 + lax.erf(a_t * (2.0 ** -0.5)))
    blocks = []
    for rr in range(n_rr):
        r = j * n_rr + rr
        row_blocks = []
        for lg in range(lg_n):
            g = None
            for h in range(PK_HEADS):
                cnt_b = row_bf16(cnt_s[h, lg, pl.ds(r, 1), :])
                e1_b = row_bf16(e1_s[h, lg, pl.ds(r, 1), :])
                term = jnp.where(rk_s[h, lg] < cnt_b, e2_s[h, lg], jnp.zeros((), BF16)) * e1_b
                g = term if g is None else g + term
            blk = act[rr * N_KEYS:(rr + 1) * N_KEYS, lg * LANES:(lg + 1) * LANES] * g.astype(F32)
            row_blocks.append(blk.astype(BF16))
        blocks.append(jnp.concatenate(row_blocks, axis=1) if lg_n > 1 else row_blocks[0])
    p_t = jnp.concatenate(blocks, axis=0) if len(blocks) > 1 else blocks[0]
    acc[...] += _dot(vt_ref[...], p_t)

    @pl.when(j == pl.num_programs(1) - 1)
    def _():
        x = x_ref[...] + _mod_rows(g2_ref, tiles_per_group) * acc[...].T
        if final_norm:
            x = _rms(x, fg_ref[...])
        o_ref[...] = x


def _peer(h2, x, ms, wq, wq_lo, keys, keys_lo, u_tab, vt_tab, final_g, *, tt, et, final_norm):
    t = x.shape[0]
    lg_n = tt // LANES
    n_j = N_EXPERTS // et
    kern = functools.partial(_peer_kernel, tt=tt, et=et, final_norm=final_norm,
                             tiles_per_group=ms.tiles_per_group)
    const = lambda shape: pl.BlockSpec(shape, lambda i, j: (0,) * len(shape), pipeline_mode=pl.Buffered(1))
    return pl.pallas_call(
        kern,
        grid=(t // tt, n_j),
        in_specs=[
            pl.BlockSpec((tt, D_MODEL), lambda i, j: (i, 0)),
            pl.BlockSpec((tt, D_MODEL), lambda i, j: (i, 0)),
            _mod_spec(5, ms, tt),
            const((D_MODEL, PK_HEADS * PK_QDIM)),
            const((D_MODEL, PK_HEADS * PK_QDIM)),
            const((2 * PK_HEADS, N_KEYS, PK_HALF)),
            const((2 * PK_HEADS, N_KEYS, PK_HALF)),
            pl.BlockSpec((et, D_MODEL), lambda i, j: (j, 0)),
            pl.BlockSpec((D_MODEL, et), lambda i, j: (0, j)),
            const((1, D_MODEL)),
        ],
        out_specs=pl.BlockSpec((tt, D_MODEL), lambda i, j: (i, 0)),
        out_shape=jax.ShapeDtypeStruct((t, D_MODEL), F32),
        scratch_shapes=[
            pltpu.VMEM((2 * PK_HEADS, lg_n, N_KEYS, LANES), F32),
            pltpu.VMEM((PK_HEADS, lg_n, N_KEYS, LANES), F32),
            pltpu.VMEM((PK_HEADS, lg_n, N_KEYS, LANES), F32),
            pltpu.VMEM((PK_HEADS, lg_n, N_KEYS, LANES), BF16),
            pltpu.VMEM((PK_HEADS, lg_n, N_KEYS, LANES), BF16),
            pltpu.VMEM((3 * SUBLANES, LANES), F32),
            pltpu.VMEM((D_MODEL, tt), F32),
            pltpu.VMEM((tt, D_MODEL), BF16),
        ],
        compiler_params=_cparams(("arbitrary", "arbitrary")),
        name="peer",
    )(h2, x, ms.arr, wq, wq_lo, keys, keys_lo, u_tab, vt_tab, final_g.reshape(1, D_MODEL))


def _pad_lanes(v):
    return jnp.pad(v, ((0, 0), (0, LANES - v.shape[-1])))


def _prep_w_in(w):
    return jnp.concatenate(
        [w[:, :OFF_B], _pad_lanes(w[:, OFF_B:OFF_A]), _pad_lanes(w[:, OFF_A:OFF_SQ]), w[:, OFF_SQ:]],
        axis=1).astype(BF16)


def _layer(l, x, ms, prior, p, *, seq, tt, final_norm, final_g):
    t = x.shape[0]
    bsz = t // seq
    qkv, z, bcol, acol, sq, skv = _inproj(x, ms, p["norm1_g"][l], p["w_in"][l], tt)
    r3 = lambda a: a.reshape(bsz, seq, a.shape[-1])
    qkv3 = r3(qkv)
    if prior is None:
        conv_buf = jnp.zeros((bsz, DN_CONV - 1, CONV_DIM), F32)
        s0 = jnp.zeros((bsz, DN_HEADS, DN_DK, DN_DV), F32)
        dn_cfg = dict(nb_blk=1, lb=256, chunk=DN_CHUNK, valid_len=256)
    else:
        s0, conv_buf, k_buf, v_buf, n_new = prior
        dn_cfg = dict(nb_blk=8, lb=DEC_PAD, chunk=DEC_PAD, valid_len=n_new)
    o_dn, s_new = _deltanet(qkv3, r3(z), r3(bcol), r3(acol), p["conv_w"][l], p["alog"][l], p["dtb"][l],
                            p["dn_norm_g"][l], conv_buf, s0, **dn_cfg)
    if prior is None:
        o_sw = _swa_prompt(r3(sq), r3(skv), p["sw_sinks"][l])
        new_conv = qkv3[:, seq - (DN_CONV - 1):, :]
        n_buf = WINDOW
        new_k = r3(skv)[:, seq - n_buf:, :SW_KV_W].reshape(bsz, n_buf, SW_KVHEADS, SW_HD)
        new_v = r3(skv)[:, seq - n_buf:, SW_KV_W:].reshape(bsz, n_buf, SW_KVHEADS, SW_HD)
    else:
        p_len = k_buf.shape[1]
        o_sw, new_k, new_v = _swa_decode(r3(sq), r3(skv), k_buf.reshape(bsz, p_len, SW_KV_W),
                                         v_buf.reshape(bsz, p_len, SW_KV_W), p["sw_sinks"][l],
                                         nb_blk=16, n_new=n_new)
        new_conv = jnp.concatenate([conv_buf, qkv3[:, :n_new]], axis=1)[:, n_new:]
        new_k = new_k.reshape(bsz, p_len, SW_KVHEADS, SW_HD)
        new_v = new_v.reshape(bsz, p_len, SW_KVHEADS, SW_HD)
    x, h2 = _outproj(x, o_dn.reshape(t, DN_V_W), o_sw.reshape(t, SW_Q_W), ms,
                     p["norm2_g"][l], p["w_out"][l], tt)
    x = _peer(h2, x, ms, p["wq"][l], p["wq_lo"][l], p["keys"][l], p["keys_lo"][l],
              p["u"][l], p["vt"][l], final_g, tt=tt, et=512, final_norm=final_norm)
    return x, (s_new, new_conv, new_k, new_v)


def _trunk(x, mod, row0, tiles_per_group, prior_stack, p, final_g, *, seq, tt):
    new = []
    for l in range(DEPTH):
        prior = None if prior_stack is None else tuple(s[l] for s in prior_stack[:4]) + (prior_stack[4],)
        x, st = _layer(l, x, ModSrc(mod, l, row0, tiles_per_group), prior, p, seq=seq, tt=tt,
                       final_norm=(l == DEPTH - 1), final_g=final_g)
        new.append(st)
    return x, [jnp.stack(zz) for zz in zip(*new)]


def kernel(x_prompt, x_sample, state_delta, state_conv, cache_swa_k, cache_swa_v, c_prompt, c_sample,
           norm1_g, norm2_g, final_norm_g, w_ada, b_ada, w_in, conv_w, dn_a_log, dn_dt_bias, dn_norm_g,
           sw_sinks, w_out, peer_wq, peer_keys, peer_u, peer_v):
    batch, seq, _ = x_prompt.shape
    dec_b, dec_l, _ = x_sample.shape
    tt = 512

    wq_hi, wq_lo = _split_bf16(peer_wq)
    keys_hi, keys_lo = _split_bf16(peer_keys.reshape(DEPTH, 2 * PK_HEADS, N_KEYS, PK_HALF))
    p = dict(
        norm1_g=norm1_g, norm2_g=norm2_g,
        w_in=jax.vmap(_prep_w_in)(w_in),
        conv_w=conv_w,
        alog=_pad_lanes(dn_a_log).reshape(DEPTH, 1, LANES),
        dtb=_pad_lanes(dn_dt_bias).reshape(DEPTH, 1, LANES),
        dn_norm_g=dn_norm_g.reshape(DEPTH, 1, DN_DV),
        sw_sinks=sw_sinks,
        w_out=w_out.astype(BF16),
        wq=wq_hi, wq_lo=wq_lo, keys=keys_hi, keys_lo=keys_lo,
        u=peer_u.astype(BF16),
        vt=jnp.swapaxes(peer_v, 1, 2).astype(BF16),
    )

    assert batch <= SUBLANES and (dec_b * DEC_PAD) % tt == 0
    c_dec = jnp.repeat(c_sample, DEC_PAD, axis=0)
    n_dec = dec_b * DEC_PAD
    pad_rows = (-(n_dec + batch)) % SUBLANES
    c_all = jnp.concatenate([c_dec, c_prompt, jnp.zeros((pad_rows, D_MODEL), F32)], axis=0)
    mod = _ada_mod(c_all, w_ada, b_ada)

    xp = x_prompt.reshape(batch * seq, D_MODEL)
    yp, (delta_p, conv_p, k_p, v_p) = _trunk(xp, mod, n_dec, seq // tt, None, p, final_norm_g, seq=seq, tt=tt)

    xs = jnp.pad(x_sample, ((0, 0), (0, DEC_PAD - dec_l), (0, 0))).reshape(dec_b * DEC_PAD, D_MODEL)
    prior = (state_delta, state_conv, cache_swa_k, cache_swa_v, dec_l)
    ys, (delta_s, conv_s, k_s, v_s) = _trunk(xs, mod, 0, None, prior, p, final_norm_g, seq=DEC_PAD, tt=tt)

    y_prompt = yp.reshape(batch, seq, D_MODEL)
    y_sample = ys.reshape(dec_b, DEC_PAD, D_MODEL)[:, :dec_l]
    return (y_prompt, y_sample, delta_p, conv_p, k_p, v_p, delta_s, conv_s, k_s, v_s)
```

```python
import functools
from typing import NamedTuple

import jax
import jax.numpy as jnp
from jax import lax
from jax.experimental import pallas as pl
from jax.experimental.pallas import tpu as pltpu

F32 = jnp.float32
BF16 = jnp.bfloat16

D_MODEL = 1024
DEPTH = 4
DN_HEADS = 4
DN_DK = 128
DN_DV = 128
DN_CONV = 4
DN_CHUNK = 64
SW_QHEADS = 8
SW_KVHEADS = 2
SW_HD = 64
SW_GROUP = SW_QHEADS // SW_KVHEADS
WINDOW = 128
DN_QK_W = DN_HEADS * DN_DK
DN_V_W = DN_HEADS * DN_DV
CONV_DIM = 2 * DN_QK_W + DN_V_W
SW_Q_W = SW_QHEADS * SW_HD
SW_KV_W = SW_KVHEADS * SW_HD
OFF_Z = CONV_DIM
OFF_B = OFF_Z + DN_V_W
OFF_A = OFF_B + DN_HEADS
OFF_SQ = OFF_A + DN_HEADS
PK_HEADS = 8
N_KEYS = 128
N_EXPERTS = N_KEYS * N_KEYS
PK_QDIM = 256
PK_HALF = PK_QDIM // 2
PK_TOPK = 16
EPS = 1e-6

LANES = 128
SUBLANES = 8
VMEM_LIMIT = 56 * 1024 * 1024
PROJ_W = CONV_DIM + DN_V_W + 2 * LANES + SW_Q_W + 2 * SW_KV_W
DEC_PAD = SUBLANES
N_TOP = PK_TOPK + 1
TOP_SPLIT = 4
TOP_T1_ROW = 3 * SUBLANES
TOP_ROWS = TOP_T1_ROW + 2 * SUBLANES
DN_INV_BLOCK = SUBLANES
NEG_INF = float("-inf")


def _cparams(sem):
    return pltpu.CompilerParams(dimension_semantics=sem, vmem_limit_bytes=VMEM_LIMIT)


def _silu(x):
    return x * jax.nn.sigmoid(x)


def _softplus(x):
    return jnp.maximum(x, 0.0) + jnp.log1p(jnp.exp(-jnp.abs(x)))


def _rms(x, g):
    return x * lax.rsqrt(jnp.mean(x * x, axis=-1, keepdims=True) + EPS) * g


def _dot(a, b):
    return jnp.dot(a, b, preferred_element_type=F32)


def _dot_nt(a, b):
    return lax.dot_general(a, b, (((1,), (1,)), ((), ())), preferred_element_type=F32)


def _dot_tn(a, b):
    return lax.dot_general(a, b, (((0,), (0,)), ((), ())), preferred_element_type=F32)


def _split_bf16(v):
    hi = v.astype(BF16)
    lo = (v - hi.astype(F32)).astype(BF16)
    return hi, lo


def _dot3(a, b, dot=_dot):
    a_hi, a_lo = _split_bf16(a)
    b_hi, b_lo = _split_bf16(b)
    return dot(a_hi, b_hi) + (dot(a_hi, b_lo) + dot(a_lo, b_hi))


def _ada_kernel(c_ref, w_ref, b_ref, o_ref):
    a = _silu(c_ref[...]).astype(BF16)
    o_ref[0] = _dot(a, w_ref[0].astype(BF16)) + b_ref[0]


def _ada_mod(c_all, w_ada, b_ada):
    rows = c_all.shape[0]
    tn = 1536
    return pl.pallas_call(
        _ada_kernel,
        grid=(DEPTH, 6 * D_MODEL // tn),
        in_specs=[
            pl.BlockSpec((rows, D_MODEL), lambda l, j: (0, 0)),
            pl.BlockSpec((1, D_MODEL, tn), lambda l, j: (l, 0, j)),
            pl.BlockSpec((1, 1, tn), lambda l, j: (l, 0, j)),
        ],
        out_specs=pl.BlockSpec((1, rows, tn), lambda l, j: (l, 0, j)),
        out_shape=jax.ShapeDtypeStruct((DEPTH, rows, 6 * D_MODEL), F32),
        compiler_params=_cparams(("arbitrary", "arbitrary")),
        name="ada_mod",
    )(c_all, w_ada, b_ada.reshape(DEPTH, 1, 6 * D_MODEL))


class ModSrc(NamedTuple):
    arr: jax.Array
    layer: int
    row0: int
    tiles_per_group: int | None


def _mod_spec(k, ms, tt):
    if ms.tiles_per_group is None:
        return pl.BlockSpec((1, tt, D_MODEL), lambda i, *_: (ms.layer, ms.row0 // tt + i, k))
    return pl.BlockSpec((1, SUBLANES, D_MODEL), lambda i, *_: (ms.layer, ms.row0 // SUBLANES, k))


def _mod_rows(ref, tiles_per_group):
    if tiles_per_group is None:
        return ref[0]
    return ref[0, pl.ds(pl.program_id(0) // tiles_per_group, 1), :]


def _inproj_kernel(x_ref, sh_ref, sc_ref, g_ref, w_ref,
                   qkv_ref, z_ref, b_ref, a_ref, sq_ref, skv_ref, *, tiles_per_group):
    sc = _mod_rows(sc_ref, tiles_per_group)
    sh = _mod_rows(sh_ref, tiles_per_group)
    h = _rms(x_ref[...], g_ref[...]) * (1.0 + sc) + sh
    p = _dot(h.astype(BF16), w_ref[...])
    o = 0
    for ref in (qkv_ref, z_ref, b_ref, a_ref, sq_ref, skv_ref):
        w = ref.shape[-1]
        ref[...] = p[:, o:o + w]
        o += w


def _inproj(x, ms, norm_g, w_r, tt):
    t = x.shape[0]
    widths = (CONV_DIM, DN_V_W, LANES, LANES, SW_Q_W, 2 * SW_KV_W)
    return pl.pallas_call(
        functools.partial(_inproj_kernel, tiles_per_group=ms.tiles_per_group),
        grid=(t // tt,),
        in_specs=[
            pl.BlockSpec((tt, D_MODEL), lambda i: (i, 0)),
            _mod_spec(0, ms, tt),
            _mod_spec(1, ms, tt),
            pl.BlockSpec((1, D_MODEL), lambda i: (0, 0)),
            pl.BlockSpec((D_MODEL, PROJ_W), lambda i: (0, 0)),
        ],
        out_specs=[pl.BlockSpec((tt, w), lambda i: (i, 0)) for w in widths],
        out_shape=[jax.ShapeDtypeStruct((t, w), F32) for w in widths],
        compiler_params=_cparams(("arbitrary",)),
        name="inproj",
    )(x, ms.arr, ms.arr, norm_g.reshape(1, D_MODEL), w_r)


def _dn_kernel(u_ref, z_ref, b_ref, a_ref, cw_ref, alog_ref, dtb_ref, ng_ref, cbuf_ref, s0_ref,
               o_ref, sfin_ref, ubuf, qkv, beta_s, gc_s, gct_s, state,
               *, nb_blk, lb, chunk, valid_len):
    l = pl.program_id(1)
    nc = lb // chunk
    n_scan = chunk.bit_length() - 1

    @pl.when(l == 0)
    def _():
        state[...] = s0_ref[...]
        ubuf[:, SUBLANES - (DN_CONV - 1):SUBLANES, :] = cbuf_ref[...]

    ubuf[:, SUBLANES:SUBLANES + lb, :] = u_ref[...]
    y = None
    for i in range(DN_CONV):
        start = SUBLANES - (DN_CONV - 1) + i
        term = ubuf[:, start:start + lb, :] * cw_ref[i:i + 1, :]
        y = term if y is None else y + term
    qkv[...] = _silu(y)
    ubuf[:, SUBLANES - (DN_CONV - 1):SUBLANES, :] = ubuf[:, lb + SUBLANES - (DN_CONV - 1):lb + SUBLANES, :]

    row = lax.broadcasted_iota(jnp.int32, (lb, LANES), 0)
    for nb in range(nb_blk):
        beta = jax.nn.sigmoid(b_ref[nb])
        g = -jnp.exp(alog_ref[...]) * _softplus(a_ref[nb] + dtb_ref[...])
        if valid_len < lb:
            beta = jnp.where(row < valid_len, beta, 0.0)
            g = jnp.where(row < valid_len, g, 0.0)
        gc = g
        for s in range(n_scan):
            sh = 1 << s
            gc = gc + jnp.where((row % chunk) >= sh, pltpu.roll(gc, sh, 0), 0.0)
        beta_s[nb] = beta
        gc_s[nb] = gc
        for r0 in range(0, lb, LANES):
            nrow = min(LANES, lb - r0)
            blk = gc[r0:r0 + nrow]
            if nrow < LANES:
                blk = jnp.concatenate([blk, jnp.zeros((LANES - nrow, LANES), F32)], axis=0)
            blk_t = blk.T
            for c0 in range(0, nrow, chunk):
                gct_s[nb, (r0 + c0) // chunk] = blk_t[:, c0:c0 + chunk]

    ii = lax.broadcasted_iota(jnp.int32, (chunk, chunk), 0)
    jj = lax.broadcasted_iota(jnp.int32, (chunk, chunk), 1)
    causal = ii >= jj
    strict = ii > jj
    eye = jnp.where(ii == jj, 1.0, 0.0)

    items = [(nb, c, h) for nb in range(nb_blk) for c in range(nc) for h in range(DN_HEADS)]

    def rows_of(c):
        return slice(c * chunk, (c + 1) * chunk)

    def l2n(x):
        return x * lax.rsqrt(jnp.sum(x * x, axis=-1, keepdims=True) + EPS)

    qs = [l2n(qkv[nb, rows_of(c), h * DN_DK:(h + 1) * DN_DK]) * (DN_DK ** -0.5) for nb, c, h in items]
    ks = [l2n(qkv[nb, rows_of(c), DN_QK_W + h * DN_DK:DN_QK_W + (h + 1) * DN_DK]) for nb, c, h in items]
    vs = [qkv[nb, rows_of(c), 2 * DN_QK_W + h * DN_DV:2 * DN_QK_W + (h + 1) * DN_DV] for nb, c, h in items]
    cols = [gc_s[nb, rows_of(c), h:h + 1] for nb, c, h in items]
    rws = [gct_s[nb, c, h:h + 1, :] for nb, c, h in items]
    bcols = [beta_s[nb, rows_of(c), h:h + 1] for nb, c, h in items]
    lasts = [gc_s[nb, (c + 1) * chunk - 1:(c + 1) * chunk, h:h + 1] for nb, c, h in items]
    decays = [jnp.where(causal, jnp.exp(jnp.where(causal, col - rw, 0.0)), 0.0) for col, rw in zip(cols, rws)]
    kks = [_dot3(k, k, _dot_nt) for k in ks]
    a_mats = [jnp.where(strict, bcol * kk * decay, 0.0) for bcol, kk, decay in zip(bcols, kks, decays)]
    base = min(DN_INV_BLOCK, chunk)
    same = lambda b: (ii // b) == (jj // b)
    p_mats = [jnp.where(same(base), -a, 0.0) for a in a_mats]
    t_mats = [eye + n_mat for n_mat in p_mats]
    for _ in range(base.bit_length() - 2):
        p_mats = [_dot3(p_mat, p_mat) for p_mat in p_mats]
        t_mats = [t_mat + _dot3(t_mat, p_mat) for t_mat, p_mat in zip(t_mats, p_mats)]
    b = base
    while b < chunk:
        offs = [jnp.where(same(2 * b) & jnp.logical_not(same(b)), a, 0.0) for a in a_mats]
        t_mats = [t_mat - _dot3(_dot3(t_mat, off), t_mat) for t_mat, off in zip(t_mats, offs)]
        b *= 2
    e_cols = [jnp.exp(col) for col in cols]
    sols = [_dot3(t_mat, jnp.concatenate([bcol * v, (bcol * e_col) * k], axis=-1))
            for t_mat, bcol, v, e_col, k in zip(t_mats, bcols, vs, e_cols, ks)]
    qks = [_dot_nt(q, k) * decay for q, k, decay in zip(qs, ks, decays)]
    wqs = [jnp.concatenate([sol[:, DN_DV:], q * e_col], axis=0) for sol, q, e_col in zip(sols, qs, e_cols)]
    k_decs = [k * jnp.exp(last - col) for k, last, col in zip(ks, lasts, cols)]

    for idx, (nb, c, h) in enumerate(items):
        s_old = state[nb, h]
        ws = _dot(wqs[idx], s_old)
        uu = sols[idx][:, :DN_DV] - ws[:chunk]
        o = ws[chunk:] + _dot(qks[idx], uu)
        state[nb, h] = jnp.exp(lasts[idx]) * s_old + _dot_tn(k_decs[idx], uu)
        o = _rms(o, ng_ref[...])
        zz = z_ref[nb, rows_of(c), h * DN_DV:(h + 1) * DN_DV]
        o_ref[nb, rows_of(c), h * DN_DV:(h + 1) * DN_DV] = (o * _silu(zz)).astype(o_ref.dtype)

    @pl.when(l == pl.num_programs(1) - 1)
    def _():
        sfin_ref[...] = state[...]


def _deltanet(u, z, b, a, conv_w, alog, dtb, norm_g, conv_buf, s0, *, nb_blk, lb, chunk, valid_len):
    bsz, seq, _ = u.shape
    nc = lb // chunk
    kern = functools.partial(_dn_kernel, nb_blk=nb_blk, lb=lb, chunk=chunk, valid_len=valid_len)
    tok = lambda w: pl.BlockSpec((nb_blk, lb, w), lambda i, l: (i, l, 0))
    full2 = lambda r, w: pl.BlockSpec((r, w), lambda i, l: (0, 0))
    return pl.pallas_call(
        kern,
        grid=(bsz // nb_blk, seq // lb),
        in_specs=[
            tok(CONV_DIM), tok(DN_V_W), tok(LANES), tok(LANES),
            full2(DN_CONV, CONV_DIM), full2(1, LANES), full2(1, LANES), full2(1, DN_DV),
            pl.BlockSpec((nb_blk, DN_CONV - 1, CONV_DIM), lambda i, l: (i, 0, 0)),
            pl.BlockSpec((nb_blk, DN_HEADS, DN_DK, DN_DV), lambda i, l: (i, 0, 0, 0)),
        ],
        out_specs=[
            tok(DN_V_W),
            pl.BlockSpec((nb_blk, DN_HEADS, DN_DK, DN_DV), lambda i, l: (i, 0, 0, 0)),
        ],
        out_shape=[
            jax.ShapeDtypeStruct((bsz, seq, DN_V_W), F32),
            jax.ShapeDtypeStruct((bsz, DN_HEADS, DN_DK, DN_DV), F32),
        ],
        scratch_shapes=[
            pltpu.VMEM((nb_blk, lb + SUBLANES, CONV_DIM), F32),
            pltpu.VMEM((nb_blk, lb, CONV_DIM), F32),
            pltpu.VMEM((nb_blk, lb, LANES), F32),
            pltpu.VMEM((nb_blk, lb, LANES), F32),
            pltpu.VMEM((nb_blk, nc, LANES, chunk), F32),
            pltpu.VMEM((nb_blk, DN_HEADS, DN_DK, DN_DV), F32),
        ],
        compiler_params=_cparams(("arbitrary", "arbitrary")),
        name="deltanet",
    )(u, z, b, a, conv_w, alog, dtb, norm_g, conv_buf, s0)


def _swa_prompt_kernel(sink_ref, q_ref, kvp_ref, kvc_ref, o_ref):
    i = pl.program_id(1)
    q = q_ref[0]
    kvp = kvp_ref[0]
    kvc = kvc_ref[0]
    kcat = jnp.concatenate([kvp[:, :SW_KV_W], kvc[:, :SW_KV_W]], axis=0).astype(BF16)
    vcat = jnp.concatenate([kvp[:, SW_KV_W:], kvc[:, SW_KV_W:]], axis=0).astype(BF16)
    row = lax.broadcasted_iota(jnp.int32, (WINDOW, 2 * WINDOW), 0)
    col = lax.broadcasted_iota(jnp.int32, (WINDOW, 2 * WINDOW), 1)
    valid = (col > row) & (col <= row + WINDOW) & ((col >= WINDOW) | (i > 0))
    for qh in range(SW_QHEADS):
        kv = qh // SW_GROUP
        qs = q[:, qh * SW_HD:(qh + 1) * SW_HD].astype(BF16)
        ks = kcat[:, kv * SW_HD:(kv + 1) * SW_HD]
        vs = vcat[:, kv * SW_HD:(kv + 1) * SW_HD]
        s = _dot_nt(qs, ks) * (SW_HD ** -0.5)
        s = jnp.where(valid, s, NEG_INF)
        sink = sink_ref[qh]
        m = jnp.maximum(jnp.max(s, axis=-1, keepdims=True), sink)
        p = jnp.exp(s - m)
        den = jnp.sum(p, axis=-1, keepdims=True) + jnp.exp(sink - m)
        o = _dot(p.astype(BF16), vs) / den
        o_ref[0, :, qh * SW_HD:(qh + 1) * SW_HD] = o.astype(o_ref.dtype)


def _swa_prompt(sq, skv, sinks):
    bsz, seq, _ = sq.shape
    nblk = seq // WINDOW
    return pl.pallas_call(
        _swa_prompt_kernel,
        grid=(bsz, nblk),
        in_specs=[
            pl.BlockSpec(memory_space=pltpu.SMEM),
            pl.BlockSpec((1, WINDOW, SW_Q_W), lambda b, i: (b, i, 0)),
            pl.BlockSpec((1, WINDOW, 2 * SW_KV_W), lambda b, i: (b, jnp.maximum(i - 1, 0), 0)),
            pl.BlockSpec((1, WINDOW, 2 * SW_KV_W), lambda b, i: (b, i, 0)),
        ],
        out_specs=pl.BlockSpec((1, WINDOW, SW_Q_W), lambda b, i: (b, i, 0)),
        out_shape=jax.ShapeDtypeStruct((bsz, seq, SW_Q_W), F32),
        compiler_params=_cparams(("arbitrary", "arbitrary")),
        name="swa_prompt",
    )(sinks, sq, skv, skv)


def _swa_decode_kernel(sink_ref, q_ref, kvn_ref, kb_ref, vb_ref, o_ref, ko_ref, vo_ref, *, n_new):
    q = q_ref[...]
    kvn = kvn_ref[...]
    kb = kb_ref[...]
    vb = vb_ref[...]
    nb = q.shape[0]
    p_len = kb.shape[1]
    kn = kvn[:, :, :SW_KV_W]
    vn = kvn[:, :, SW_KV_W:]
    ko_ref[:, :p_len - n_new, :] = kb[:, n_new:, :]
    ko_ref[:, p_len - n_new:, :] = kn[:, :n_new, :]
    vo_ref[:, :p_len - n_new, :] = vb[:, n_new:, :]
    vo_ref[:, p_len - n_new:, :] = vn[:, :n_new, :]
    qi = lax.broadcasted_iota(jnp.int32, (nb, DEC_PAD, p_len), 1)
    kj = lax.broadcasted_iota(jnp.int32, (nb, DEC_PAD, p_len), 2)
    valid_buf = (p_len + qi - kj) < WINDOW
    qi2 = lax.broadcasted_iota(jnp.int32, (nb, DEC_PAD, DEC_PAD), 1)
    kj2 = lax.broadcasted_iota(jnp.int32, (nb, DEC_PAD, DEC_PAD), 2)
    valid_new = (kj2 <= qi2) & (kj2 < n_new)
    for qh in range(SW_QHEADS):
        kv = qh // SW_GROUP
        hs = slice(kv * SW_HD, (kv + 1) * SW_HD)
        qs = q[:, :, qh * SW_HD:(qh + 1) * SW_HD]
        sb = jnp.einsum("bqd,bkd->bqk", qs, kb[:, :, hs], preferred_element_type=F32) * (SW_HD ** -0.5)
        sn = jnp.einsum("bqd,bkd->bqk", qs, kn[:, :, hs], preferred_element_type=F32) * (SW_HD ** -0.5)
        sb = jnp.where(valid_buf, sb, NEG_INF)
        sn = jnp.where(valid_new, sn, NEG_INF)
        sink = sink_ref[qh]
        m = jnp.maximum(jnp.maximum(jnp.max(sb, axis=-1, keepdims=True),
                                    jnp.max(sn, axis=-1, keepdims=True)), sink)
        pb = jnp.exp(sb - m)
        pn = jnp.exp(sn - m)
        den = jnp.sum(pb, axis=-1, keepdims=True) + jnp.sum(pn, axis=-1, keepdims=True) + jnp.exp(sink - m)
        o = (jnp.einsum("bqk,bkd->bqd", pb, vb[:, :, hs], preferred_element_type=F32)
             + jnp.einsum("bqk,bkd->bqd", pn, vn[:, :, hs], preferred_element_type=F32))
        o_ref[:, :, qh * SW_HD:(qh + 1) * SW_HD] = (o / den).astype(o_ref.dtype)


def _swa_decode(sq, skv, k_buf, v_buf, sinks, *, nb_blk, n_new):
    bsz = sq.shape[0]
    p_len = k_buf.shape[1]
    tok = lambda w: pl.BlockSpec((nb_blk, DEC_PAD, w), lambda i: (i, 0, 0))
    cache = pl.BlockSpec((nb_blk, p_len, SW_KV_W), lambda i: (i, 0, 0))
    return pl.pallas_call(
        functools.partial(_swa_decode_kernel, n_new=n_new),
        grid=(bsz // nb_blk,),
        in_specs=[pl.BlockSpec(memory_space=pltpu.SMEM), tok(SW_Q_W), tok(2 * SW_KV_W), cache, cache],
        out_specs=[tok(SW_Q_W), cache, cache],
        out_shape=[
            jax.ShapeDtypeStruct((bsz, DEC_PAD, SW_Q_W), F32),
            jax.ShapeDtypeStruct((bsz, p_len, SW_KV_W), F32),
            jax.ShapeDtypeStruct((bsz, p_len, SW_KV_W), F32),
        ],
        compiler_params=_cparams(("arbitrary",)),
        name="swa_decode",
    )(sinks, sq, skv, k_buf, v_buf)


def _outproj_kernel(x_ref, odn_ref, osw_ref, g1_ref, sh2_ref, sc2_ref, n2_ref, w_ref, xo_ref, h2_ref,
                    *, tiles_per_group):
    mix = (_dot(odn_ref[...].astype(BF16), w_ref[:DN_V_W, :])
           + _dot(osw_ref[...].astype(BF16), w_ref[DN_V_W:, :]))
    x = x_ref[...] + _mod_rows(g1_ref, tiles_per_group) * mix
    xo_ref[...] = x
    h2 = (_rms(x, n2_ref[...]) * (1.0 + _mod_rows(sc2_ref, tiles_per_group))
          + _mod_rows(sh2_ref, tiles_per_group))
    h2_ref[...] = h2.astype(h2_ref.dtype)


def _outproj(x, o_dn, o_sw, ms, norm_g, w_out, tt):
    t = x.shape[0]
    return pl.pallas_call(
        functools.partial(_outproj_kernel, tiles_per_group=ms.tiles_per_group),
        grid=(t // tt,),
        in_specs=[
            pl.BlockSpec((tt, D_MODEL), lambda i: (i, 0)),
            pl.BlockSpec((tt, DN_V_W), lambda i: (i, 0)),
            pl.BlockSpec((tt, SW_Q_W), lambda i: (i, 0)),
            _mod_spec(2, ms, tt),
            _mod_spec(3, ms, tt),
            _mod_spec(4, ms, tt),
            pl.BlockSpec((1, D_MODEL), lambda i: (0, 0)),
            pl.BlockSpec((D_MODEL, D_MODEL), lambda i: (0, 0)),
        ],
        out_specs=[pl.BlockSpec((tt, D_MODEL), lambda i: (i, 0))] * 2,
        out_shape=[jax.ShapeDtypeStruct((t, D_MODEL), F32), jax.ShapeDtypeStruct((t, D_MODEL), F32)],
        compiler_params=_cparams(("arbitrary",)),
        name="outproj",
    )(x, o_dn, o_sw, ms.arr, ms.arr, ms.arr, norm_g.reshape(1, D_MODEL), w_out)


def _top_values(work, n, with_rank=False):
    out = []
    rank = jnp.full(work.shape, float(n), F32) if with_rank else None
    for it in range(n):
        m = jnp.max(work, axis=0, keepdims=True)
        out.append(m)
        if it + 1 < n or with_rank:
            hit = work == m
            if with_rank:
                rank = jnp.where(hit, float(it), rank)
            work = jnp.where(hit, NEG_INF, work)
    return (out, rank) if with_rank else out


def _peer_kernel(h2_ref, x_ref, g2_ref, wq_ref, wql_ref, keys_ref, keysl_ref, u_ref, vt_ref, fg_ref, o_ref,
                 s_s, cnt_s, e1_s, rk_s, e2_s, tv_s, acc, hb_s, *, tt, et, final_norm, tiles_per_group):
    j = pl.program_id(1)
    lg_n = tt // LANES
    n_rr = et // N_KEYS
    bf16_rows = 2 * SUBLANES

    def row_bf16(row):
        one = jnp.broadcast_to(row, (bf16_rows, LANES)).astype(BF16)
        return jnp.concatenate([one] * (N_KEYS // bf16_rows), axis=0)

    @pl.when(j == 0)
    def _():
        acc[...] = jnp.zeros_like(acc)
        tv_s[...] = jnp.full(tv_s.shape, NEG_INF, F32)
        h2 = h2_ref[...]
        h_hi, h_lo = _split_bf16(h2)
        hb_s[...] = h2.T.astype(BF16)
        q = _dot(h_hi, wq_ref[...]) + (_dot(h_hi, wql_ref[...]) + _dot(h_lo, wq_ref[...]))
        q_hi, q_lo = _split_bf16(q)
        for hp in range(2 * PK_HEADS):
            cs = slice(hp * PK_HALF, (hp + 1) * PK_HALF)
            s_t = _dot_nt(keys_ref[hp], q_hi[:, cs]) + (
                _dot_nt(keys_ref[hp], q_lo[:, cs]) + _dot_nt(keysl_ref[hp], q_hi[:, cs]))
            for lg in range(lg_n):
                s_s[hp, lg] = s_t[:, lg * LANES:(lg + 1) * LANES]

        def head_body(idx, carry):
            h = idx // lg_n
            lg = idx % lg_n
            s1 = s_s[2 * h, lg]
            s2 = s_s[2 * h + 1, lg]
            t1 = _top_values(s1, N_TOP)
            t2, rank2 = _top_values(s2, N_TOP, with_rank=True)
            for it in range(N_TOP):
                tv_s[it:it + 1, :] = t2[it]
            for it in range(TOP_SPLIT, N_TOP):
                tv_s[TOP_T1_ROW + it - TOP_SPLIT:TOP_T1_ROW + it - TOP_SPLIT + 1, :] = t1[it]
            rid = lax.broadcasted_iota(jnp.int32, (SUBLANES, LANES), 0)
            cands = []

            def add_blocks(row, base, count):
                for r0 in range(0, count, SUBLANES):
                    blk = row + tv_s[base + r0:base + r0 + SUBLANES, :]
                    if count - r0 < SUBLANES:
                        blk = jnp.where(rid < count - r0, blk, NEG_INF)
                    cands.append(blk)

            for i in range(TOP_SPLIT):
                add_blocks(t1[i], 0, N_TOP // (i + 1))
            for jx in range(N_TOP // (TOP_SPLIT + 1)):
                add_blocks(t2[jx], TOP_T1_ROW, N_TOP // (jx + 1) - TOP_SPLIT)
            cand = jnp.concatenate(cands, axis=0)
            tops = _top_values(cand, N_TOP)
            thr = 0.5 * (tops[PK_TOPK - 1] + tops[PK_TOPK])
            m1 = t1[0]
            m2 = t2[0]
            zsum = jnp.sum(jnp.where(cand >= thr, jnp.exp(cand - (m1 + m2)), 0.0), axis=0, keepdims=True)
            need = thr - s1
            cnt = jnp.zeros_like(s1)
            for it in range(N_TOP):
                cnt = cnt + jnp.where(t2[it] >= need, 1.0, 0.0)
            cnt_s[h, lg] = cnt
            rk_s[h, lg] = rank2.astype(BF16)
            e1_s[h, lg] = jnp.exp(s1 - m1) * (0.5 / zsum)
            e2_s[h, lg] = jnp.exp(s2 - m2).astype(BF16)
            return carry

        lax.fori_loop(0, PK_HEADS * lg_n, head_body, 0)

    a_t = _dot(u_ref[...], hb_s[...])
    act = a_t * (1.0 + lax.erf(a_t * (2.0 ** -0.5)))
    blocks = []
    for rr in range(n_rr):
        r = j * n_rr + rr
        row_blocks = []
        for lg in range(lg_n):
            g = None
            for h in range(PK_HEADS):
                cnt_b = row_bf16(cnt_s[h, lg, pl.ds(r, 1), :])
                e1_b = row_bf16(e1_s[h, lg, pl.ds(r, 1), :])
                term = jnp.where(rk_s[h, lg] < cnt_b, e2_s[h, lg], jnp.zeros((), BF16)) * e1_b
                g = term if g is None else g + term
            blk = act[rr * N_KEYS:(rr + 1) * N_KEYS, lg * LANES:(lg + 1) * LANES] * g.astype(F32)
            row_blocks.append(blk.astype(BF16))
        blocks.append(jnp.concatenate(row_blocks, axis=1) if lg_n > 1 else row_blocks[0])
    p_t = jnp.concatenate(blocks, axis=0) if len(blocks) > 1 else blocks[0]
    acc[...] += _dot(vt_ref[...], p_t)

    @pl.when(j == pl.num_programs(1) - 1)
    def _():
        x = x_ref[...] + _mod_rows(g2_ref, tiles_per_group) * acc[...].T
        if final_norm:
            x = _rms(x, fg_ref[...])
        o_ref[...] = x


def _peer(h2, x, ms, wq, wq_lo, keys, keys_lo, u_tab, vt_tab, final_g, *, tt, et, final_norm):
    t = x.shape[0]
    lg_n = tt // LANES
    n_j = N_EXPERTS // et
    kern = functools.partial(_peer_kernel, tt=tt, et=et, final_norm=final_norm,
                             tiles_per_group=ms.tiles_per_group)
    const = lambda shape: pl.BlockSpec(shape, lambda i, j: (0,) * len(shape), pipeline_mode=pl.Buffered(1))
    return pl.pallas_call(
        kern,
        grid=(t // tt, n_j),
        in_specs=[
            pl.BlockSpec((tt, D_MODEL), lambda i, j: (i, 0)),
            pl.BlockSpec((tt, D_MODEL), lambda i, j: (i, 0)),
            _mod_spec(5, ms, tt),
            const((D_MODEL, PK_HEADS * PK_QDIM)),
            const((D_MODEL, PK_HEADS * PK_QDIM)),
            const((2 * PK_HEADS, N_KEYS, PK_HALF)),
            const((2 * PK_HEADS, N_KEYS, PK_HALF)),
            pl.BlockSpec((et, D_MODEL), lambda i, j: (j, 0)),
            pl.BlockSpec((D_MODEL, et), lambda i, j: (0, j)),
            const((1, D_MODEL)),
        ],
        out_specs=pl.BlockSpec((tt, D_MODEL), lambda i, j: (i, 0)),
        out_shape=jax.ShapeDtypeStruct((t, D_MODEL), F32),
        scratch_shapes=[
            pltpu.VMEM((2 * PK_HEADS, lg_n, N_KEYS, LANES), F32),
            pltpu.VMEM((PK_HEADS, lg_n, N_KEYS, LANES), F32),
            pltpu.VMEM((PK_HEADS, lg_n, N_KEYS, LANES), F32),
            pltpu.VMEM((PK_HEADS, lg_n, N_KEYS, LANES), BF16),
            pltpu.VMEM((PK_HEADS, lg_n, N_KEYS, LANES), BF16),
            pltpu.VMEM((TOP_ROWS, LANES), F32),
            pltpu.VMEM((D_MODEL, tt), F32),
            pltpu.VMEM((D_MODEL, tt), BF16),
        ],
        compiler_params=_cparams(("arbitrary", "arbitrary")),
        name="peer",
    )(h2, x, ms.arr, wq, wq_lo, keys, keys_lo, u_tab, vt_tab, final_g.reshape(1, D_MODEL))


def _pad_lanes(v):
    return jnp.pad(v, ((0, 0), (0, LANES - v.shape[-1])))


def _prep_w_in(w):
    return jnp.concatenate(
        [w[:, :OFF_B], _pad_lanes(w[:, OFF_B:OFF_A]), _pad_lanes(w[:, OFF_A:OFF_SQ]), w[:, OFF_SQ:]],
        axis=1).astype(BF16)


def _layer(l, x, ms, prior, p, *, seq, tt, final_norm, final_g):
    t = x.shape[0]
    bsz = t // seq
    qkv, z, bcol, acol, sq, skv = _inproj(x, ms, p["norm1_g"][l], p["w_in"][l], tt)
    r3 = lambda a: a.reshape(bsz, seq, a.shape[-1])
    qkv3 = r3(qkv)
    if prior is None:
        conv_buf = jnp.zeros((bsz, DN_CONV - 1, CONV_DIM), F32)
        s0 = jnp.zeros((bsz, DN_HEADS, DN_DK, DN_DV), F32)
        dn_cfg = dict(nb_blk=1, lb=256, chunk=DN_CHUNK, valid_len=256)
    else:
        s0, conv_buf, k_buf, v_buf, n_new = prior
        dn_cfg = dict(nb_blk=8, lb=DEC_PAD, chunk=DEC_PAD, valid_len=n_new)
        r3 = lambda a: jnp.pad(a.reshape(bsz, seq, a.shape[-1]), ((0, 0), (0, DEC_PAD - seq), (0, 0)))
    o_dn, s_new = _deltanet(r3(qkv), r3(z), r3(bcol), r3(acol), p["conv_w"][l], p["alog"][l], p["dtb"][l],
                            p["dn_norm_g"][l], conv_buf, s0, **dn_cfg)
    if prior is None:
        o_sw = _swa_prompt(r3(sq), r3(skv), p["sw_sinks"][l])
        new_conv = qkv3[:, seq - (DN_CONV - 1):, :]
        n_buf = WINDOW
        new_k = r3(skv)[:, seq - n_buf:, :SW_KV_W].reshape(bsz, n_buf, SW_KVHEADS, SW_HD)
        new_v = r3(skv)[:, seq - n_buf:, SW_KV_W:].reshape(bsz, n_buf, SW_KVHEADS, SW_HD)
    else:
        p_len = k_buf.shape[1]
        o_sw, new_k, new_v = _swa_decode(r3(sq), r3(skv), k_buf.reshape(bsz, p_len, SW_KV_W),
                                         v_buf.reshape(bsz, p_len, SW_KV_W), p["sw_sinks"][l],
                                         nb_blk=16, n_new=n_new)
        new_conv = jnp.concatenate([conv_buf, qkv3], axis=1)[:, seq:]
        new_k = new_k.reshape(bsz, p_len, SW_KVHEADS, SW_HD)
        new_v = new_v.reshape(bsz, p_len, SW_KVHEADS, SW_HD)
        o_dn, o_sw = o_dn[:, :seq], o_sw[:, :seq]
    x, h2 = _outproj(x, o_dn.reshape(t, DN_V_W), o_sw.reshape(t, SW_Q_W), ms,
                     p["norm2_g"][l], p["w_out"][l], tt)
    x = _peer(h2, x, ms, p["wq"][l], p["wq_lo"][l], p["keys"][l], p["keys_lo"][l],
              p["u"][l], p["vt"][l], final_g, tt=tt, et=512, final_norm=final_norm)
    return x, (s_new, new_conv, new_k, new_v)


def _trunk(x, mod, row0, tiles_per_group, prior_stack, p, final_g, *, seq, tt):
    new = []
    for l in range(DEPTH):
        prior = None if prior_stack is None else tuple(s[l] for s in prior_stack[:4]) + (prior_stack[4],)
        x, st = _layer(l, x, ModSrc(mod, l, row0, tiles_per_group), prior, p, seq=seq, tt=tt,
                       final_norm=(l == DEPTH - 1), final_g=final_g)
        new.append(st)
    return x, [jnp.stack(zz) for zz in zip(*new)]


def kernel(x_prompt, x_sample, state_delta, state_conv, cache_swa_k, cache_swa_v, c_prompt, c_sample,
           norm1_g, norm2_g, final_norm_g, w_ada, b_ada, w_in, conv_w, dn_a_log, dn_dt_bias, dn_norm_g,
           sw_sinks, w_out, peer_wq, peer_keys, peer_u, peer_v):
    batch, seq, _ = x_prompt.shape
    dec_b, dec_l, _ = x_sample.shape
    tt = 512

    wq_hi, wq_lo = _split_bf16(peer_wq)
    keys_hi, keys_lo = _split_bf16(peer_keys.reshape(DEPTH, 2 * PK_HEADS, N_KEYS, PK_HALF))
    p = dict(
        norm1_g=norm1_g, norm2_g=norm2_g,
        w_in=jax.vmap(_prep_w_in)(w_in),
        conv_w=conv_w,
        alog=_pad_lanes(dn_a_log).reshape(DEPTH, 1, LANES),
        dtb=_pad_lanes(dn_dt_bias).reshape(DEPTH, 1, LANES),
        dn_norm_g=dn_norm_g.reshape(DEPTH, 1, DN_DV),
        sw_sinks=sw_sinks,
        w_out=w_out.astype(BF16),
        wq=wq_hi, wq_lo=wq_lo, keys=keys_hi, keys_lo=keys_lo,
        u=peer_u.astype(BF16),
        vt=jnp.swapaxes(peer_v, 1, 2).astype(BF16),
    )

    n_dec = dec_b * dec_l
    assert batch <= SUBLANES and n_dec % tt == 0 and DN_CONV - 1 <= dec_l <= DEC_PAD
    c_dec = jnp.repeat(c_sample, dec_l, axis=0)
    pad_rows = (-(n_dec + batch)) % SUBLANES
    c_all = jnp.concatenate([c_dec, c_prompt, jnp.zeros((pad_rows, D_MODEL), F32)], axis=0)
    mod = _ada_mod(c_all, w_ada, b_ada)

    xp = x_prompt.reshape(batch * seq, D_MODEL)
    yp, (delta_p, conv_p, k_p, v_p) = _trunk(xp, mod, n_dec, seq // tt, None, p, final_norm_g, seq=seq, tt=tt)

    xs = x_sample.reshape(n_dec, D_MODEL)
    prior = (state_delta, state_conv, cache_swa_k, cache_swa_v, dec_l)
    ys, (delta_s, conv_s, k_s, v_s) = _trunk(xs, mod, 0, None, prior, p, final_norm_g, seq=dec_l, tt=tt)

    y_prompt = yp.reshape(batch, seq, D_MODEL)
    y_sample = ys.reshape(dec_b, dec_l, D_MODEL)
    return (y_prompt, y_sample, delta_p, conv_p, k_p, v_p, delta_s, conv_s, k_s, v_s)
```

```python
import functools
from typing import NamedTuple

import jax
import jax.numpy as jnp
from jax import lax
from jax.experimental import pallas as pl
from jax.experimental.pallas import tpu as pltpu

F32 = jnp.float32
BF16 = jnp.bfloat16

D_MODEL = 1024
DEPTH = 4
DN_HEADS = 4
DN_DK = 128
DN_DV = 128
DN_CONV = 4
DN_CHUNK = 64
SW_QHEADS = 8
SW_KVHEADS = 2
SW_HD = 64
SW_GROUP = SW_QHEADS // SW_KVHEADS
WINDOW = 128
DN_QK_W = DN_HEADS * DN_DK
DN_V_W = DN_HEADS * DN_DV
CONV_DIM = 2 * DN_QK_W + DN_V_W
SW_Q_W = SW_QHEADS * SW_HD
SW_KV_W = SW_KVHEADS * SW_HD
OFF_Z = CONV_DIM
OFF_B = OFF_Z + DN_V_W
OFF_A = OFF_B + DN_HEADS
OFF_SQ = OFF_A + DN_HEADS
PK_HEADS = 8
N_KEYS = 128
N_EXPERTS = N_KEYS * N_KEYS
PK_QDIM = 256
PK_HALF = PK_QDIM // 2
PK_TOPK = 16
EPS = 1e-6

LANES = 128
SUBLANES = 8
VMEM_LIMIT = 56 * 1024 * 1024
PROJ_W = CONV_DIM + DN_V_W + 2 * LANES + SW_Q_W + 2 * SW_KV_W
DEC_PAD = SUBLANES
N_TOP = PK_TOPK + 1
TOP_SPLIT = 4
TOP_T1_ROW = 3 * SUBLANES
TOP_ROWS = TOP_T1_ROW + 2 * SUBLANES
DN_INV_BLOCK = SUBLANES
NEG_INF = float("-inf")


def _cparams(sem):
    return pltpu.CompilerParams(dimension_semantics=sem, vmem_limit_bytes=VMEM_LIMIT)


def _silu(x):
    return x * jax.nn.sigmoid(x)


def _softplus(x):
    return jnp.maximum(x, 0.0) + jnp.log1p(jnp.exp(-jnp.abs(x)))


def _rms(x, g):
    return x * lax.rsqrt(jnp.mean(x * x, axis=-1, keepdims=True) + EPS) * g


def _dot(a, b):
    return jnp.dot(a, b, preferred_element_type=F32)


def _dot_nt(a, b):
    return lax.dot_general(a, b, (((1,), (1,)), ((), ())), preferred_element_type=F32)


def _dot_tn(a, b):
    return lax.dot_general(a, b, (((0,), (0,)), ((), ())), preferred_element_type=F32)


def _split_bf16(v):
    hi = v.astype(BF16)
    lo = (v - hi.astype(F32)).astype(BF16)
    return hi, lo


def _dot3(a, b, dot=_dot):
    a_hi, a_lo = _split_bf16(a)
    b_hi, b_lo = _split_bf16(b)
    return dot(a_hi, b_hi) + (dot(a_hi, b_lo) + dot(a_lo, b_hi))


def _ada_kernel(c_ref, w_ref, b_ref, o_ref):
    a = _silu(c_ref[...]).astype(BF16)
    o_ref[0] = _dot(a, w_ref[0].astype(BF16)) + b_ref[0]


def _ada_mod(c_all, w_ada, b_ada):
    rows = c_all.shape[0]
    tn = 1536
    return pl.pallas_call(
        _ada_kernel,
        grid=(DEPTH, 6 * D_MODEL // tn),
        in_specs=[
            pl.BlockSpec((rows, D_MODEL), lambda l, j: (0, 0)),
            pl.BlockSpec((1, D_MODEL, tn), lambda l, j: (l, 0, j)),
            pl.BlockSpec((1, 1, tn), lambda l, j: (l, 0, j)),
        ],
        out_specs=pl.BlockSpec((1, rows, tn), lambda l, j: (l, 0, j)),
        out_shape=jax.ShapeDtypeStruct((DEPTH, rows, 6 * D_MODEL), F32),
        compiler_params=_cparams(("arbitrary", "arbitrary")),
        name="ada_mod",
    )(c_all, w_ada, b_ada.reshape(DEPTH, 1, 6 * D_MODEL))


class ModSrc(NamedTuple):
    arr: jax.Array
    layer: int
    row0: int
    tiles_per_group: int | None


def _mod_spec(k, ms, tt):
    if ms.tiles_per_group is None:
        return pl.BlockSpec((1, tt, D_MODEL), lambda i, *_: (ms.layer, ms.row0 // tt + i, k))
    return pl.BlockSpec((1, SUBLANES, D_MODEL), lambda i, *_: (ms.layer, ms.row0 // SUBLANES, k))


def _mod_rows(ref, tiles_per_group):
    if tiles_per_group is None:
        return ref[0]
    return ref[0, pl.ds(pl.program_id(0) // tiles_per_group, 1), :]


def _inproj_kernel(x_ref, sh_ref, sc_ref, g_ref, w_ref,
                   qkv_ref, z_ref, b_ref, a_ref, sq_ref, skv_ref, *, tiles_per_group):
    sc = _mod_rows(sc_ref, tiles_per_group)
    sh = _mod_rows(sh_ref, tiles_per_group)
    h = _rms(x_ref[...], g_ref[...]) * (1.0 + sc) + sh
    p = _dot(h.astype(BF16), w_ref[...])
    o = 0
    for ref in (qkv_ref, z_ref, b_ref, a_ref, sq_ref, skv_ref):
        w = ref.shape[-1]
        ref[...] = p[:, o:o + w]
        o += w


def _inproj(x, ms, norm_g, w_r, tt):
    t = x.shape[0]
    widths = (CONV_DIM, DN_V_W, LANES, LANES, SW_Q_W, 2 * SW_KV_W)
    return pl.pallas_call(
        functools.partial(_inproj_kernel, tiles_per_group=ms.tiles_per_group),
        grid=(t // tt,),
        in_specs=[
            pl.BlockSpec((tt, D_MODEL), lambda i: (i, 0)),
            _mod_spec(0, ms, tt),
            _mod_spec(1, ms, tt),
            pl.BlockSpec((1, D_MODEL), lambda i: (0, 0)),
            pl.BlockSpec((D_MODEL, PROJ_W), lambda i: (0, 0)),
        ],
        out_specs=[pl.BlockSpec((tt, w), lambda i: (i, 0)) for w in widths],
        out_shape=[jax.ShapeDtypeStruct((t, w), F32) for w in widths],
        compiler_params=_cparams(("arbitrary",)),
        name="inproj",
    )(x, ms.arr, ms.arr, norm_g.reshape(1, D_MODEL), w_r)


def _dn_kernel(u_ref, z_ref, b_ref, a_ref, cw_ref, alog_ref, dtb_ref, ng_ref, cbuf_ref, s0_ref,
               o_ref, sfin_ref, ubuf, qkv, beta_s, gc_s, gct_s, state,
               *, nb_blk, lb, chunk, valid_len):
    l = pl.program_id(1)
    nc = lb // chunk
    n_scan = chunk.bit_length() - 1

    @pl.when(l == 0)
    def _():
        state[...] = s0_ref[...]
        ubuf[:, SUBLANES - (DN_CONV - 1):SUBLANES, :] = cbuf_ref[...]

    ubuf[:, SUBLANES:SUBLANES + lb, :] = u_ref[...]
    y = None
    for i in range(DN_CONV):
        start = SUBLANES - (DN_CONV - 1) + i
        term = ubuf[:, start:start + lb, :] * cw_ref[i:i + 1, :]
        y = term if y is None else y + term
    qkv[...] = _silu(y)
    ubuf[:, SUBLANES - (DN_CONV - 1):SUBLANES, :] = ubuf[:, lb + SUBLANES - (DN_CONV - 1):lb + SUBLANES, :]

    row = lax.broadcasted_iota(jnp.int32, (lb, LANES), 0)
    for nb in range(nb_blk):
        beta = jax.nn.sigmoid(b_ref[nb])
        g = -jnp.exp(alog_ref[...]) * _softplus(a_ref[nb] + dtb_ref[...])
        if valid_len < lb:
            beta = jnp.where(row < valid_len, beta, 0.0)
            g = jnp.where(row < valid_len, g, 0.0)
        gc = g
        for s in range(n_scan):
            sh = 1 << s
            gc = gc + jnp.where((row % chunk) >= sh, pltpu.roll(gc, sh, 0), 0.0)
        beta_s[nb] = beta
        gc_s[nb] = gc
        for r0 in range(0, lb, LANES):
            nrow = min(LANES, lb - r0)
            blk = gc[r0:r0 + nrow]
            if nrow < LANES:
                blk = jnp.concatenate([blk, jnp.zeros((LANES - nrow, LANES), F32)], axis=0)
            blk_t = blk.T
            for c0 in range(0, nrow, chunk):
                gct_s[nb, (r0 + c0) // chunk] = blk_t[:, c0:c0 + chunk]

    ii = lax.broadcasted_iota(jnp.int32, (chunk, chunk), 0)
    jj = lax.broadcasted_iota(jnp.int32, (chunk, chunk), 1)
    causal = ii >= jj
    strict = ii > jj
    eye = jnp.where(ii == jj, 1.0, 0.0)

    items = [(nb, c, h) for nb in range(nb_blk) for c in range(nc) for h in range(DN_HEADS)]

    def rows_of(c):
        return slice(c * chunk, (c + 1) * chunk)

    def l2n(x):
        return x * lax.rsqrt(jnp.sum(x * x, axis=-1, keepdims=True) + EPS)

    qs = [l2n(qkv[nb, rows_of(c), h * DN_DK:(h + 1) * DN_DK]) * (DN_DK ** -0.5) for nb, c, h in items]
    ks = [l2n(qkv[nb, rows_of(c), DN_QK_W + h * DN_DK:DN_QK_W + (h + 1) * DN_DK]) for nb, c, h in items]
    vs = [qkv[nb, rows_of(c), 2 * DN_QK_W + h * DN_DV:2 * DN_QK_W + (h + 1) * DN_DV] for nb, c, h in items]
    cols = [gc_s[nb, rows_of(c), h:h + 1] for nb, c, h in items]
    rws = [gct_s[nb, c, h:h + 1, :] for nb, c, h in items]
    bcols = [beta_s[nb, rows_of(c), h:h + 1] for nb, c, h in items]
    lasts = [gc_s[nb, (c + 1) * chunk - 1:(c + 1) * chunk, h:h + 1] for nb, c, h in items]
    decays = [jnp.where(causal, jnp.exp(jnp.where(causal, col - rw, 0.0)), 0.0) for col, rw in zip(cols, rws)]
    kks = [_dot3(k, k, _dot_nt) for k in ks]
    a_mats = [jnp.where(strict, bcol * kk * decay, 0.0) for bcol, kk, decay in zip(bcols, kks, decays)]
    base = min(DN_INV_BLOCK, chunk)
    same = lambda b: (ii // b) == (jj // b)
    p_mats = [jnp.where(same(base), -a, 0.0) for a in a_mats]
    t_mats = [eye + n_mat for n_mat in p_mats]
    for _ in range(base.bit_length() - 2):
        p_mats = [_dot3(p_mat, p_mat) for p_mat in p_mats]
        t_mats = [t_mat + _dot3(t_mat, p_mat) for t_mat, p_mat in zip(t_mats, p_mats)]
    b = base
    while b < chunk:
        offs = [jnp.where(same(2 * b) & jnp.logical_not(same(b)), a, 0.0) for a in a_mats]
        t_mats = [t_mat - _dot3(_dot3(t_mat, off), t_mat) for t_mat, off in zip(t_mats, offs)]
        b *= 2
    e_cols = [jnp.exp(col) for col in cols]
    sols = [_dot3(t_mat, jnp.concatenate([bcol * v, (bcol * e_col) * k], axis=-1))
            for t_mat, bcol, v, e_col, k in zip(t_mats, bcols, vs, e_cols, ks)]
    qks = [_dot_nt(q, k) * decay for q, k, decay in zip(qs, ks, decays)]
    wqs = [jnp.concatenate([sol[:, DN_DV:], q * e_col], axis=0) for sol, q, e_col in zip(sols, qs, e_cols)]
    k_decs = [k * jnp.exp(last - col) for k, last, col in zip(ks, lasts, cols)]

    for idx, (nb, c, h) in enumerate(items):
        s_old = state[nb, h]
        ws = _dot(wqs[idx], s_old)
        uu = sols[idx][:, :DN_DV] - ws[:chunk]
        o = ws[chunk:] + _dot(qks[idx], uu)
        state[nb, h] = jnp.exp(lasts[idx]) * s_old + _dot_tn(k_decs[idx], uu)
        o = _rms(o, ng_ref[...])
        zz = z_ref[nb, rows_of(c), h * DN_DV:(h + 1) * DN_DV]
        o_ref[nb, rows_of(c), h * DN_DV:(h + 1) * DN_DV] = (o * _silu(zz)).astype(o_ref.dtype)

    @pl.when(l == pl.num_programs(1) - 1)
    def _():
        sfin_ref[...] = state[...]


def _deltanet(u, z, b, a, conv_w, alog, dtb, norm_g, conv_buf, s0, *, nb_blk, lb, chunk, valid_len):
    bsz, seq, _ = u.shape
    nc = lb // chunk
    kern = functools.partial(_dn_kernel, nb_blk=nb_blk, lb=lb, chunk=chunk, valid_len=valid_len)
    tok = lambda w: pl.BlockSpec((nb_blk, lb, w), lambda i, l: (i, l, 0))
    full2 = lambda r, w: pl.BlockSpec((r, w), lambda i, l: (0, 0))
    return pl.pallas_call(
        kern,
        grid=(bsz // nb_blk, seq // lb),
        in_specs=[
            tok(CONV_DIM), tok(DN_V_W), tok(LANES), tok(LANES),
            full2(DN_CONV, CONV_DIM), full2(1, LANES), full2(1, LANES), full2(1, DN_DV),
            pl.BlockSpec((nb_blk, DN_CONV - 1, CONV_DIM), lambda i, l: (i, 0, 0)),
            pl.BlockSpec((nb_blk, DN_HEADS, DN_DK, DN_DV), lambda i, l: (i, 0, 0, 0)),
        ],
        out_specs=[
            tok(DN_V_W),
            pl.BlockSpec((nb_blk, DN_HEADS, DN_DK, DN_DV), lambda i, l: (i, 0, 0, 0)),
        ],
        out_shape=[
            jax.ShapeDtypeStruct((bsz, seq, DN_V_W), F32),
            jax.ShapeDtypeStruct((bsz, DN_HEADS, DN_DK, DN_DV), F32),
        ],
        scratch_shapes=[
            pltpu.VMEM((nb_blk, lb + SUBLANES, CONV_DIM), F32),
            pltpu.VMEM((nb_blk, lb, CONV_DIM), F32),
            pltpu.VMEM((nb_blk, lb, LANES), F32),
            pltpu.VMEM((nb_blk, lb, LANES), F32),
            pltpu.VMEM((nb_blk, nc, LANES, chunk), F32),
            pltpu.VMEM((nb_blk, DN_HEADS, DN_DK, DN_DV), F32),
        ],
        compiler_params=_cparams(("arbitrary", "arbitrary")),
        name="deltanet",
    )(u, z, b, a, conv_w, alog, dtb, norm_g, conv_buf, s0)


def _swa_prompt_kernel(sink_ref, q_ref, kvp_ref, kvc_ref, o_ref):
    i = pl.program_id(1)
    q = q_ref[0]
    kvp = kvp_ref[0]
    kvc = kvc_ref[0]
    kcat = jnp.concatenate([kvp[:, :SW_KV_W], kvc[:, :SW_KV_W]], axis=0).astype(BF16)
    vcat = jnp.concatenate([kvp[:, SW_KV_W:], kvc[:, SW_KV_W:]], axis=0).astype(BF16)
    row = lax.broadcasted_iota(jnp.int32, (WINDOW, 2 * WINDOW), 0)
    col = lax.broadcasted_iota(jnp.int32, (WINDOW, 2 * WINDOW), 1)
    valid = (col > row) & (col <= row + WINDOW) & ((col >= WINDOW) | (i > 0))
    for qh in range(SW_QHEADS):
        kv = qh // SW_GROUP
        qs = q[:, qh * SW_HD:(qh + 1) * SW_HD].astype(BF16)
        ks = kcat[:, kv * SW_HD:(kv + 1) * SW_HD]
        vs = vcat[:, kv * SW_HD:(kv + 1) * SW_HD]
        s = _dot_nt(qs, ks) * (SW_HD ** -0.5)
        s = jnp.where(valid, s, NEG_INF)
        sink = sink_ref[qh]
        m = jnp.maximum(jnp.max(s, axis=-1, keepdims=True), sink)
        p = jnp.exp(s - m)
        den = jnp.sum(p, axis=-1, keepdims=True) + jnp.exp(sink - m)
        o = _dot(p.astype(BF16), vs) / den
        o_ref[0, :, qh * SW_HD:(qh + 1) * SW_HD] = o.astype(o_ref.dtype)


def _swa_prompt(sq, skv, sinks):
    bsz, seq, _ = sq.shape
    nblk = seq // WINDOW
    return pl.pallas_call(
        _swa_prompt_kernel,
        grid=(bsz, nblk),
        in_specs=[
            pl.BlockSpec(memory_space=pltpu.SMEM),
            pl.BlockSpec((1, WINDOW, SW_Q_W), lambda b, i: (b, i, 0)),
            pl.BlockSpec((1, WINDOW, 2 * SW_KV_W), lambda b, i: (b, jnp.maximum(i - 1, 0), 0)),
            pl.BlockSpec((1, WINDOW, 2 * SW_KV_W), lambda b, i: (b, i, 0)),
        ],
        out_specs=pl.BlockSpec((1, WINDOW, SW_Q_W), lambda b, i: (b, i, 0)),
        out_shape=jax.ShapeDtypeStruct((bsz, seq, SW_Q_W), F32),
        compiler_params=_cparams(("arbitrary", "arbitrary")),
        name="swa_prompt",
    )(sinks, sq, skv, skv)


def _swa_decode_kernel(sink_ref, q_ref, kvn_ref, kb_ref, vb_ref, o_ref, ko_ref, vo_ref, *, n_new):
    q = q_ref[...]
    kvn = kvn_ref[...]
    kb = kb_ref[...]
    vb = vb_ref[...]
    nb = q.shape[0]
    p_len = kb.shape[1]
    kn = kvn[:, :, :SW_KV_W]
    vn = kvn[:, :, SW_KV_W:]
    ko_ref[:, :p_len - n_new, :] = kb[:, n_new:, :]
    ko_ref[:, p_len - n_new:, :] = kn[:, :n_new, :]
    vo_ref[:, :p_len - n_new, :] = vb[:, n_new:, :]
    vo_ref[:, p_len - n_new:, :] = vn[:, :n_new, :]
    qi = lax.broadcasted_iota(jnp.int32, (nb, DEC_PAD, p_len), 1)
    kj = lax.broadcasted_iota(jnp.int32, (nb, DEC_PAD, p_len), 2)
    valid_buf = (p_len + qi - kj) < WINDOW
    qi2 = lax.broadcasted_iota(jnp.int32, (nb, DEC_PAD, DEC_PAD), 1)
    kj2 = lax.broadcasted_iota(jnp.int32, (nb, DEC_PAD, DEC_PAD), 2)
    valid_new = (kj2 <= qi2) & (kj2 < n_new)
    for qh in range(SW_QHEADS):
        kv = qh // SW_GROUP
        hs = slice(kv * SW_HD, (kv + 1) * SW_HD)
        qs = q[:, :, qh * SW_HD:(qh + 1) * SW_HD]
        sb = jnp.einsum("bqd,bkd->bqk", qs, kb[:, :, hs], preferred_element_type=F32) * (SW_HD ** -0.5)
        sn = jnp.einsum("bqd,bkd->bqk", qs, kn[:, :, hs], preferred_element_type=F32) * (SW_HD ** -0.5)
        sb = jnp.where(valid_buf, sb, NEG_INF)
        sn = jnp.where(valid_new, sn, NEG_INF)
        sink = sink_ref[qh]
        m = jnp.maximum(jnp.maximum(jnp.max(sb, axis=-1, keepdims=True),
                                    jnp.max(sn, axis=-1, keepdims=True)), sink)
        pb = jnp.exp(sb - m)
        pn = jnp.exp(sn - m)
        den = jnp.sum(pb, axis=-1, keepdims=True) + jnp.sum(pn, axis=-1, keepdims=True) + jnp.exp(sink - m)
        o = (jnp.einsum("bqk,bkd->bqd", pb, vb[:, :, hs], preferred_element_type=F32)
             + jnp.einsum("bqk,bkd->bqd", pn, vn[:, :, hs], preferred_element_type=F32))
        o_ref[:, :, qh * SW_HD:(qh + 1) * SW_HD] = (o / den).astype(o_ref.dtype)


def _swa_decode(sq, skv, k_buf, v_buf, sinks, *, nb_blk, n_new):
    bsz = sq.shape[0]
    p_len = k_buf.shape[1]
    tok = lambda w: pl.BlockSpec((nb_blk, DEC_PAD, w), lambda i: (i, 0, 0))
    cache = pl.BlockSpec((nb_blk, p_len, SW_KV_W), lambda i: (i, 0, 0))
    return pl.pallas_call(
        functools.partial(_swa_decode_kernel, n_new=n_new),
        grid=(bsz // nb_blk,),
        in_specs=[pl.BlockSpec(memory_space=pltpu.SMEM), tok(SW_Q_W), tok(2 * SW_KV_W), cache, cache],
        out_specs=[tok(SW_Q_W), cache, cache],
        out_shape=[
            jax.ShapeDtypeStruct((bsz, DEC_PAD, SW_Q_W), F32),
            jax.ShapeDtypeStruct((bsz, p_len, SW_KV_W), F32),
            jax.ShapeDtypeStruct((bsz, p_len, SW_KV_W), F32),
        ],
        compiler_params=_cparams(("arbitrary",)),
        name="swa_decode",
    )(sinks, sq, skv, k_buf, v_buf)


def _outproj_kernel(x_ref, odn_ref, osw_ref, g1_ref, sh2_ref, sc2_ref, n2_ref, w_ref, xo_ref, h2_ref,
                    *, tiles_per_group):
    mix = (_dot(odn_ref[...].astype(BF16), w_ref[:DN_V_W, :])
           + _dot(osw_ref[...].astype(BF16), w_ref[DN_V_W:, :]))
    x = x_ref[...] + _mod_rows(g1_ref, tiles_per_group) * mix
    xo_ref[...] = x
    h2 = (_rms(x, n2_ref[...]) * (1.0 + _mod_rows(sc2_ref, tiles_per_group))
          + _mod_rows(sh2_ref, tiles_per_group))
    h2_ref[...] = h2.astype(h2_ref.dtype)


def _outproj(x, o_dn, o_sw, ms, norm_g, w_out, tt):
    t = x.shape[0]
    return pl.pallas_call(
        functools.partial(_outproj_kernel, tiles_per_group=ms.tiles_per_group),
        grid=(t // tt,),
        in_specs=[
            pl.BlockSpec((tt, D_MODEL), lambda i: (i, 0)),
            pl.BlockSpec((tt, DN_V_W), lambda i: (i, 0)),
            pl.BlockSpec((tt, SW_Q_W), lambda i: (i, 0)),
            _mod_spec(2, ms, tt),
            _mod_spec(3, ms, tt),
            _mod_spec(4, ms, tt),
            pl.BlockSpec((1, D_MODEL), lambda i: (0, 0)),
            pl.BlockSpec((D_MODEL, D_MODEL), lambda i: (0, 0)),
        ],
        out_specs=[pl.BlockSpec((tt, D_MODEL), lambda i: (i, 0))] * 2,
        out_shape=[jax.ShapeDtypeStruct((t, D_MODEL), F32), jax.ShapeDtypeStruct((t, D_MODEL), BF16)],
        compiler_params=_cparams(("arbitrary",)),
        name="outproj",
    )(x, o_dn, o_sw, ms.arr, ms.arr, ms.arr, norm_g.reshape(1, D_MODEL), w_out)


def _top_values(work, n, with_rank=False):
    out = []
    rank = jnp.full(work.shape, float(n), F32) if with_rank else None
    for it in range(n):
        m = jnp.max(work, axis=0, keepdims=True)
        out.append(m)
        if it + 1 < n or with_rank:
            hit = work == m
            if with_rank:
                rank = jnp.where(hit, float(it), rank)
            work = jnp.where(hit, NEG_INF, work)
    return (out, rank) if with_rank else out


def _peer_kernel(h2_ref, x_ref, g2_ref, wq_ref, keys_ref, u_ref, vt_ref, fg_ref, o_ref,
                 s_s, cnt_s, e1_s, rk_s, e2_s, tv_s, acc, hb_s, *, tt, et, final_norm, tiles_per_group):
    j = pl.program_id(1)
    lg_n = tt // LANES
    n_rr = et // N_KEYS
    bf16_rows = 2 * SUBLANES

    def row_bf16(row):
        one = jnp.broadcast_to(row, (bf16_rows, LANES)).astype(BF16)
        return jnp.concatenate([one] * (N_KEYS // bf16_rows), axis=0)

    @pl.when(j == 0)
    def _():
        acc[...] = jnp.zeros_like(acc)
        tv_s[...] = jnp.full(tv_s.shape, NEG_INF, F32)
        h2 = h2_ref[...]
        hb_s[...] = h2.astype(F32).T.astype(BF16)
        q = _dot(h2, wq_ref[...]).astype(BF16)
        for hp in range(2 * PK_HEADS):
            s_t = _dot_nt(keys_ref[hp], q[:, hp * PK_HALF:(hp + 1) * PK_HALF])
            for lg in range(lg_n):
                s_s[hp, lg] = s_t[:, lg * LANES:(lg + 1) * LANES]

        def head_body(idx, carry):
            h = idx // lg_n
            lg = idx % lg_n
            s1 = s_s[2 * h, lg]
            s2 = s_s[2 * h + 1, lg]
            t1 = _top_values(s1, N_TOP)
            t2, rank2 = _top_values(s2, N_TOP, with_rank=True)
            for it in range(N_TOP):
                tv_s[it:it + 1, :] = t2[it]
            for it in range(TOP_SPLIT, N_TOP):
                tv_s[TOP_T1_ROW + it - TOP_SPLIT:TOP_T1_ROW + it - TOP_SPLIT + 1, :] = t1[it]
            rid = lax.broadcasted_iota(jnp.int32, (SUBLANES, LANES), 0)
            cands = []

            def add_blocks(row, base, count):
                for r0 in range(0, count, SUBLANES):
                    blk = row + tv_s[base + r0:base + r0 + SUBLANES, :]
                    if count - r0 < SUBLANES:
                        blk = jnp.where(rid < count - r0, blk, NEG_INF)
                    cands.append(blk)

            for i in range(TOP_SPLIT):
                add_blocks(t1[i], 0, N_TOP // (i + 1))
            for jx in range(N_TOP // (TOP_SPLIT + 1)):
                add_blocks(t2[jx], TOP_T1_ROW, N_TOP // (jx + 1) - TOP_SPLIT)
            cand = jnp.concatenate(cands, axis=0)
            tops = _top_values(cand, N_TOP)
            thr = 0.5 * (tops[PK_TOPK - 1] + tops[PK_TOPK])
            m1 = t1[0]
            m2 = t2[0]
            zsum = jnp.sum(jnp.where(cand >= thr, jnp.exp(cand - (m1 + m2)), 0.0), axis=0, keepdims=True)
            need = thr - s1
            cnt = jnp.zeros_like(s1)
            for it in range(N_TOP):
                cnt = cnt + jnp.where(t2[it] >= need, 1.0, 0.0)
            cnt_s[h, lg] = cnt
            rk_s[h, lg] = rank2.astype(BF16)
            e1_s[h, lg] = jnp.exp(s1 - m1) * (0.5 / zsum)
            e2_s[h, lg] = jnp.exp(s2 - m2).astype(BF16)
            return carry

        lax.fori_loop(0, PK_HEADS * lg_n, head_body, 0)

    a_t = _dot(u_ref[...], hb_s[...])
    act = a_t * (1.0 + lax.erf(a_t * (2.0 ** -0.5)))
    blocks = []
    for rr in range(n_rr):
        r = j * n_rr + rr
        row_blocks = []
        for lg in range(lg_n):
            g = None
            for h in range(PK_HEADS):
                cnt_b = row_bf16(cnt_s[h, lg, pl.ds(r, 1), :])
                e1_b = row_bf16(e1_s[h, lg, pl.ds(r, 1), :])
                term = jnp.where(rk_s[h, lg] < cnt_b, e2_s[h, lg], jnp.zeros((), BF16)) * e1_b
                g = term if g is None else g + term
            blk = act[rr * N_KEYS:(rr + 1) * N_KEYS, lg * LANES:(lg + 1) * LANES] * g.astype(F32)
            row_blocks.append(blk.astype(BF16))
        blocks.append(jnp.concatenate(row_blocks, axis=1) if lg_n > 1 else row_blocks[0])
    p_t = jnp.concatenate(blocks, axis=0) if len(blocks) > 1 else blocks[0]
    acc[...] += _dot(vt_ref[...], p_t)

    @pl.when(j == pl.num_programs(1) - 1)
    def _():
        x = x_ref[...] + _mod_rows(g2_ref, tiles_per_group) * acc[...].T
        if final_norm:
            x = _rms(x, fg_ref[...])
        o_ref[...] = x


def _peer(h2, x, ms, wq, keys, u_tab, vt_tab, final_g, *, tt, et, final_norm):
    t = x.shape[0]
    lg_n = tt // LANES
    n_j = N_EXPERTS // et
    kern = functools.partial(_peer_kernel, tt=tt, et=et, final_norm=final_norm,
                             tiles_per_group=ms.tiles_per_group)
    const = lambda shape: pl.BlockSpec(shape, lambda i, j: (0,) * len(shape), pipeline_mode=pl.Buffered(1))
    return pl.pallas_call(
        kern,
        grid=(t // tt, n_j),
        in_specs=[
            pl.BlockSpec((tt, D_MODEL), lambda i, j: (i, 0)),
            pl.BlockSpec((tt, D_MODEL), lambda i, j: (i, 0)),
            _mod_spec(5, ms, tt),
            const((D_MODEL, PK_HEADS * PK_QDIM)),
            const((2 * PK_HEADS, N_KEYS, PK_HALF)),
            pl.BlockSpec((et, D_MODEL), lambda i, j: (j, 0)),
            pl.BlockSpec((D_MODEL, et), lambda i, j: (0, j)),
            const((1, D_MODEL)),
        ],
        out_specs=pl.BlockSpec((tt, D_MODEL), lambda i, j: (i, 0)),
        out_shape=jax.ShapeDtypeStruct((t, D_MODEL), F32),
        scratch_shapes=[
            pltpu.VMEM((2 * PK_HEADS, lg_n, N_KEYS, LANES), F32),
            pltpu.VMEM((PK_HEADS, lg_n, N_KEYS, LANES), F32),
            pltpu.VMEM((PK_HEADS, lg_n, N_KEYS, LANES), F32),
            pltpu.VMEM((PK_HEADS, lg_n, N_KEYS, LANES), BF16),
            pltpu.VMEM((PK_HEADS, lg_n, N_KEYS, LANES), BF16),
            pltpu.VMEM((TOP_ROWS, LANES), F32),
            pltpu.VMEM((D_MODEL, tt), F32),
            pltpu.VMEM((D_MODEL, tt), BF16),
        ],
        compiler_params=_cparams(("arbitrary", "arbitrary")),
        name="peer",
    )(h2, x, ms.arr, wq, keys, u_tab, vt_tab, final_g.reshape(1, D_MODEL))


def _pad_lanes(v):
    return jnp.pad(v, ((0, 0), (0, LANES - v.shape[-1])))


def _prep_w_in(w):
    return jnp.concatenate(
        [w[:, :OFF_B], _pad_lanes(w[:, OFF_B:OFF_A]), _pad_lanes(w[:, OFF_A:OFF_SQ]), w[:, OFF_SQ:]],
        axis=1).astype(BF16)


def _layer(l, x, ms, prior, p, *, seq, tt, final_norm, final_g):
    t = x.shape[0]
    bsz = t // seq
    qkv, z, bcol, acol, sq, skv = _inproj(x, ms, p["norm1_g"][l], p["w_in"][l], tt)
    r3 = lambda a: a.reshape(bsz, seq, a.shape[-1])
    qkv3 = r3(qkv)
    if prior is None:
        conv_buf = jnp.zeros((bsz, DN_CONV - 1, CONV_DIM), F32)
        s0 = jnp.zeros((bsz, DN_HEADS, DN_DK, DN_DV), F32)
        dn_cfg = dict(nb_blk=1, lb=256, chunk=DN_CHUNK, valid_len=256)
    else:
        s0, conv_buf, k_buf, v_buf, n_new = prior
        dn_cfg = dict(nb_blk=8, lb=DEC_PAD, chunk=DEC_PAD, valid_len=n_new)
        r3 = lambda a: jnp.pad(a.reshape(bsz, seq, a.shape[-1]), ((0, 0), (0, DEC_PAD - seq), (0, 0)))
    o_dn, s_new = _deltanet(r3(qkv), r3(z), r3(bcol), r3(acol), p["conv_w"][l], p["alog"][l], p["dtb"][l],
                            p["dn_norm_g"][l], conv_buf, s0, **dn_cfg)
    if prior is None:
        o_sw = _swa_prompt(r3(sq), r3(skv), p["sw_sinks"][l])
        new_conv = qkv3[:, seq - (DN_CONV - 1):, :]
        n_buf = WINDOW
        new_k = r3(skv)[:, seq - n_buf:, :SW_KV_W].reshape(bsz, n_buf, SW_KVHEADS, SW_HD)
        new_v = r3(skv)[:, seq - n_buf:, SW_KV_W:].reshape(bsz, n_buf, SW_KVHEADS, SW_HD)
    else:
        p_len = k_buf.shape[1]
        o_sw, new_k, new_v = _swa_decode(r3(sq), r3(skv), k_buf.reshape(bsz, p_len, SW_KV_W),
                                         v_buf.reshape(bsz, p_len, SW_KV_W), p["sw_sinks"][l],
                                         nb_blk=16, n_new=n_new)
        new_conv = jnp.concatenate([conv_buf, qkv3], axis=1)[:, seq:]
        new_k = new_k.reshape(bsz, p_len, SW_KVHEADS, SW_HD)
        new_v = new_v.reshape(bsz, p_len, SW_KVHEADS, SW_HD)
        o_dn, o_sw = o_dn[:, :seq], o_sw[:, :seq]
    x, h2 = _outproj(x, o_dn.reshape(t, DN_V_W), o_sw.reshape(t, SW_Q_W), ms,
                     p["norm2_g"][l], p["w_out"][l], tt)
    x = _peer(h2, x, ms, p["wq"][l], p["keys"][l], p["u"][l], p["vt"][l], final_g,
              tt=tt, et=512, final_norm=final_norm)
    return x, (s_new, new_conv, new_k, new_v)


def _trunk(x, mod, row0, tiles_per_group, prior_stack, p, final_g, *, seq, tt):
    new = []
    for l in range(DEPTH):
        prior = None if prior_stack is None else tuple(s[l] for s in prior_stack[:4]) + (prior_stack[4],)
        x, st = _layer(l, x, ModSrc(mod, l, row0, tiles_per_group), prior, p, seq=seq, tt=tt,
                       final_norm=(l == DEPTH - 1), final_g=final_g)
        new.append(st)
    return x, [jnp.stack(zz) for zz in zip(*new)]


def kernel(x_prompt, x_sample, state_delta, state_conv, cache_swa_k, cache_swa_v, c_prompt, c_sample,
           norm1_g, norm2_g, final_norm_g, w_ada, b_ada, w_in, conv_w, dn_a_log, dn_dt_bias, dn_norm_g,
           sw_sinks, w_out, peer_wq, peer_keys, peer_u, peer_v):
    batch, seq, _ = x_prompt.shape
    dec_b, dec_l, _ = x_sample.shape
    tt = 512

    p = dict(
        norm1_g=norm1_g, norm2_g=norm2_g,
        w_in=jax.vmap(_prep_w_in)(w_in),
        conv_w=conv_w,
        alog=_pad_lanes(dn_a_log).reshape(DEPTH, 1, LANES),
        dtb=_pad_lanes(dn_dt_bias).reshape(DEPTH, 1, LANES),
        dn_norm_g=dn_norm_g.reshape(DEPTH, 1, DN_DV),
        sw_sinks=sw_sinks,
        w_out=w_out.astype(BF16),
        wq=peer_wq.astype(BF16),
        keys=peer_keys.reshape(DEPTH, 2 * PK_HEADS, N_KEYS, PK_HALF).astype(BF16),
        u=peer_u.astype(BF16),
        vt=jnp.swapaxes(peer_v, 1, 2).astype(BF16),
    )

    n_dec = dec_b * dec_l
    assert batch <= SUBLANES and n_dec % tt == 0 and DN_CONV - 1 <= dec_l <= DEC_PAD
    c_dec = jnp.repeat(c_sample, dec_l, axis=0)
    pad_rows = (-(n_dec + batch)) % SUBLANES
    c_all = jnp.concatenate([c_dec, c_prompt, jnp.zeros((pad_rows, D_MODEL), F32)], axis=0)
    mod = _ada_mod(c_all, w_ada, b_ada)

    xp = x_prompt.reshape(batch * seq, D_MODEL)
    yp, (delta_p, conv_p, k_p, v_p) = _trunk(xp, mod, n_dec, seq // tt, None, p, final_norm_g, seq=seq, tt=tt)

    xs = x_sample.reshape(n_dec, D_MODEL)
    prior = (state_delta, state_conv, cache_swa_k, cache_swa_v, dec_l)
    ys, (delta_s, conv_s, k_s, v_s) = _trunk(xs, mod, 0, None, prior, p, final_norm_g, seq=dec_l, tt=tt)

    y_prompt = yp.reshape(batch, seq, D_MODEL)
    y_sample = ys.reshape(dec_b, dec_l, D_MODEL)
    return (y_prompt, y_sample, delta_p, conv_p, k_p, v_p, delta_s, conv_s, k_s, v_s)
```

```python
import functools
from typing import NamedTuple

import jax
import jax.numpy as jnp
from jax import lax
from jax.experimental import pallas as pl
from jax.experimental.pallas import tpu as pltpu

F32 = jnp.float32
BF16 = jnp.bfloat16

D_MODEL = 1024
DEPTH = 4
DN_HEADS = 4
DN_DK = 128
DN_DV = 128
DN_CONV = 4
DN_CHUNK = 64
SW_QHEADS = 8
SW_KVHEADS = 2
SW_HD = 64
SW_GROUP = SW_QHEADS // SW_KVHEADS
WINDOW = 128
DN_QK_W = DN_HEADS * DN_DK
DN_V_W = DN_HEADS * DN_DV
CONV_DIM = 2 * DN_QK_W + DN_V_W
SW_Q_W = SW_QHEADS * SW_HD
SW_KV_W = SW_KVHEADS * SW_HD
OFF_Z = CONV_DIM
OFF_B = OFF_Z + DN_V_W
OFF_A = OFF_B + DN_HEADS
OFF_SQ = OFF_A + DN_HEADS
PK_HEADS = 8
N_KEYS = 128
N_EXPERTS = N_KEYS * N_KEYS
PK_QDIM = 256
PK_HALF = PK_QDIM // 2
PK_TOPK = 16
EPS = 1e-6

LANES = 128
SUBLANES = 8
VMEM_LIMIT = 56 * 1024 * 1024
PROJ_W = CONV_DIM + DN_V_W + 2 * LANES + SW_Q_W + 2 * SW_KV_W
DEC_PAD = SUBLANES
N_TOP = PK_TOPK + 1
TOP_SPLIT = 4
TOP_T1_ROW = 3 * SUBLANES
TOP_ROWS = TOP_T1_ROW + 2 * SUBLANES
DN_INV_BLOCK = SUBLANES
NEG_INF = float("-inf")


def _cparams(sem):
    return pltpu.CompilerParams(dimension_semantics=sem, vmem_limit_bytes=VMEM_LIMIT)


def _silu(x):
    return x * jax.nn.sigmoid(x)


def _softplus(x):
    return jnp.maximum(x, 0.0) + jnp.log1p(jnp.exp(-jnp.abs(x)))


def _rms(x, g):
    return x * lax.rsqrt(jnp.mean(x * x, axis=-1, keepdims=True) + EPS) * g


def _dot(a, b):
    return jnp.dot(a, b, preferred_element_type=F32)


def _dot_nt(a, b):
    return lax.dot_general(a, b, (((1,), (1,)), ((), ())), preferred_element_type=F32)


def _dot_tn(a, b):
    return lax.dot_general(a, b, (((0,), (0,)), ((), ())), preferred_element_type=F32)


def _split_bf16(v):
    hi = v.astype(BF16)
    lo = (v - hi.astype(F32)).astype(BF16)
    return hi, lo


def _dot3(a, b, dot=_dot):
    a_hi, a_lo = _split_bf16(a)
    b_hi, b_lo = _split_bf16(b)
    return dot(a_hi, b_hi) + (dot(a_hi, b_lo) + dot(a_lo, b_hi))


def _ada_kernel(c_ref, w_ref, b_ref, o_ref):
    a = _silu(c_ref[...]).astype(BF16)
    o_ref[0] = _dot(a, w_ref[0].astype(BF16)) + b_ref[0]


def _ada_mod(c_all, w_ada, b_ada):
    rows = c_all.shape[0]
    tn = 1536
    return pl.pallas_call(
        _ada_kernel,
        grid=(DEPTH, 6 * D_MODEL // tn),
        in_specs=[
            pl.BlockSpec((rows, D_MODEL), lambda l, j: (0, 0)),
            pl.BlockSpec((1, D_MODEL, tn), lambda l, j: (l, 0, j)),
            pl.BlockSpec((1, 1, tn), lambda l, j: (l, 0, j)),
        ],
        out_specs=pl.BlockSpec((1, rows, tn), lambda l, j: (l, 0, j)),
        out_shape=jax.ShapeDtypeStruct((DEPTH, rows, 6 * D_MODEL), F32),
        compiler_params=_cparams(("arbitrary", "arbitrary")),
        name="ada_mod",
    )(c_all, w_ada, b_ada.reshape(DEPTH, 1, 6 * D_MODEL))


class ModSrc(NamedTuple):
    arr: jax.Array
    layer: int
    row0: int
    tiles_per_group: int | None


def _mod_spec(k, ms, tt):
    if ms.tiles_per_group is None:
        return pl.BlockSpec((1, tt, D_MODEL), lambda i, *_: (ms.layer, ms.row0 // tt + i, k))
    return pl.BlockSpec((1, SUBLANES, D_MODEL), lambda i, *_: (ms.layer, ms.row0 // SUBLANES, k))


def _mod_rows(ref, tiles_per_group):
    if tiles_per_group is None:
        return ref[0]
    return ref[0, pl.ds(pl.program_id(0) // tiles_per_group, 1), :]


def _inproj_kernel(x_ref, sh_ref, sc_ref, g_ref, w_ref,
                   qkv_ref, z_ref, b_ref, a_ref, sq_ref, skv_ref, *, tiles_per_group):
    sc = _mod_rows(sc_ref, tiles_per_group)
    sh = _mod_rows(sh_ref, tiles_per_group)
    h = _rms(x_ref[...], g_ref[...]) * (1.0 + sc) + sh
    p = _dot(h.astype(BF16), w_ref[...])
    o = 0
    for ref in (qkv_ref, z_ref, b_ref, a_ref, sq_ref, skv_ref):
        w = ref.shape[-1]
        ref[...] = p[:, o:o + w]
        o += w


def _inproj(x, ms, norm_g, w_r, tt):
    t = x.shape[0]
    widths = (CONV_DIM, DN_V_W, LANES, LANES, SW_Q_W, 2 * SW_KV_W)
    return pl.pallas_call(
        functools.partial(_inproj_kernel, tiles_per_group=ms.tiles_per_group),
        grid=(t // tt,),
        in_specs=[
            pl.BlockSpec((tt, D_MODEL), lambda i: (i, 0)),
            _mod_spec(0, ms, tt),
            _mod_spec(1, ms, tt),
            pl.BlockSpec((1, D_MODEL), lambda i: (0, 0)),
            pl.BlockSpec((None, D_MODEL, PROJ_W), lambda i: (ms.layer, 0, 0)),
        ],
        out_specs=[pl.BlockSpec((tt, w), lambda i: (i, 0)) for w in widths],
        out_shape=[jax.ShapeDtypeStruct((t, w), F32) for w in widths],
        compiler_params=_cparams(("arbitrary",)),
        name="inproj",
    )(x, ms.arr, ms.arr, norm_g.reshape(1, D_MODEL), w_r)


def _dn_kernel(u_ref, z_ref, b_ref, a_ref, cw_ref, alog_ref, dtb_ref, ng_ref, cbuf_ref, s0_ref,
               o_ref, sfin_ref, ubuf, qkv, beta_s, gc_s, gct_s, state,
               *, nb_blk, lb, chunk, valid_len):
    l = pl.program_id(1)
    nc = lb // chunk
    n_scan = chunk.bit_length() - 1

    @pl.when(l == 0)
    def _():
        state[...] = s0_ref[...]
        ubuf[:, SUBLANES - (DN_CONV - 1):SUBLANES, :] = cbuf_ref[...]

    ubuf[:, SUBLANES:SUBLANES + lb, :] = u_ref[...]
    y = None
    for i in range(DN_CONV):
        start = SUBLANES - (DN_CONV - 1) + i
        term = ubuf[:, start:start + lb, :] * cw_ref[i:i + 1, :]
        y = term if y is None else y + term
    qkv[...] = _silu(y)
    ubuf[:, SUBLANES - (DN_CONV - 1):SUBLANES, :] = ubuf[:, lb + SUBLANES - (DN_CONV - 1):lb + SUBLANES, :]

    row = lax.broadcasted_iota(jnp.int32, (lb, LANES), 0)
    for nb in range(nb_blk):
        beta = jax.nn.sigmoid(b_ref[nb])
        g = -jnp.exp(alog_ref[...]) * _softplus(a_ref[nb] + dtb_ref[...])
        if valid_len < lb:
            beta = jnp.where(row < valid_len, beta, 0.0)
            g = jnp.where(row < valid_len, g, 0.0)
        gc = g
        for s in range(n_scan):
            sh = 1 << s
            gc = gc + jnp.where((row % chunk) >= sh, pltpu.roll(gc, sh, 0), 0.0)
        beta_s[nb] = beta
        gc_s[nb] = gc
        for r0 in range(0, lb, LANES):
            nrow = min(LANES, lb - r0)
            blk = gc[r0:r0 + nrow]
            if nrow < LANES:
                blk = jnp.concatenate([blk, jnp.zeros((LANES - nrow, LANES), F32)], axis=0)
            blk_t = blk.T
            for c0 in range(0, nrow, chunk):
                gct_s[nb, (r0 + c0) // chunk] = blk_t[:, c0:c0 + chunk]

    ii = lax.broadcasted_iota(jnp.int32, (chunk, chunk), 0)
    jj = lax.broadcasted_iota(jnp.int32, (chunk, chunk), 1)
    causal = ii >= jj
    strict = ii > jj
    eye = jnp.where(ii == jj, 1.0, 0.0)

    items = [(nb, c, h) for nb in range(nb_blk) for c in range(nc) for h in range(DN_HEADS)]

    def rows_of(c):
        return slice(c * chunk, (c + 1) * chunk)

    def l2n(x):
        return x * lax.rsqrt(jnp.sum(x * x, axis=-1, keepdims=True) + EPS)

    qs = [l2n(qkv[nb, rows_of(c), h * DN_DK:(h + 1) * DN_DK]) * (DN_DK ** -0.5) for nb, c, h in items]
    ks = [l2n(qkv[nb, rows_of(c), DN_QK_W + h * DN_DK:DN_QK_W + (h + 1) * DN_DK]) for nb, c, h in items]
    vs = [qkv[nb, rows_of(c), 2 * DN_QK_W + h * DN_DV:2 * DN_QK_W + (h + 1) * DN_DV] for nb, c, h in items]
    cols = [gc_s[nb, rows_of(c), h:h + 1] for nb, c, h in items]
    rws = [gct_s[nb, c, h:h + 1, :] for nb, c, h in items]
    bcols = [beta_s[nb, rows_of(c), h:h + 1] for nb, c, h in items]
    lasts = [gc_s[nb, (c + 1) * chunk - 1:(c + 1) * chunk, h:h + 1] for nb, c, h in items]
    decays = [jnp.where(causal, jnp.exp(jnp.where(causal, col - rw, 0.0)), 0.0) for col, rw in zip(cols, rws)]
    kks = [_dot3(k, k, _dot_nt) for k in ks]
    a_mats = [jnp.where(strict, bcol * kk * decay, 0.0) for bcol, kk, decay in zip(bcols, kks, decays)]
    base = min(DN_INV_BLOCK, chunk)
    same = lambda b: (ii // b) == (jj // b)
    p_mats = [jnp.where(same(base), -a, 0.0) for a in a_mats]
    t_mats = [eye + n_mat for n_mat in p_mats]
    for _ in range(base.bit_length() - 2):
        p_mats = [_dot3(p_mat, p_mat) for p_mat in p_mats]
        t_mats = [t_mat + _dot3(t_mat, p_mat) for t_mat, p_mat in zip(t_mats, p_mats)]
    b = base
    while b < chunk:
        offs = [jnp.where(same(2 * b) & jnp.logical_not(same(b)), a, 0.0) for a in a_mats]
        t_mats = [t_mat - _dot3(_dot3(t_mat, off), t_mat) for t_mat, off in zip(t_mats, offs)]
        b *= 2
    e_cols = [jnp.exp(col) for col in cols]
    sols = [_dot3(t_mat, jnp.concatenate([bcol * v, (bcol * e_col) * k], axis=-1))
            for t_mat, bcol, v, e_col, k in zip(t_mats, bcols, vs, e_cols, ks)]
    qks = [_dot_nt(q, k) * decay for q, k, decay in zip(qs, ks, decays)]
    wqs = [jnp.concatenate([sol[:, DN_DV:], q * e_col], axis=0) for sol, q, e_col in zip(sols, qs, e_cols)]
    k_decs = [k * jnp.exp(last - col) for k, last, col in zip(ks, lasts, cols)]

    for idx, (nb, c, h) in enumerate(items):
        s_old = state[nb, h]
        ws = _dot(wqs[idx], s_old)
        uu = sols[idx][:, :DN_DV] - ws[:chunk]
        o = ws[chunk:] + _dot(qks[idx], uu)
        state[nb, h] = jnp.exp(lasts[idx]) * s_old + _dot_tn(k_decs[idx], uu)
        o = _rms(o, ng_ref[...])
        zz = z_ref[nb, rows_of(c), h * DN_DV:(h + 1) * DN_DV]
        o_ref[nb, rows_of(c), h * DN_DV:(h + 1) * DN_DV] = (o * _silu(zz)).astype(o_ref.dtype)

    @pl.when(l == pl.num_programs(1) - 1)
    def _():
        sfin_ref[...] = state[...]


def _deltanet(u, z, b, a, conv_w, alog, dtb, norm_g, conv_buf, s0, *, nb_blk, lb, chunk, valid_len):
    bsz, seq, _ = u.shape
    nc = lb // chunk
    kern = functools.partial(_dn_kernel, nb_blk=nb_blk, lb=lb, chunk=chunk, valid_len=valid_len)
    tok = lambda w: pl.BlockSpec((nb_blk, lb, w), lambda i, l: (i, l, 0))
    full2 = lambda r, w: pl.BlockSpec((r, w), lambda i, l: (0, 0))
    return pl.pallas_call(
        kern,
        grid=(bsz // nb_blk, seq // lb),
        in_specs=[
            tok(CONV_DIM), tok(DN_V_W), tok(LANES), tok(LANES),
            full2(DN_CONV, CONV_DIM), full2(1, LANES), full2(1, LANES), full2(1, DN_DV),
            pl.BlockSpec((nb_blk, DN_CONV - 1, CONV_DIM), lambda i, l: (i, 0, 0)),
            pl.BlockSpec((nb_blk, DN_HEADS, DN_DK, DN_DV), lambda i, l: (i, 0, 0, 0)),
        ],
        out_specs=[
            tok(DN_V_W),
            pl.BlockSpec((nb_blk, DN_HEADS, DN_DK, DN_DV), lambda i, l: (i, 0, 0, 0)),
        ],
        out_shape=[
            jax.ShapeDtypeStruct((bsz, seq, DN_V_W), F32),
            jax.ShapeDtypeStruct((bsz, DN_HEADS, DN_DK, DN_DV), F32),
        ],
        scratch_shapes=[
            pltpu.VMEM((nb_blk, lb + SUBLANES, CONV_DIM), F32),
            pltpu.VMEM((nb_blk, lb, CONV_DIM), F32),
            pltpu.VMEM((nb_blk, lb, LANES), F32),
            pltpu.VMEM((nb_blk, lb, LANES), F32),
            pltpu.VMEM((nb_blk, nc, LANES, chunk), F32),
            pltpu.VMEM((nb_blk, DN_HEADS, DN_DK, DN_DV), F32),
        ],
        compiler_params=_cparams(("arbitrary", "arbitrary")),
        name="deltanet",
    )(u, z, b, a, conv_w, alog, dtb, norm_g, conv_buf, s0)


def _swa_prompt_kernel(sink_ref, q_ref, kvp_ref, kvc_ref, o_ref):
    i = pl.program_id(1)
    q = q_ref[0]
    kvp = kvp_ref[0]
    kvc = kvc_ref[0]
    kcat = jnp.concatenate([kvp[:, :SW_KV_W], kvc[:, :SW_KV_W]], axis=0).astype(BF16)
    vcat = jnp.concatenate([kvp[:, SW_KV_W:], kvc[:, SW_KV_W:]], axis=0).astype(BF16)
    row = lax.broadcasted_iota(jnp.int32, (WINDOW, 2 * WINDOW), 0)
    col = lax.broadcasted_iota(jnp.int32, (WINDOW, 2 * WINDOW), 1)
    valid = (col > row) & (col <= row + WINDOW) & ((col >= WINDOW) | (i > 0))
    for qh in range(SW_QHEADS):
        kv = qh // SW_GROUP
        qs = q[:, qh * SW_HD:(qh + 1) * SW_HD].astype(BF16)
        ks = kcat[:, kv * SW_HD:(kv + 1) * SW_HD]
        vs = vcat[:, kv * SW_HD:(kv + 1) * SW_HD]
        s = _dot_nt(qs, ks) * (SW_HD ** -0.5)
        s = jnp.where(valid, s, NEG_INF)
        sink = sink_ref[qh]
        m = jnp.maximum(jnp.max(s, axis=-1, keepdims=True), sink)
        p = jnp.exp(s - m)
        den = jnp.sum(p, axis=-1, keepdims=True) + jnp.exp(sink - m)
        o = _dot(p.astype(BF16), vs) / den
        o_ref[0, :, qh * SW_HD:(qh + 1) * SW_HD] = o.astype(o_ref.dtype)


def _swa_prompt(sq, skv, sinks):
    bsz, seq, _ = sq.shape
    nblk = seq // WINDOW
    return pl.pallas_call(
        _swa_prompt_kernel,
        grid=(bsz, nblk),
        in_specs=[
            pl.BlockSpec(memory_space=pltpu.SMEM),
            pl.BlockSpec((1, WINDOW, SW_Q_W), lambda b, i: (b, i, 0)),
            pl.BlockSpec((1, WINDOW, 2 * SW_KV_W), lambda b, i: (b, jnp.maximum(i - 1, 0), 0)),
            pl.BlockSpec((1, WINDOW, 2 * SW_KV_W), lambda b, i: (b, i, 0)),
        ],
        out_specs=pl.BlockSpec((1, WINDOW, SW_Q_W), lambda b, i: (b, i, 0)),
        out_shape=jax.ShapeDtypeStruct((bsz, seq, SW_Q_W), F32),
        compiler_params=_cparams(("arbitrary", "arbitrary")),
        name="swa_prompt",
    )(sinks, sq, skv, skv)


def _swa_decode_kernel(sink_ref, q_ref, kvn_ref, kb_ref, vb_ref, o_ref, ko_ref, vo_ref, *, n_new):
    q = q_ref[...]
    kvn = kvn_ref[...]
    kb = kb_ref[...]
    vb = vb_ref[...]
    nb = q.shape[0]
    p_len = kb.shape[1]
    kn = kvn[:, :, :SW_KV_W]
    vn = kvn[:, :, SW_KV_W:]
    ko_ref[:, :p_len - n_new, :] = kb[:, n_new:, :]
    ko_ref[:, p_len - n_new:, :] = kn[:, :n_new, :]
    vo_ref[:, :p_len - n_new, :] = vb[:, n_new:, :]
    vo_ref[:, p_len - n_new:, :] = vn[:, :n_new, :]
    qi = lax.broadcasted_iota(jnp.int32, (nb, DEC_PAD, p_len), 1)
    kj = lax.broadcasted_iota(jnp.int32, (nb, DEC_PAD, p_len), 2)
    valid_buf = (p_len + qi - kj) < WINDOW
    qi2 = lax.broadcasted_iota(jnp.int32, (nb, DEC_PAD, DEC_PAD), 1)
    kj2 = lax.broadcasted_iota(jnp.int32, (nb, DEC_PAD, DEC_PAD), 2)
    valid_new = (kj2 <= qi2) & (kj2 < n_new)
    for qh in range(SW_QHEADS):
        kv = qh // SW_GROUP
        hs = slice(kv * SW_HD, (kv + 1) * SW_HD)
        qs = q[:, :, qh * SW_HD:(qh + 1) * SW_HD]
        sb = jnp.einsum("bqd,bkd->bqk", qs, kb[:, :, hs], preferred_element_type=F32) * (SW_HD ** -0.5)
        sn = jnp.einsum("bqd,bkd->bqk", qs, kn[:, :, hs], preferred_element_type=F32) * (SW_HD ** -0.5)
        sb = jnp.where(valid_buf, sb, NEG_INF)
        sn = jnp.where(valid_new, sn, NEG_INF)
        sink = sink_ref[qh]
        m = jnp.maximum(jnp.maximum(jnp.max(sb, axis=-1, keepdims=True),
                                    jnp.max(sn, axis=-1, keepdims=True)), sink)
        pb = jnp.exp(sb - m)
        pn = jnp.exp(sn - m)
        den = jnp.sum(pb, axis=-1, keepdims=True) + jnp.sum(pn, axis=-1, keepdims=True) + jnp.exp(sink - m)
        o = (jnp.einsum("bqk,bkd->bqd", pb, vb[:, :, hs], preferred_element_type=F32)
             + jnp.einsum("bqk,bkd->bqd", pn, vn[:, :, hs], preferred_element_type=F32))
        o_ref[:, :, qh * SW_HD:(qh + 1) * SW_HD] = (o / den).astype(o_ref.dtype)


def _swa_decode(sq, skv, k_buf, v_buf, sinks, *, nb_blk, n_new):
    bsz = sq.shape[0]
    p_len = k_buf.shape[1]
    tok = lambda w: pl.BlockSpec((nb_blk, DEC_PAD, w), lambda i: (i, 0, 0))
    cache = pl.BlockSpec((nb_blk, p_len, SW_KV_W), lambda i: (i, 0, 0))
    return pl.pallas_call(
        functools.partial(_swa_decode_kernel, n_new=n_new),
        grid=(bsz // nb_blk,),
        in_specs=[pl.BlockSpec(memory_space=pltpu.SMEM), tok(SW_Q_W), tok(2 * SW_KV_W), cache, cache],
        out_specs=[tok(SW_Q_W), cache, cache],
        out_shape=[
            jax.ShapeDtypeStruct((bsz, DEC_PAD, SW_Q_W), F32),
            jax.ShapeDtypeStruct((bsz, p_len, SW_KV_W), F32),
            jax.ShapeDtypeStruct((bsz, p_len, SW_KV_W), F32),
        ],
        compiler_params=_cparams(("arbitrary",)),
        name="swa_decode",
    )(sinks, sq, skv, k_buf, v_buf)


def _outproj_kernel(x_ref, odn_ref, osw_ref, g1_ref, sh2_ref, sc2_ref, n2_ref, w_ref, xo_ref, h2_ref,
                    *, tiles_per_group):
    mix = (_dot(odn_ref[...].astype(BF16), w_ref[:DN_V_W, :])
           + _dot(osw_ref[...].astype(BF16), w_ref[DN_V_W:, :]))
    x = x_ref[...] + _mod_rows(g1_ref, tiles_per_group) * mix
    xo_ref[...] = x
    h2 = (_rms(x, n2_ref[...]) * (1.0 + _mod_rows(sc2_ref, tiles_per_group))
          + _mod_rows(sh2_ref, tiles_per_group))
    h2_ref[...] = h2.astype(h2_ref.dtype)


def _outproj(x, o_dn, o_sw, ms, norm_g, w_out, tt):
    t = x.shape[0]
    return pl.pallas_call(
        functools.partial(_outproj_kernel, tiles_per_group=ms.tiles_per_group),
        grid=(t // tt,),
        in_specs=[
            pl.BlockSpec((tt, D_MODEL), lambda i: (i, 0)),
            pl.BlockSpec((tt, DN_V_W), lambda i: (i, 0)),
            pl.BlockSpec((tt, SW_Q_W), lambda i: (i, 0)),
            _mod_spec(2, ms, tt),
            _mod_spec(3, ms, tt),
            _mod_spec(4, ms, tt),
            pl.BlockSpec((1, D_MODEL), lambda i: (0, 0)),
            pl.BlockSpec((None, D_MODEL, D_MODEL), lambda i: (ms.layer, 0, 0)),
        ],
        out_specs=[pl.BlockSpec((tt, D_MODEL), lambda i: (i, 0))] * 2,
        out_shape=[jax.ShapeDtypeStruct((t, D_MODEL), F32), jax.ShapeDtypeStruct((t, D_MODEL), BF16)],
        compiler_params=_cparams(("arbitrary",)),
        name="outproj",
    )(x, o_dn, o_sw, ms.arr, ms.arr, ms.arr, norm_g.reshape(1, D_MODEL), w_out)


def _top_values(work, n, with_rank=False):
    out = []
    rank = jnp.full(work.shape, float(n), F32) if with_rank else None
    for it in range(n):
        m = jnp.max(work, axis=0, keepdims=True)
        out.append(m)
        if it + 1 < n or with_rank:
            hit = work == m
            if with_rank:
                rank = jnp.where(hit, float(it), rank)
            work = jnp.where(hit, NEG_INF, work)
    return (out, rank) if with_rank else out


def _peer_kernel(h2_ref, x_ref, g2_ref, wq_ref, keys_ref, u_ref, vt_ref, fg_ref, o_ref,
                 s_s, cnt_s, e1_s, rk_s, e2_s, tv_s, acc, hb_s, *, tt, et, final_norm, tiles_per_group):
    j = pl.program_id(1)
    lg_n = tt // LANES
    n_rr = et // N_KEYS
    bf16_rows = 2 * SUBLANES

    def row_bf16(row):
        one = jnp.broadcast_to(row, (bf16_rows, LANES)).astype(BF16)
        return jnp.concatenate([one] * (N_KEYS // bf16_rows), axis=0)

    @pl.when(j == 0)
    def _():
        acc[...] = jnp.zeros_like(acc)
        tv_s[...] = jnp.full(tv_s.shape, NEG_INF, F32)
        h2 = h2_ref[...]
        hb_s[...] = h2.astype(F32).T.astype(BF16)
        q = _dot(h2, wq_ref[...]).astype(BF16)
        for hp in range(2 * PK_HEADS):
            s_t = _dot_nt(keys_ref[hp], q[:, hp * PK_HALF:(hp + 1) * PK_HALF])
            for lg in range(lg_n):
                s_s[hp, lg] = s_t[:, lg * LANES:(lg + 1) * LANES]

        def head_body(idx, carry):
            h = idx // lg_n
            lg = idx % lg_n
            s1 = s_s[2 * h, lg]
            s2 = s_s[2 * h + 1, lg]
            t1 = _top_values(s1, N_TOP)
            t2, rank2 = _top_values(s2, N_TOP, with_rank=True)
            for it in range(N_TOP):
                tv_s[it:it + 1, :] = t2[it]
            for it in range(TOP_SPLIT, N_TOP):
                tv_s[TOP_T1_ROW + it - TOP_SPLIT:TOP_T1_ROW + it - TOP_SPLIT + 1, :] = t1[it]
            rid = lax.broadcasted_iota(jnp.int32, (SUBLANES, LANES), 0)
            cands = []

            def add_blocks(row, base, count):
                for r0 in range(0, count, SUBLANES):
                    blk = row + tv_s[base + r0:base + r0 + SUBLANES, :]
                    if count - r0 < SUBLANES:
                        blk = jnp.where(rid < count - r0, blk, NEG_INF)
                    cands.append(blk)

            for i in range(TOP_SPLIT):
                add_blocks(t1[i], 0, N_TOP // (i + 1))
            for jx in range(N_TOP // (TOP_SPLIT + 1)):
                add_blocks(t2[jx], TOP_T1_ROW, N_TOP // (jx + 1) - TOP_SPLIT)
            cand = jnp.concatenate(cands, axis=0)
            tops = _top_values(cand, N_TOP)
            thr = 0.5 * (tops[PK_TOPK - 1] + tops[PK_TOPK])
            m1 = t1[0]
            m2 = t2[0]
            zsum = jnp.sum(jnp.where(cand >= thr, jnp.exp(cand - (m1 + m2)), 0.0), axis=0, keepdims=True)
            need = thr - s1
            cnt = jnp.zeros_like(s1)
            for it in range(N_TOP):
                cnt = cnt + jnp.where(t2[it] >= need, 1.0, 0.0)
            cnt_s[h, lg] = cnt
            rk_s[h, lg] = rank2.astype(BF16)
            e1_s[h, lg] = jnp.exp(s1 - m1) * (0.5 / zsum)
            e2_s[h, lg] = jnp.exp(s2 - m2).astype(BF16)
            return carry

        lax.fori_loop(0, PK_HEADS * lg_n, head_body, 0)

    a_t = _dot(u_ref[...], hb_s[...])
    a_b = a_t.astype(BF16)
    act = a_b * (1.0 + lax.erf(a_b * (2.0 ** -0.5)))
    blocks = []
    for rr in range(n_rr):
        r = j * n_rr + rr
        row_blocks = []
        for lg in range(lg_n):
            g = None
            for h in range(PK_HEADS):
                cnt_b = row_bf16(cnt_s[h, lg, pl.ds(r, 1), :])
                e1_b = row_bf16(e1_s[h, lg, pl.ds(r, 1), :])
                term = jnp.where(rk_s[h, lg] < cnt_b, e2_s[h, lg], jnp.zeros((), BF16)) * e1_b
                g = term if g is None else g + term
            row_blocks.append(act[rr * N_KEYS:(rr + 1) * N_KEYS, lg * LANES:(lg + 1) * LANES] * g)
        blocks.append(jnp.concatenate(row_blocks, axis=1) if lg_n > 1 else row_blocks[0])
    p_t = jnp.concatenate(blocks, axis=0) if len(blocks) > 1 else blocks[0]
    acc[...] += _dot(vt_ref[...], p_t)

    @pl.when(j == pl.num_programs(1) - 1)
    def _():
        x = x_ref[...] + _mod_rows(g2_ref, tiles_per_group) * acc[...].T
        if final_norm:
            x = _rms(x, fg_ref[...])
        o_ref[...] = x


def _peer(h2, x, ms, wq, keys, u_tab, vt_tab, final_g, *, tt, et, final_norm):
    t = x.shape[0]
    lg_n = tt // LANES
    n_j = N_EXPERTS // et
    kern = functools.partial(_peer_kernel, tt=tt, et=et, final_norm=final_norm,
                             tiles_per_group=ms.tiles_per_group)
    const = lambda shape: pl.BlockSpec(shape, lambda i, j: (0,) * len(shape), pipeline_mode=pl.Buffered(1))
    layer_const = lambda shape: pl.BlockSpec((None,) + shape, lambda i, j: (ms.layer,) + (0,) * len(shape),
                                             pipeline_mode=pl.Buffered(1))
    return pl.pallas_call(
        kern,
        grid=(t // tt, n_j),
        in_specs=[
            pl.BlockSpec((tt, D_MODEL), lambda i, j: (i, 0)),
            pl.BlockSpec((tt, D_MODEL), lambda i, j: (i, 0)),
            _mod_spec(5, ms, tt),
            layer_const((D_MODEL, PK_HEADS * PK_QDIM)),
            layer_const((2 * PK_HEADS, N_KEYS, PK_HALF)),
            pl.BlockSpec((None, et, D_MODEL), lambda i, j: (ms.layer, j, 0)),
            pl.BlockSpec((None, D_MODEL, et), lambda i, j: (ms.layer, 0, j)),
            const((1, D_MODEL)),
        ],
        out_specs=pl.BlockSpec((tt, D_MODEL), lambda i, j: (i, 0)),
        out_shape=jax.ShapeDtypeStruct((t, D_MODEL), F32),
        scratch_shapes=[
            pltpu.VMEM((2 * PK_HEADS, lg_n, N_KEYS, LANES), F32),
            pltpu.VMEM((PK_HEADS, lg_n, N_KEYS, LANES), F32),
            pltpu.VMEM((PK_HEADS, lg_n, N_KEYS, LANES), F32),
            pltpu.VMEM((PK_HEADS, lg_n, N_KEYS, LANES), BF16),
            pltpu.VMEM((PK_HEADS, lg_n, N_KEYS, LANES), BF16),
            pltpu.VMEM((TOP_ROWS, LANES), F32),
            pltpu.VMEM((D_MODEL, tt), F32),
            pltpu.VMEM((D_MODEL, tt), BF16),
        ],
        compiler_params=_cparams(("arbitrary", "arbitrary")),
        name="peer",
    )(h2, x, ms.arr, wq, keys, u_tab, vt_tab, final_g.reshape(1, D_MODEL))


def _pad_lanes(v):
    return jnp.pad(v, ((0, 0), (0, LANES - v.shape[-1])))


def _prep_w_in(w):
    return jnp.concatenate(
        [w[:, :OFF_B], _pad_lanes(w[:, OFF_B:OFF_A]), _pad_lanes(w[:, OFF_A:OFF_SQ]), w[:, OFF_SQ:]],
        axis=1).astype(BF16)


def _layer(l, x, ms, prior, p, *, seq, tt, final_norm, final_g):
    t = x.shape[0]
    bsz = t // seq
    qkv, z, bcol, acol, sq, skv = _inproj(x, ms, p["norm1_g"][l], p["w_in"], tt)
    r3 = lambda a: a.reshape(bsz, seq, a.shape[-1])
    qkv3 = r3(qkv)
    if prior is None:
        conv_buf = jnp.zeros((bsz, DN_CONV - 1, CONV_DIM), F32)
        s0 = jnp.zeros((bsz, DN_HEADS, DN_DK, DN_DV), F32)
        dn_cfg = dict(nb_blk=1, lb=256, chunk=DN_CHUNK, valid_len=256)
    else:
        s0, conv_buf, k_buf, v_buf, n_new = prior
        dn_cfg = dict(nb_blk=8, lb=DEC_PAD, chunk=DEC_PAD, valid_len=n_new)
        r3 = lambda a: jnp.pad(a.reshape(bsz, seq, a.shape[-1]), ((0, 0), (0, DEC_PAD - seq), (0, 0)))
    o_dn, s_new = _deltanet(r3(qkv), r3(z), r3(bcol), r3(acol), p["conv_w"][l], p["alog"][l], p["dtb"][l],
                            p["dn_norm_g"][l], conv_buf, s0, **dn_cfg)
    if prior is None:
        o_sw = _swa_prompt(r3(sq), r3(skv), p["sw_sinks"][l])
        new_conv = qkv3[:, seq - (DN_CONV - 1):, :]
        n_buf = WINDOW
        new_k = r3(skv)[:, seq - n_buf:, :SW_KV_W].reshape(bsz, n_buf, SW_KVHEADS, SW_HD)
        new_v = r3(skv)[:, seq - n_buf:, SW_KV_W:].reshape(bsz, n_buf, SW_KVHEADS, SW_HD)
    else:
        p_len = k_buf.shape[1]
        o_sw, new_k, new_v = _swa_decode(r3(sq), r3(skv), k_buf.reshape(bsz, p_len, SW_KV_W),
                                         v_buf.reshape(bsz, p_len, SW_KV_W), p["sw_sinks"][l],
                                         nb_blk=16, n_new=n_new)
        new_conv = jnp.concatenate([conv_buf, qkv3], axis=1)[:, seq:]
        new_k = new_k.reshape(bsz, p_len, SW_KVHEADS, SW_HD)
        new_v = new_v.reshape(bsz, p_len, SW_KVHEADS, SW_HD)
        o_dn, o_sw = o_dn[:, :seq], o_sw[:, :seq]
    x, h2 = _outproj(x, o_dn.reshape(t, DN_V_W), o_sw.reshape(t, SW_Q_W), ms,
                     p["norm2_g"][l], p["w_out"], tt)
    x = _peer(h2, x, ms, p["wq"], p["keys"], p["u"], p["vt"], final_g,
              tt=tt, et=512, final_norm=final_norm)
    return x, (s_new, new_conv, new_k, new_v)


def _trunk(x, mod, row0, tiles_per_group, prior_stack, p, final_g, *, seq, tt):
    new = []
    for l in range(DEPTH):
        prior = None if prior_stack is None else tuple(s[l] for s in prior_stack[:4]) + (prior_stack[4],)
        x, st = _layer(l, x, ModSrc(mod, l, row0, tiles_per_group), prior, p, seq=seq, tt=tt,
                       final_norm=(l == DEPTH - 1), final_g=final_g)
        new.append(st)
    return x, [jnp.stack(zz) for zz in zip(*new)]


def kernel(x_prompt, x_sample, state_delta, state_conv, cache_swa_k, cache_swa_v, c_prompt, c_sample,
           norm1_g, norm2_g, final_norm_g, w_ada, b_ada, w_in, conv_w, dn_a_log, dn_dt_bias, dn_norm_g,
           sw_sinks, w_out, peer_wq, peer_keys, peer_u, peer_v):
    batch, seq, _ = x_prompt.shape
    dec_b, dec_l, _ = x_sample.shape
    tt = 512

    p = dict(
        norm1_g=norm1_g, norm2_g=norm2_g,
        w_in=jax.vmap(_prep_w_in)(w_in),
        conv_w=conv_w,
        alog=_pad_lanes(dn_a_log).reshape(DEPTH, 1, LANES),
        dtb=_pad_lanes(dn_dt_bias).reshape(DEPTH, 1, LANES),
        dn_norm_g=dn_norm_g.reshape(DEPTH, 1, DN_DV),
        sw_sinks=sw_sinks,
        w_out=w_out.astype(BF16),
        wq=peer_wq.astype(BF16),
        keys=peer_keys.reshape(DEPTH, 2 * PK_HEADS, N_KEYS, PK_HALF).astype(BF16),
        u=peer_u.astype(BF16),
        vt=jnp.swapaxes(peer_v, 1, 2).astype(BF16),
    )

    n_dec = dec_b * dec_l
    assert batch <= SUBLANES and n_dec % tt == 0 and DN_CONV - 1 <= dec_l <= DEC_PAD
    c_dec = jnp.repeat(c_sample, dec_l, axis=0)
    pad_rows = (-(n_dec + batch)) % SUBLANES
    c_all = jnp.concatenate([c_dec, c_prompt, jnp.zeros((pad_rows, D_MODEL), F32)], axis=0)
    mod = _ada_mod(c_all, w_ada, b_ada)

    xp = x_prompt.reshape(batch * seq, D_MODEL)
    yp, (delta_p, conv_p, k_p, v_p) = _trunk(xp, mod, n_dec, seq // tt, None, p, final_norm_g, seq=seq, tt=tt)

    xs = x_sample.reshape(n_dec, D_MODEL)
    prior = (state_delta, state_conv, cache_swa_k, cache_swa_v, dec_l)
    ys, (delta_s, conv_s, k_s, v_s) = _trunk(xs, mod, 0, None, prior, p, final_norm_g, seq=dec_l, tt=tt)

    y_prompt = yp.reshape(batch, seq, D_MODEL)
    y_sample = ys.reshape(dec_b, dec_l, D_MODEL)
    return (y_prompt, y_sample, delta_p, conv_p, k_p, v_p, delta_s, conv_s, k_s, v_s)
```

```python
import functools
from typing import NamedTuple

import jax
import jax.numpy as jnp
from jax import lax
from jax.experimental import pallas as pl
from jax.experimental.pallas import tpu as pltpu

F32 = jnp.float32
BF16 = jnp.bfloat16

D_MODEL = 1024
DEPTH = 4
DN_HEADS = 4
DN_DK = 128
DN_DV = 128
DN_CONV = 4
DN_CHUNK = 64
SW_QHEADS = 8
SW_KVHEADS = 2
SW_HD = 64
SW_GROUP = SW_QHEADS // SW_KVHEADS
WINDOW = 128
DN_QK_W = DN_HEADS * DN_DK
DN_V_W = DN_HEADS * DN_DV
CONV_DIM = 2 * DN_QK_W + DN_V_W
SW_Q_W = SW_QHEADS * SW_HD
SW_KV_W = SW_KVHEADS * SW_HD
OFF_Z = CONV_DIM
OFF_B = OFF_Z + DN_V_W
OFF_A = OFF_B + DN_HEADS
OFF_SQ = OFF_A + DN_HEADS
PK_HEADS = 8
N_KEYS = 128
N_EXPERTS = N_KEYS * N_KEYS
PK_QDIM = 256
PK_HALF = PK_QDIM // 2
PK_TOPK = 16
EPS = 1e-6

LANES = 128
SUBLANES = 8
VMEM_LIMIT = 56 * 1024 * 1024
PROJ_W = CONV_DIM + DN_V_W + 2 * LANES + SW_Q_W + 2 * SW_KV_W
DEC_PAD = SUBLANES
N_TOP = PK_TOPK + 1
TOP_SPLIT = 4
TOP_T1_ROW = 3 * SUBLANES
TOP_ROWS = TOP_T1_ROW + 2 * SUBLANES
DN_INV_BLOCK = SUBLANES
NEG_INF = float("-inf")


def _cparams(sem):
    return pltpu.CompilerParams(dimension_semantics=sem, vmem_limit_bytes=VMEM_LIMIT)


def _silu(x):
    return x * jax.nn.sigmoid(x)


def _softplus(x):
    return jnp.maximum(x, 0.0) + jnp.log1p(jnp.exp(-jnp.abs(x)))


def _rms(x, g):
    return x * lax.rsqrt(jnp.mean(x * x, axis=-1, keepdims=True) + EPS) * g


def _dot(a, b):
    return jnp.dot(a, b, preferred_element_type=F32)


def _dot_nt(a, b):
    return lax.dot_general(a, b, (((1,), (1,)), ((), ())), preferred_element_type=F32)


def _dot_tn(a, b):
    return lax.dot_general(a, b, (((0,), (0,)), ((), ())), preferred_element_type=F32)


def _split_bf16(v):
    hi = v.astype(BF16)
    lo = (v - hi.astype(F32)).astype(BF16)
    return hi, lo


def _dot3(a, b, dot=_dot):
    a_hi, a_lo = _split_bf16(a)
    b_hi, b_lo = _split_bf16(b)
    return dot(a_hi, b_hi) + (dot(a_hi, b_lo) + dot(a_lo, b_hi))


def _ada_kernel(c_ref, w_ref, b_ref, o_ref):
    a = _silu(c_ref[...]).astype(BF16)
    o_ref[0] = _dot(a, w_ref[0].astype(BF16)) + b_ref[0]


def _ada_mod(c_all, w_ada, b_ada):
    rows = c_all.shape[0]
    tn = 1536
    return pl.pallas_call(
        _ada_kernel,
        grid=(DEPTH, 6 * D_MODEL // tn),
        in_specs=[
            pl.BlockSpec((rows, D_MODEL), lambda l, j: (0, 0)),
            pl.BlockSpec((1, D_MODEL, tn), lambda l, j: (l, 0, j)),
            pl.BlockSpec((1, 1, tn), lambda l, j: (l, 0, j)),
        ],
        out_specs=pl.BlockSpec((1, rows, tn), lambda l, j: (l, 0, j)),
        out_shape=jax.ShapeDtypeStruct((DEPTH, rows, 6 * D_MODEL), F32),
        compiler_params=_cparams(("arbitrary", "arbitrary")),
        name="ada_mod",
    )(c_all, w_ada, b_ada.reshape(DEPTH, 1, 6 * D_MODEL))


class ModSrc(NamedTuple):
    arr: jax.Array
    layer: int
    row0: int
    tiles_per_group: int | None


def _mod_spec(k, ms, tt):
    if ms.tiles_per_group is None:
        return pl.BlockSpec((1, tt, D_MODEL), lambda i, *_: (ms.layer, ms.row0 // tt + i, k))
    return pl.BlockSpec((1, SUBLANES, D_MODEL), lambda i, *_: (ms.layer, ms.row0 // SUBLANES, k))


def _mod_rows(ref, tiles_per_group):
    if tiles_per_group is None:
        return ref[0]
    return ref[0, pl.ds(pl.program_id(0) // tiles_per_group, 1), :]


def _inproj_kernel(x_ref, sh_ref, sc_ref, g_ref, w_ref,
                   qkv_ref, z_ref, b_ref, a_ref, sq_ref, skv_ref, *, tiles_per_group):
    sc = _mod_rows(sc_ref, tiles_per_group)
    sh = _mod_rows(sh_ref, tiles_per_group)
    h = _rms(x_ref[...], g_ref[...]) * (1.0 + sc) + sh
    p = _dot(h.astype(BF16), w_ref[...])
    o = 0
    for ref in (qkv_ref, z_ref, b_ref, a_ref, sq_ref, skv_ref):
        w = ref.shape[-1]
        ref[...] = p[:, o:o + w]
        o += w


def _inproj(x, ms, norm_g, w_r, tt):
    t = x.shape[0]
    widths = (CONV_DIM, DN_V_W, LANES, LANES, SW_Q_W, 2 * SW_KV_W)
    return pl.pallas_call(
        functools.partial(_inproj_kernel, tiles_per_group=ms.tiles_per_group),
        grid=(t // tt,),
        in_specs=[
            pl.BlockSpec((tt, D_MODEL), lambda i: (i, 0)),
            _mod_spec(0, ms, tt),
            _mod_spec(1, ms, tt),
            pl.BlockSpec((1, D_MODEL), lambda i: (0, 0)),
            pl.BlockSpec((None, D_MODEL, PROJ_W), lambda i: (ms.layer, 0, 0)),
        ],
        out_specs=[pl.BlockSpec((tt, w), lambda i: (i, 0)) for w in widths],
        out_shape=[jax.ShapeDtypeStruct((t, w), F32) for w in widths],
        compiler_params=_cparams(("arbitrary",)),
        name="inproj",
    )(x, ms.arr, ms.arr, norm_g.reshape(1, D_MODEL), w_r)


def _dn_kernel(u_ref, z_ref, b_ref, a_ref, cw_ref, alog_ref, dtb_ref, ng_ref, cbuf_ref, s0_ref,
               o_ref, sfin_ref, ubuf, qkv, beta_s, gc_s, gct_s, state,
               *, nb_blk, lb, chunk, valid_len):
    l = pl.program_id(1)
    nc = lb // chunk
    n_scan = chunk.bit_length() - 1

    @pl.when(l == 0)
    def _():
        state[...] = s0_ref[...]
        ubuf[:, SUBLANES - (DN_CONV - 1):SUBLANES, :] = cbuf_ref[...]

    ubuf[:, SUBLANES:SUBLANES + lb, :] = u_ref[...]
    y = None
    for i in range(DN_CONV):
        start = SUBLANES - (DN_CONV - 1) + i
        term = ubuf[:, start:start + lb, :] * cw_ref[i:i + 1, :]
        y = term if y is None else y + term
    qkv[...] = _silu(y)
    ubuf[:, SUBLANES - (DN_CONV - 1):SUBLANES, :] = ubuf[:, lb + SUBLANES - (DN_CONV - 1):lb + SUBLANES, :]

    row = lax.broadcasted_iota(jnp.int32, (lb, LANES), 0)
    for nb in range(nb_blk):
        beta = jax.nn.sigmoid(b_ref[nb])
        g = -jnp.exp(alog_ref[...]) * _softplus(a_ref[nb] + dtb_ref[...])
        if valid_len < lb:
            beta = jnp.where(row < valid_len, beta, 0.0)
            g = jnp.where(row < valid_len, g, 0.0)
        gc = g
        for s in range(n_scan):
            sh = 1 << s
            gc = gc + jnp.where((row % chunk) >= sh, pltpu.roll(gc, sh, 0), 0.0)
        beta_s[nb] = beta
        gc_s[nb] = gc
        for r0 in range(0, lb, LANES):
            nrow = min(LANES, lb - r0)
            blk = gc[r0:r0 + nrow]
            if nrow < LANES:
                blk = jnp.concatenate([blk, jnp.zeros((LANES - nrow, LANES), F32)], axis=0)
            blk_t = blk.T
            for c0 in range(0, nrow, chunk):
                gct_s[nb, (r0 + c0) // chunk] = blk_t[:, c0:c0 + chunk]

    ii = lax.broadcasted_iota(jnp.int32, (chunk, chunk), 0)
    jj = lax.broadcasted_iota(jnp.int32, (chunk, chunk), 1)
    causal = ii >= jj
    strict = ii > jj
    eye = jnp.where(ii == jj, 1.0, 0.0)

    items = [(nb, c, h) for nb in range(nb_blk) for c in range(nc) for h in range(DN_HEADS)]

    def rows_of(c):
        return slice(c * chunk, (c + 1) * chunk)

    def l2n(x):
        return x * lax.rsqrt(jnp.sum(x * x, axis=-1, keepdims=True) + EPS)

    qs = [l2n(qkv[nb, rows_of(c), h * DN_DK:(h + 1) * DN_DK]) * (DN_DK ** -0.5) for nb, c, h in items]
    ks = [l2n(qkv[nb, rows_of(c), DN_QK_W + h * DN_DK:DN_QK_W + (h + 1) * DN_DK]) for nb, c, h in items]
    vs = [qkv[nb, rows_of(c), 2 * DN_QK_W + h * DN_DV:2 * DN_QK_W + (h + 1) * DN_DV] for nb, c, h in items]
    cols = [gc_s[nb, rows_of(c), h:h + 1] for nb, c, h in items]
    rws = [gct_s[nb, c, h:h + 1, :] for nb, c, h in items]
    bcols = [beta_s[nb, rows_of(c), h:h + 1] for nb, c, h in items]
    lasts = [gc_s[nb, (c + 1) * chunk - 1:(c + 1) * chunk, h:h + 1] for nb, c, h in items]
    decays = [jnp.where(causal, jnp.exp(jnp.where(causal, col - rw, 0.0)), 0.0) for col, rw in zip(cols, rws)]
    kks = [_dot3(k, k, _dot_nt) for k in ks]
    a_mats = [jnp.where(strict, bcol * kk * decay, 0.0) for bcol, kk, decay in zip(bcols, kks, decays)]
    base = min(DN_INV_BLOCK, chunk)
    same = lambda b: (ii // b) == (jj // b)
    p_mats = [jnp.where(same(base), -a, 0.0) for a in a_mats]
    t_mats = [eye + n_mat for n_mat in p_mats]
    for _ in range(base.bit_length() - 2):
        p_mats = [_dot3(p_mat, p_mat) for p_mat in p_mats]
        t_mats = [t_mat + _dot3(t_mat, p_mat) for t_mat, p_mat in zip(t_mats, p_mats)]
    b = base
    while b < chunk:
        offs = [jnp.where(same(2 * b) & jnp.logical_not(same(b)), a, 0.0) for a in a_mats]
        t_mats = [t_mat - _dot3(_dot3(t_mat, off), t_mat) for t_mat, off in zip(t_mats, offs)]
        b *= 2
    e_cols = [jnp.exp(col) for col in cols]
    sols = [_dot3(t_mat, jnp.concatenate([bcol * v, (bcol * e_col) * k], axis=-1))
            for t_mat, bcol, v, e_col, k in zip(t_mats, bcols, vs, e_cols, ks)]
    qks = [_dot_nt(q, k) * decay for q, k, decay in zip(qs, ks, decays)]
    wqs = [jnp.concatenate([sol[:, DN_DV:], q * e_col], axis=0) for sol, q, e_col in zip(sols, qs, e_cols)]
    k_decs = [k * jnp.exp(last - col) for k, last, col in zip(ks, lasts, cols)]

    for idx, (nb, c, h) in enumerate(items):
        s_old = state[nb, h]
        ws = _dot(wqs[idx], s_old)
        uu = sols[idx][:, :DN_DV] - ws[:chunk]
        o = ws[chunk:] + _dot(qks[idx], uu)
        state[nb, h] = jnp.exp(lasts[idx]) * s_old + _dot_tn(k_decs[idx], uu)
        o = _rms(o, ng_ref[...])
        zz = z_ref[nb, rows_of(c), h * DN_DV:(h + 1) * DN_DV]
        o_ref[nb, rows_of(c), h * DN_DV:(h + 1) * DN_DV] = (o * _silu(zz)).astype(o_ref.dtype)

    @pl.when(l == pl.num_programs(1) - 1)
    def _():
        sfin_ref[...] = state[...]


def _deltanet(u, z, b, a, conv_w, alog, dtb, norm_g, conv_buf, s0, *, nb_blk, lb, chunk, valid_len):
    bsz, seq, _ = u.shape
    nc = lb // chunk
    kern = functools.partial(_dn_kernel, nb_blk=nb_blk, lb=lb, chunk=chunk, valid_len=valid_len)
    tok = lambda w: pl.BlockSpec((nb_blk, lb, w), lambda i, l: (i, l, 0))
    full2 = lambda r, w: pl.BlockSpec((r, w), lambda i, l: (0, 0))
    return pl.pallas_call(
        kern,
        grid=(bsz // nb_blk, seq // lb),
        in_specs=[
            tok(CONV_DIM), tok(DN_V_W), tok(LANES), tok(LANES),
            full2(DN_CONV, CONV_DIM), full2(1, LANES), full2(1, LANES), full2(1, DN_DV),
            pl.BlockSpec((nb_blk, DN_CONV - 1, CONV_DIM), lambda i, l: (i, 0, 0)),
            pl.BlockSpec((nb_blk, DN_HEADS, DN_DK, DN_DV), lambda i, l: (i, 0, 0, 0)),
        ],
        out_specs=[
            tok(DN_V_W),
            pl.BlockSpec((nb_blk, DN_HEADS, DN_DK, DN_DV), lambda i, l: (i, 0, 0, 0)),
        ],
        out_shape=[
            jax.ShapeDtypeStruct((bsz, seq, DN_V_W), F32),
            jax.ShapeDtypeStruct((bsz, DN_HEADS, DN_DK, DN_DV), F32),
        ],
        scratch_shapes=[
            pltpu.VMEM((nb_blk, lb + SUBLANES, CONV_DIM), F32),
            pltpu.VMEM((nb_blk, lb, CONV_DIM), F32),
            pltpu.VMEM((nb_blk, lb, LANES), F32),
            pltpu.VMEM((nb_blk, lb, LANES), F32),
            pltpu.VMEM((nb_blk, nc, LANES, chunk), F32),
            pltpu.VMEM((nb_blk, DN_HEADS, DN_DK, DN_DV), F32),
        ],
        compiler_params=_cparams(("arbitrary", "arbitrary")),
        name="deltanet",
    )(u, z, b, a, conv_w, alog, dtb, norm_g, conv_buf, s0)


def _swa_prompt_kernel(sink_ref, q_ref, kvp_ref, kvc_ref, o_ref):
    i = pl.program_id(1)
    q = q_ref[0]
    kvp = kvp_ref[0]
    kvc = kvc_ref[0]
    kcat = jnp.concatenate([kvp[:, :SW_KV_W], kvc[:, :SW_KV_W]], axis=0).astype(BF16)
    vcat = jnp.concatenate([kvp[:, SW_KV_W:], kvc[:, SW_KV_W:]], axis=0).astype(BF16)
    row = lax.broadcasted_iota(jnp.int32, (WINDOW, 2 * WINDOW), 0)
    col = lax.broadcasted_iota(jnp.int32, (WINDOW, 2 * WINDOW), 1)
    valid = (col > row) & (col <= row + WINDOW) & ((col >= WINDOW) | (i > 0))
    for qh in range(SW_QHEADS):
        kv = qh // SW_GROUP
        qs = q[:, qh * SW_HD:(qh + 1) * SW_HD].astype(BF16)
        ks = kcat[:, kv * SW_HD:(kv + 1) * SW_HD]
        vs = vcat[:, kv * SW_HD:(kv + 1) * SW_HD]
        s = _dot_nt(qs, ks) * (SW_HD ** -0.5)
        s = jnp.where(valid, s, NEG_INF)
        sink = sink_ref[qh]
        m = jnp.maximum(jnp.max(s, axis=-1, keepdims=True), sink)
        p = jnp.exp(s - m)
        den = jnp.sum(p, axis=-1, keepdims=True) + jnp.exp(sink - m)
        o = _dot(p.astype(BF16), vs) / den
        o_ref[0, :, qh * SW_HD:(qh + 1) * SW_HD] = o.astype(o_ref.dtype)


def _swa_prompt(sq, skv, sinks):
    bsz, seq, _ = sq.shape
    nblk = seq // WINDOW
    return pl.pallas_call(
        _swa_prompt_kernel,
        grid=(bsz, nblk),
        in_specs=[
            pl.BlockSpec(memory_space=pltpu.SMEM),
            pl.BlockSpec((1, WINDOW, SW_Q_W), lambda b, i: (b, i, 0)),
            pl.BlockSpec((1, WINDOW, 2 * SW_KV_W), lambda b, i: (b, jnp.maximum(i - 1, 0), 0)),
            pl.BlockSpec((1, WINDOW, 2 * SW_KV_W), lambda b, i: (b, i, 0)),
        ],
        out_specs=pl.BlockSpec((1, WINDOW, SW_Q_W), lambda b, i: (b, i, 0)),
        out_shape=jax.ShapeDtypeStruct((bsz, seq, SW_Q_W), F32),
        compiler_params=_cparams(("arbitrary", "arbitrary")),
        name="swa_prompt",
    )(sinks, sq, skv, skv)


def _swa_decode_kernel(sink_ref, q_ref, kvn_ref, kb_ref, vb_ref, o_ref, ko_ref, vo_ref, *, n_new):
    q = q_ref[...]
    kvn = kvn_ref[...]
    kb = kb_ref[...]
    vb = vb_ref[...]
    nb = q.shape[0]
    p_len = kb.shape[1]
    kn = kvn[:, :, :SW_KV_W]
    vn = kvn[:, :, SW_KV_W:]
    ko_ref[:, :p_len - n_new, :] = kb[:, n_new:, :]
    ko_ref[:, p_len - n_new:, :] = kn[:, :n_new, :]
    vo_ref[:, :p_len - n_new, :] = vb[:, n_new:, :]
    vo_ref[:, p_len - n_new:, :] = vn[:, :n_new, :]
    qi = lax.broadcasted_iota(jnp.int32, (nb, DEC_PAD, p_len), 1)
    kj = lax.broadcasted_iota(jnp.int32, (nb, DEC_PAD, p_len), 2)
    valid_buf = (p_len + qi - kj) < WINDOW
    qi2 = lax.broadcasted_iota(jnp.int32, (nb, DEC_PAD, DEC_PAD), 1)
    kj2 = lax.broadcasted_iota(jnp.int32, (nb, DEC_PAD, DEC_PAD), 2)
    valid_new = (kj2 <= qi2) & (kj2 < n_new)
    for qh in range(SW_QHEADS):
        kv = qh // SW_GROUP
        hs = slice(kv * SW_HD, (kv + 1) * SW_HD)
        qs = q[:, :, qh * SW_HD:(qh + 1) * SW_HD]
        sb = jnp.einsum("bqd,bkd->bqk", qs, kb[:, :, hs], preferred_element_type=F32) * (SW_HD ** -0.5)
        sn = jnp.einsum("bqd,bkd->bqk", qs, kn[:, :, hs], preferred_element_type=F32) * (SW_HD ** -0.5)
        sb = jnp.where(valid_buf, sb, NEG_INF)
        sn = jnp.where(valid_new, sn, NEG_INF)
        sink = sink_ref[qh]
        m = jnp.maximum(jnp.maximum(jnp.max(sb, axis=-1, keepdims=True),
                                    jnp.max(sn, axis=-1, keepdims=True)), sink)
        pb = jnp.exp(sb - m)
        pn = jnp.exp(sn - m)
        den = jnp.sum(pb, axis=-1, keepdims=True) + jnp.sum(pn, axis=-1, keepdims=True) + jnp.exp(sink - m)
        o = (jnp.einsum("bqk,bkd->bqd", pb, vb[:, :, hs], preferred_element_type=F32)
             + jnp.einsum("bqk,bkd->bqd", pn, vn[:, :, hs], preferred_element_type=F32))
        o_ref[:, :, qh * SW_HD:(qh + 1) * SW_HD] = (o / den).astype(o_ref.dtype)


def _swa_decode(sq, skv, k_buf, v_buf, sinks, *, nb_blk, n_new):
    bsz = sq.shape[0]
    p_len = k_buf.shape[1]
    tok = lambda w: pl.BlockSpec((nb_blk, DEC_PAD, w), lambda i: (i, 0, 0))
    cache = pl.BlockSpec((nb_blk, p_len, SW_KV_W), lambda i: (i, 0, 0))
    return pl.pallas_call(
        functools.partial(_swa_decode_kernel, n_new=n_new),
        grid=(bsz // nb_blk,),
        in_specs=[pl.BlockSpec(memory_space=pltpu.SMEM), tok(SW_Q_W), tok(2 * SW_KV_W), cache, cache],
        out_specs=[tok(SW_Q_W), cache, cache],
        out_shape=[
            jax.ShapeDtypeStruct((bsz, DEC_PAD, SW_Q_W), F32),
            jax.ShapeDtypeStruct((bsz, p_len, SW_KV_W), F32),
            jax.ShapeDtypeStruct((bsz, p_len, SW_KV_W), F32),
        ],
        compiler_params=_cparams(("arbitrary",)),
        name="swa_decode",
    )(sinks, sq, skv, k_buf, v_buf)


def _outproj_kernel(x_ref, odn_ref, osw_ref, g1_ref, sh2_ref, sc2_ref, n2_ref, w_ref, xo_ref, h2_ref,
                    *, tiles_per_group):
    mix = (_dot(odn_ref[...].astype(BF16), w_ref[:DN_V_W, :])
           + _dot(osw_ref[...].astype(BF16), w_ref[DN_V_W:, :]))
    x = x_ref[...] + _mod_rows(g1_ref, tiles_per_group) * mix
    xo_ref[...] = x
    h2 = (_rms(x, n2_ref[...]) * (1.0 + _mod_rows(sc2_ref, tiles_per_group))
          + _mod_rows(sh2_ref, tiles_per_group))
    h2_ref[...] = h2.astype(h2_ref.dtype)


def _outproj(x, o_dn, o_sw, ms, norm_g, w_out, tt):
    t = x.shape[0]
    return pl.pallas_call(
        functools.partial(_outproj_kernel, tiles_per_group=ms.tiles_per_group),
        grid=(t // tt,),
        in_specs=[
            pl.BlockSpec((tt, D_MODEL), lambda i: (i, 0)),
            pl.BlockSpec((tt, DN_V_W), lambda i: (i, 0)),
            pl.BlockSpec((tt, SW_Q_W), lambda i: (i, 0)),
            _mod_spec(2, ms, tt),
            _mod_spec(3, ms, tt),
            _mod_spec(4, ms, tt),
            pl.BlockSpec((1, D_MODEL), lambda i: (0, 0)),
            pl.BlockSpec((None, D_MODEL, D_MODEL), lambda i: (ms.layer, 0, 0)),
        ],
        out_specs=[pl.BlockSpec((tt, D_MODEL), lambda i: (i, 0))] * 2,
        out_shape=[jax.ShapeDtypeStruct((t, D_MODEL), F32), jax.ShapeDtypeStruct((t, D_MODEL), BF16)],
        compiler_params=_cparams(("arbitrary",)),
        name="outproj",
    )(x, o_dn, o_sw, ms.arr, ms.arr, ms.arr, norm_g.reshape(1, D_MODEL), w_out)


def _top_values(work, n, with_rank=False):
    out = []
    rank = jnp.full(work.shape, float(n), F32) if with_rank else None
    for it in range(n):
        m = jnp.max(work, axis=0, keepdims=True)
        out.append(m)
        if it + 1 < n or with_rank:
            hit = work == m
            if with_rank:
                rank = jnp.where(hit, float(it), rank)
            work = jnp.where(hit, NEG_INF, work)
    return (out, rank) if with_rank else out


def _peer_kernel(h2_ref, x_ref, g2_ref, wq_ref, keys_ref, u_ref, vt_ref, fg_ref, o_ref,
                 s_s, cnt_s, e1_s, rk_s, e2_s, tv_s, acc, hb_s, *, tt, et, final_norm, tiles_per_group):
    j = pl.program_id(1)
    lg_n = tt // LANES
    n_rr = et // N_KEYS
    bf16_rows = 2 * SUBLANES

    def row_bf16(row):
        one = jnp.broadcast_to(row, (bf16_rows, LANES)).astype(BF16)
        return jnp.concatenate([one] * (N_KEYS // bf16_rows), axis=0)

    @pl.when(j == 0)
    def _():
        acc[...] = jnp.zeros_like(acc)
        tv_s[...] = jnp.full(tv_s.shape, NEG_INF, F32)
        h2 = h2_ref[...]
        hb_s[...] = h2.astype(F32).T.astype(BF16)
        q = _dot(h2, wq_ref[...]).astype(BF16)
        for hp in range(2 * PK_HEADS):
            s_t = _dot_nt(keys_ref[hp], q[:, hp * PK_HALF:(hp + 1) * PK_HALF])
            for lg in range(lg_n):
                s_s[hp, lg] = s_t[:, lg * LANES:(lg + 1) * LANES]

        def head_body(idx, carry):
            h = idx // lg_n
            lg = idx % lg_n
            s1 = s_s[2 * h, lg]
            s2 = s_s[2 * h + 1, lg]
            t1 = _top_values(s1, N_TOP)
            t2, rank2 = _top_values(s2, N_TOP, with_rank=True)
            for it in range(N_TOP):
                tv_s[it:it + 1, :] = t2[it]
            for it in range(TOP_SPLIT, N_TOP):
                tv_s[TOP_T1_ROW + it - TOP_SPLIT:TOP_T1_ROW + it - TOP_SPLIT + 1, :] = t1[it]
            rid = lax.broadcasted_iota(jnp.int32, (SUBLANES, LANES), 0)
            cands = []

            def add_blocks(row, base, count):
                for r0 in range(0, count, SUBLANES):
                    blk = row + tv_s[base + r0:base + r0 + SUBLANES, :]
                    if count - r0 < SUBLANES:
                        blk = jnp.where(rid < count - r0, blk, NEG_INF)
                    cands.append(blk)

            for i in range(TOP_SPLIT):
                add_blocks(t1[i], 0, N_TOP // (i + 1))
            for jx in range(N_TOP // (TOP_SPLIT + 1)):
                add_blocks(t2[jx], TOP_T1_ROW, N_TOP // (jx + 1) - TOP_SPLIT)
            cand = jnp.concatenate(cands, axis=0)
            tops = _top_values(cand, N_TOP)
            thr = 0.5 * (tops[PK_TOPK - 1] + tops[PK_TOPK])
            m1 = t1[0]
            m2 = t2[0]
            zsum = jnp.sum(jnp.where(cand >= thr, jnp.exp(cand - (m1 + m2)), 0.0), axis=0, keepdims=True)
            need = thr - s1
            cnt = jnp.zeros_like(s1)
            for it in range(N_TOP):
                cnt = cnt + jnp.where(t2[it] >= need, 1.0, 0.0)
            cnt_s[h, lg] = cnt
            rk_s[h, lg] = rank2.astype(BF16)
            e1_s[h, lg] = jnp.exp(s1 - m1) * (0.5 / zsum)
            e2_s[h, lg] = jnp.exp(s2 - m2).astype(BF16)
            return carry

        lax.fori_loop(0, PK_HEADS * lg_n, head_body, 0)

    a_t = _dot(u_ref[...], hb_s[...])
    a_b = a_t.astype(BF16)
    act = a_b * (1.0 + lax.erf(a_b * (2.0 ** -0.5)))
    blocks = []
    for rr in range(n_rr):
        r = j * n_rr + rr
        row_blocks = []
        for lg in range(lg_n):
            g = None
            for h in range(PK_HEADS):
                cnt_b = row_bf16(cnt_s[h, lg, pl.ds(r, 1), :])
                e1_b = row_bf16(e1_s[h, lg, pl.ds(r, 1), :])
                term = jnp.where(rk_s[h, lg] < cnt_b, e2_s[h, lg], jnp.zeros((), BF16)) * e1_b
                g = term if g is None else g + term
            row_blocks.append(act[rr * N_KEYS:(rr + 1) * N_KEYS, lg * LANES:(lg + 1) * LANES] * g)
        blocks.append(jnp.concatenate(row_blocks, axis=1) if lg_n > 1 else row_blocks[0])
    p_t = jnp.concatenate(blocks, axis=0) if len(blocks) > 1 else blocks[0]
    acc[...] += _dot(vt_ref[...], p_t)

    @pl.when(j == pl.num_programs(1) - 1)
    def _():
        x = x_ref[...] + _mod_rows(g2_ref, tiles_per_group) * acc[...].T
        if final_norm:
            x = _rms(x, fg_ref[...])
        o_ref[...] = x


def _peer(h2, x, ms, wq, keys, u_tab, vt_tab, final_g, *, tt, et, final_norm):
    t = x.shape[0]
    lg_n = tt // LANES
    n_j = N_EXPERTS // et
    kern = functools.partial(_peer_kernel, tt=tt, et=et, final_norm=final_norm,
                             tiles_per_group=ms.tiles_per_group)
    const = lambda shape: pl.BlockSpec(shape, lambda i, j: (0,) * len(shape), pipeline_mode=pl.Buffered(1))
    layer_const = lambda shape: pl.BlockSpec((None,) + shape, lambda i, j: (ms.layer,) + (0,) * len(shape),
                                             pipeline_mode=pl.Buffered(1))
    return pl.pallas_call(
        kern,
        grid=(t // tt, n_j),
        in_specs=[
            pl.BlockSpec((tt, D_MODEL), lambda i, j: (i, 0)),
            pl.BlockSpec((tt, D_MODEL), lambda i, j: (i, 0)),
            _mod_spec(5, ms, tt),
            layer_const((D_MODEL, PK_HEADS * PK_QDIM)),
            layer_const((2 * PK_HEADS, N_KEYS, PK_HALF)),
            pl.BlockSpec((None, et, D_MODEL), lambda i, j: (ms.layer, j, 0)),
            pl.BlockSpec((None, D_MODEL, et), lambda i, j: (ms.layer, 0, j)),
            const((1, D_MODEL)),
        ],
        out_specs=pl.BlockSpec((tt, D_MODEL), lambda i, j: (i, 0)),
        out_shape=jax.ShapeDtypeStruct((t, D_MODEL), F32),
        scratch_shapes=[
            pltpu.VMEM((2 * PK_HEADS, lg_n, N_KEYS, LANES), F32),
            pltpu.VMEM((PK_HEADS, lg_n, N_KEYS, LANES), F32),
            pltpu.VMEM((PK_HEADS, lg_n, N_KEYS, LANES), F32),
            pltpu.VMEM((PK_HEADS, lg_n, N_KEYS, LANES), BF16),
            pltpu.VMEM((PK_HEADS, lg_n, N_KEYS, LANES), BF16),
            pltpu.VMEM((TOP_ROWS, LANES), F32),
            pltpu.VMEM((D_MODEL, tt), F32),
            pltpu.VMEM((D_MODEL, tt), BF16),
        ],
        compiler_params=_cparams(("arbitrary", "arbitrary")),
        name="peer",
    )(h2, x, ms.arr, wq, keys, u_tab, vt_tab, final_g.reshape(1, D_MODEL))


def _pad_lanes(v):
    return jnp.pad(v, ((0, 0), (0, LANES - v.shape[-1])))


def _prep_w_in(w):
    return jnp.concatenate(
        [w[:, :OFF_B], _pad_lanes(w[:, OFF_B:OFF_A]), _pad_lanes(w[:, OFF_A:OFF_SQ]), w[:, OFF_SQ:]],
        axis=1).astype(BF16)


def _layer(l, x, ms, prior, p, *, seq, tt, final_norm, final_g):
    t = x.shape[0]
    bsz = t // seq
    qkv, z, bcol, acol, sq, skv = _inproj(x, ms, p["norm1_g"][l], p["w_in"], tt)
    r3 = lambda a: a.reshape(bsz, seq, a.shape[-1])
    qkv3 = r3(qkv)
    if prior is None:
        conv_buf = jnp.zeros((bsz, DN_CONV - 1, CONV_DIM), F32)
        s0 = jnp.zeros((bsz, DN_HEADS, DN_DK, DN_DV), F32)
        dn_cfg = dict(nb_blk=1, lb=256, chunk=DN_CHUNK, valid_len=256)
    else:
        s0, conv_buf, k_buf, v_buf, n_new = prior
        dn_cfg = dict(nb_blk=8, lb=DEC_PAD, chunk=DEC_PAD, valid_len=n_new)
        r3 = lambda a: jnp.pad(a.reshape(bsz, seq, a.shape[-1]), ((0, 0), (0, DEC_PAD - seq), (0, 0)))
    o_dn, s_new = _deltanet(r3(qkv), r3(z), r3(bcol), r3(acol), p["conv_w"][l], p["alog"][l], p["dtb"][l],
                            p["dn_norm_g"][l], conv_buf, s0, **dn_cfg)
    if prior is None:
        o_sw = _swa_prompt(r3(sq), r3(skv), p["sw_sinks"][l])
        new_conv = qkv3[:, seq - (DN_CONV - 1):, :]
        n_buf = WINDOW
        new_k = r3(skv)[:, seq - n_buf:, :SW_KV_W].reshape(bsz, n_buf, SW_KVHEADS, SW_HD)
        new_v = r3(skv)[:, seq - n_buf:, SW_KV_W:].reshape(bsz, n_buf, SW_KVHEADS, SW_HD)
    else:
        p_len = k_buf.shape[1]
        o_sw, new_k, new_v = _swa_decode(r3(sq), r3(skv), k_buf.reshape(bsz, p_len, SW_KV_W),
                                         v_buf.reshape(bsz, p_len, SW_KV_W), p["sw_sinks"][l],
                                         nb_blk=16, n_new=n_new)
        new_conv = jnp.concatenate([conv_buf, qkv3], axis=1)[:, seq:]
        new_k = new_k.reshape(bsz, p_len, SW_KVHEADS, SW_HD)
        new_v = new_v.reshape(bsz, p_len, SW_KVHEADS, SW_HD)
        o_dn, o_sw = o_dn[:, :seq], o_sw[:, :seq]
    x, h2 = _outproj(x, o_dn.reshape(t, DN_V_W), o_sw.reshape(t, SW_Q_W), ms,
                     p["norm2_g"][l], p["w_out"], tt)
    x = _peer(h2, x, ms, p["wq"], p["keys"], p["u"], p["vt"], final_g,
              tt=tt, et=2048, final_norm=final_norm)
    return x, (s_new, new_conv, new_k, new_v)


def _trunk(x, mod, row0, tiles_per_group, prior_stack, p, final_g, *, seq, tt):
    new = []
    for l in range(DEPTH):
        prior = None if prior_stack is None else tuple(s[l] for s in prior_stack[:4]) + (prior_stack[4],)
        x, st = _layer(l, x, ModSrc(mod, l, row0, tiles_per_group), prior, p, seq=seq, tt=tt,
                       final_norm=(l == DEPTH - 1), final_g=final_g)
        new.append(st)
    return x, [jnp.stack(zz) for zz in zip(*new)]


def kernel(x_prompt, x_sample, state_delta, state_conv, cache_swa_k, cache_swa_v, c_prompt, c_sample,
           norm1_g, norm2_g, final_norm_g, w_ada, b_ada, w_in, conv_w, dn_a_log, dn_dt_bias, dn_norm_g,
           sw_sinks, w_out, peer_wq, peer_keys, peer_u, peer_v):
    batch, seq, _ = x_prompt.shape
    dec_b, dec_l, _ = x_sample.shape
    tt = 512

    p = dict(
        norm1_g=norm1_g, norm2_g=norm2_g,
        w_in=jax.vmap(_prep_w_in)(w_in),
        conv_w=conv_w,
        alog=_pad_lanes(dn_a_log).reshape(DEPTH, 1, LANES),
        dtb=_pad_lanes(dn_dt_bias).reshape(DEPTH, 1, LANES),
        dn_norm_g=dn_norm_g.reshape(DEPTH, 1, DN_DV),
        sw_sinks=sw_sinks,
        w_out=w_out.astype(BF16),
        wq=peer_wq.astype(BF16),
        keys=peer_keys.reshape(DEPTH, 2 * PK_HEADS, N_KEYS, PK_HALF).astype(BF16),
        u=peer_u.astype(BF16),
        vt=jnp.swapaxes(peer_v, 1, 2).astype(BF16),
    )

    n_dec = dec_b * dec_l
    assert batch <= SUBLANES and n_dec % tt == 0 and DN_CONV - 1 <= dec_l <= DEC_PAD
    c_dec = jnp.repeat(c_sample, dec_l, axis=0)
    pad_rows = (-(n_dec + batch)) % SUBLANES
    c_all = jnp.concatenate([c_dec, c_prompt, jnp.zeros((pad_rows, D_MODEL), F32)], axis=0)
    mod = _ada_mod(c_all, w_ada, b_ada)

    xp = x_prompt.reshape(batch * seq, D_MODEL)
    yp, (delta_p, conv_p, k_p, v_p) = _trunk(xp, mod, n_dec, seq // tt, None, p, final_norm_g, seq=seq, tt=tt)

    xs = x_sample.reshape(n_dec, D_MODEL)
    prior = (state_delta, state_conv, cache_swa_k, cache_swa_v, dec_l)
    ys, (delta_s, conv_s, k_s, v_s) = _trunk(xs, mod, 0, None, prior, p, final_norm_g, seq=dec_l, tt=tt)

    y_prompt = yp.reshape(batch, seq, D_MODEL)
    y_sample = ys.reshape(dec_b, dec_l, D_MODEL)
    return (y_prompt, y_sample, delta_p, conv_p, k_p, v_p, delta_s, conv_s, k_s, v_s)
```

```python
import functools
from typing import NamedTuple

import jax
import jax.numpy as jnp
from jax import lax
from jax.experimental import pallas as pl
from jax.experimental.pallas import tpu as pltpu

F32 = jnp.float32
BF16 = jnp.bfloat16

D_MODEL = 1024
DEPTH = 4
DN_HEADS = 4
DN_DK = 128
DN_DV = 128
DN_CONV = 4
DN_CHUNK = 64
SW_QHEADS = 8
SW_KVHEADS = 2
SW_HD = 64
SW_GROUP = SW_QHEADS // SW_KVHEADS
WINDOW = 128
DN_QK_W = DN_HEADS * DN_DK
DN_V_W = DN_HEADS * DN_DV
CONV_DIM = 2 * DN_QK_W + DN_V_W
SW_Q_W = SW_QHEADS * SW_HD
SW_KV_W = SW_KVHEADS * SW_HD
OFF_Z = CONV_DIM
OFF_B = OFF_Z + DN_V_W
OFF_A = OFF_B + DN_HEADS
OFF_SQ = OFF_A + DN_HEADS
PK_HEADS = 8
N_KEYS = 128
N_EXPERTS = N_KEYS * N_KEYS
PK_QDIM = 256
PK_HALF = PK_QDIM // 2
PK_TOPK = 16
EPS = 1e-6

LANES = 128
SUBLANES = 8
VMEM_LIMIT = 56 * 1024 * 1024
PROJ_W = CONV_DIM + DN_V_W + 2 * LANES + SW_Q_W + 2 * SW_KV_W
DEC_PAD = SUBLANES
N_TOP = PK_TOPK + 1
TOP_SPLIT = 4
TOP_T1_ROW = 3 * SUBLANES
TOP_ROWS = TOP_T1_ROW + 2 * SUBLANES
DN_INV_BLOCK = SUBLANES
NEG_INF = float("-inf")


def _cparams(sem):
    return pltpu.CompilerParams(dimension_semantics=sem, vmem_limit_bytes=VMEM_LIMIT)


def _silu(x):
    return x * jax.nn.sigmoid(x)


def _softplus(x):
    return jnp.maximum(x, 0.0) + jnp.log1p(jnp.exp(-jnp.abs(x)))


def _rms(x, g):
    return x * lax.rsqrt(jnp.mean(x * x, axis=-1, keepdims=True) + EPS) * g


def _dot(a, b):
    return jnp.dot(a, b, preferred_element_type=F32)


def _dot_nt(a, b):
    return lax.dot_general(a, b, (((1,), (1,)), ((), ())), preferred_element_type=F32)


def _dot_tn(a, b):
    return lax.dot_general(a, b, (((0,), (0,)), ((), ())), preferred_element_type=F32)


def _split_bf16(v):
    hi = v.astype(BF16)
    lo = (v - hi.astype(F32)).astype(BF16)
    return hi, lo


def _dot3(a, b, dot=_dot):
    a_hi, a_lo = _split_bf16(a)
    b_hi, b_lo = _split_bf16(b)
    return dot(a_hi, b_hi) + (dot(a_hi, b_lo) + dot(a_lo, b_hi))


def _ada_kernel(c_ref, w_ref, b_ref, o_ref):
    a = _silu(c_ref[...]).astype(BF16)
    o_ref[0] = _dot(a, w_ref[0].astype(BF16)) + b_ref[0]


def _ada_mod(c_all, w_ada, b_ada):
    rows = c_all.shape[0]
    tn = 1536
    return pl.pallas_call(
        _ada_kernel,
        grid=(DEPTH, 6 * D_MODEL // tn),
        in_specs=[
            pl.BlockSpec((rows, D_MODEL), lambda l, j: (0, 0)),
            pl.BlockSpec((1, D_MODEL, tn), lambda l, j: (l, 0, j)),
            pl.BlockSpec((1, 1, tn), lambda l, j: (l, 0, j)),
        ],
        out_specs=pl.BlockSpec((1, rows, tn), lambda l, j: (l, 0, j)),
        out_shape=jax.ShapeDtypeStruct((DEPTH, rows, 6 * D_MODEL), F32),
        compiler_params=_cparams(("arbitrary", "arbitrary")),
        name="ada_mod",
    )(c_all, w_ada, b_ada.reshape(DEPTH, 1, 6 * D_MODEL))


class ModSrc(NamedTuple):
    arr: jax.Array
    layer: int
    row0: int
    tiles_per_group: int | None


def _mod_spec(k, ms, tt):
    if ms.tiles_per_group is None:
        return pl.BlockSpec((1, tt, D_MODEL), lambda i, *_: (ms.layer, ms.row0 // tt + i, k))
    return pl.BlockSpec((1, SUBLANES, D_MODEL), lambda i, *_: (ms.layer, ms.row0 // SUBLANES, k))


def _mod_rows(ref, tiles_per_group):
    if tiles_per_group is None:
        return ref[0]
    return ref[0, pl.ds(pl.program_id(0) // tiles_per_group, 1), :]


def _inproj_kernel(x_ref, sh_ref, sc_ref, g_ref, w_ref,
                   qkv_ref, z_ref, b_ref, a_ref, sq_ref, skv_ref, *, tiles_per_group):
    sc = _mod_rows(sc_ref, tiles_per_group)
    sh = _mod_rows(sh_ref, tiles_per_group)
    h = _rms(x_ref[...], g_ref[...]) * (1.0 + sc) + sh
    p = _dot(h.astype(BF16), w_ref[...])
    o = 0
    for ref in (qkv_ref, z_ref, b_ref, a_ref, sq_ref, skv_ref):
        w = ref.shape[-1]
        ref[...] = p[:, o:o + w]
        o += w


def _inproj(x, ms, norm_g, w_r, tt):
    t = x.shape[0]
    widths = (CONV_DIM, DN_V_W, LANES, LANES, SW_Q_W, 2 * SW_KV_W)
    return pl.pallas_call(
        functools.partial(_inproj_kernel, tiles_per_group=ms.tiles_per_group),
        grid=(t // tt,),
        in_specs=[
            pl.BlockSpec((tt, D_MODEL), lambda i: (i, 0)),
            _mod_spec(0, ms, tt),
            _mod_spec(1, ms, tt),
            pl.BlockSpec((1, D_MODEL), lambda i: (0, 0)),
            pl.BlockSpec((None, D_MODEL, PROJ_W), lambda i: (ms.layer, 0, 0)),
        ],
        out_specs=[pl.BlockSpec((tt, w), lambda i: (i, 0)) for w in widths],
        out_shape=[jax.ShapeDtypeStruct((t, w), F32) for w in widths],
        compiler_params=_cparams(("arbitrary",)),
        name="inproj",
    )(x, ms.arr, ms.arr, norm_g.reshape(1, D_MODEL), w_r)


def _dn_kernel(u_ref, z_ref, b_ref, a_ref, cw_ref, alog_ref, dtb_ref, ng_ref, cbuf_ref, s0_ref,
               o_ref, sfin_ref, ubuf, qkv, beta_s, gc_s, gct_s, state,
               *, nb_blk, lb, chunk, valid_len):
    l = pl.program_id(1)
    nc = lb // chunk
    n_scan = chunk.bit_length() - 1

    @pl.when(l == 0)
    def _():
        state[...] = s0_ref[...]
        ubuf[:, SUBLANES - (DN_CONV - 1):SUBLANES, :] = cbuf_ref[...]

    ubuf[:, SUBLANES:SUBLANES + lb, :] = u_ref[...]
    y = None
    for i in range(DN_CONV):
        start = SUBLANES - (DN_CONV - 1) + i
        term = ubuf[:, start:start + lb, :] * cw_ref[i:i + 1, :]
        y = term if y is None else y + term
    qkv[...] = _silu(y)
    ubuf[:, SUBLANES - (DN_CONV - 1):SUBLANES, :] = ubuf[:, lb + SUBLANES - (DN_CONV - 1):lb + SUBLANES, :]

    row = lax.broadcasted_iota(jnp.int32, (lb, LANES), 0)
    for nb in range(nb_blk):
        beta = jax.nn.sigmoid(b_ref[nb])
        g = -jnp.exp(alog_ref[...]) * _softplus(a_ref[nb] + dtb_ref[...])
        if valid_len < lb:
            beta = jnp.where(row < valid_len, beta, 0.0)
            g = jnp.where(row < valid_len, g, 0.0)
        gc = g
        for s in range(n_scan):
            sh = 1 << s
            gc = gc + jnp.where((row % chunk) >= sh, pltpu.roll(gc, sh, 0), 0.0)
        beta_s[nb] = beta
        gc_s[nb] = gc
        for r0 in range(0, lb, LANES):
            nrow = min(LANES, lb - r0)
            blk = gc[r0:r0 + nrow]
            if nrow < LANES:
                blk = jnp.concatenate([blk, jnp.zeros((LANES - nrow, LANES), F32)], axis=0)
            blk_t = blk.T
            for c0 in range(0, nrow, chunk):
                gct_s[nb, (r0 + c0) // chunk] = blk_t[:, c0:c0 + chunk]

    ii = lax.broadcasted_iota(jnp.int32, (chunk, chunk), 0)
    jj = lax.broadcasted_iota(jnp.int32, (chunk, chunk), 1)
    causal = ii >= jj
    strict = ii > jj
    eye = jnp.where(ii == jj, 1.0, 0.0)

    items = [(nb, c, h) for nb in range(nb_blk) for c in range(nc) for h in range(DN_HEADS)]

    def rows_of(c):
        return slice(c * chunk, (c + 1) * chunk)

    def l2n(x):
        return x * lax.rsqrt(jnp.sum(x * x, axis=-1, keepdims=True) + EPS)

    qs = [l2n(qkv[nb, rows_of(c), h * DN_DK:(h + 1) * DN_DK]) * (DN_DK ** -0.5) for nb, c, h in items]
    ks = [l2n(qkv[nb, rows_of(c), DN_QK_W + h * DN_DK:DN_QK_W + (h + 1) * DN_DK]) for nb, c, h in items]
    vs = [qkv[nb, rows_of(c), 2 * DN_QK_W + h * DN_DV:2 * DN_QK_W + (h + 1) * DN_DV] for nb, c, h in items]
    cols = [gc_s[nb, rows_of(c), h:h + 1] for nb, c, h in items]
    rws = [gct_s[nb, c, h:h + 1, :] for nb, c, h in items]
    bcols = [beta_s[nb, rows_of(c), h:h + 1] for nb, c, h in items]
    lasts = [gc_s[nb, (c + 1) * chunk - 1:(c + 1) * chunk, h:h + 1] for nb, c, h in items]
    decays = [jnp.where(causal, jnp.exp(jnp.where(causal, col - rw, 0.0)), 0.0) for col, rw in zip(cols, rws)]
    kks = [_dot3(k, k, _dot_nt) for k in ks]
    a_mats = [jnp.where(strict, bcol * kk * decay, 0.0) for bcol, kk, decay in zip(bcols, kks, decays)]
    base = min(DN_INV_BLOCK, chunk)
    same = lambda b: (ii // b) == (jj // b)
    p_mats = [jnp.where(same(base), -a, 0.0) for a in a_mats]
    t_mats = [eye + n_mat for n_mat in p_mats]
    for _ in range(base.bit_length() - 2):
        p_mats = [_dot3(p_mat, p_mat) for p_mat in p_mats]
        t_mats = [t_mat + _dot3(t_mat, p_mat) for t_mat, p_mat in zip(t_mats, p_mats)]
    b = base
    while b < chunk:
        offs = [jnp.where(same(2 * b) & jnp.logical_not(same(b)), a, 0.0) for a in a_mats]
        t_mats = [t_mat - _dot3(_dot3(t_mat, off), t_mat) for t_mat, off in zip(t_mats, offs)]
        b *= 2
    e_cols = [jnp.exp(col) for col in cols]
    sols = [_dot3(t_mat, jnp.concatenate([bcol * v, (bcol * e_col) * k], axis=-1))
            for t_mat, bcol, v, e_col, k in zip(t_mats, bcols, vs, e_cols, ks)]
    qks = [_dot_nt(q, k) * decay for q, k, decay in zip(qs, ks, decays)]
    wqs = [jnp.concatenate([sol[:, DN_DV:], q * e_col], axis=0) for sol, q, e_col in zip(sols, qs, e_cols)]
    k_decs = [k * jnp.exp(last - col) for k, last, col in zip(ks, lasts, cols)]

    for idx, (nb, c, h) in enumerate(items):
        s_old = state[nb, h]
        ws = _dot(wqs[idx], s_old)
        uu = sols[idx][:, :DN_DV] - ws[:chunk]
        o = ws[chunk:] + _dot(qks[idx], uu)
        state[nb, h] = jnp.exp(lasts[idx]) * s_old + _dot_tn(k_decs[idx], uu)
        o = _rms(o, ng_ref[...])
        zz = z_ref[nb, rows_of(c), h * DN_DV:(h + 1) * DN_DV]
        o_ref[nb, rows_of(c), h * DN_DV:(h + 1) * DN_DV] = (o * _silu(zz)).astype(o_ref.dtype)

    @pl.when(l == pl.num_programs(1) - 1)
    def _():
        sfin_ref[...] = state[...]


def _deltanet(u, z, b, a, conv_w, alog, dtb, norm_g, conv_buf, s0, *, nb_blk, lb, chunk, valid_len):
    bsz, seq, _ = u.shape
    nc = lb // chunk
    kern = functools.partial(_dn_kernel, nb_blk=nb_blk, lb=lb, chunk=chunk, valid_len=valid_len)
    tok = lambda w: pl.BlockSpec((nb_blk, lb, w), lambda i, l: (i, l, 0))
    full2 = lambda r, w: pl.BlockSpec((r, w), lambda i, l: (0, 0))
    return pl.pallas_call(
        kern,
        grid=(bsz // nb_blk, seq // lb),
        in_specs=[
            tok(CONV_DIM), tok(DN_V_W), tok(LANES), tok(LANES),
            full2(DN_CONV, CONV_DIM), full2(1, LANES), full2(1, LANES), full2(1, DN_DV),
            pl.BlockSpec((nb_blk, DN_CONV - 1, CONV_DIM), lambda i, l: (i, 0, 0)),
            pl.BlockSpec((nb_blk, DN_HEADS, DN_DK, DN_DV), lambda i, l: (i, 0, 0, 0)),
        ],
        out_specs=[
            tok(DN_V_W),
            pl.BlockSpec((nb_blk, DN_HEADS, DN_DK, DN_DV), lambda i, l: (i, 0, 0, 0)),
        ],
        out_shape=[
            jax.ShapeDtypeStruct((bsz, seq, DN_V_W), F32),
            jax.ShapeDtypeStruct((bsz, DN_HEADS, DN_DK, DN_DV), F32),
        ],
        scratch_shapes=[
            pltpu.VMEM((nb_blk, lb + SUBLANES, CONV_DIM), F32),
            pltpu.VMEM((nb_blk, lb, CONV_DIM), F32),
            pltpu.VMEM((nb_blk, lb, LANES), F32),
            pltpu.VMEM((nb_blk, lb, LANES), F32),
            pltpu.VMEM((nb_blk, nc, LANES, chunk), F32),
            pltpu.VMEM((nb_blk, DN_HEADS, DN_DK, DN_DV), F32),
        ],
        compiler_params=_cparams(("arbitrary", "arbitrary")),
        name="deltanet",
    )(u, z, b, a, conv_w, alog, dtb, norm_g, conv_buf, s0)


def _swa_prompt_kernel(sink_ref, q_ref, kvp_ref, kvc_ref, o_ref):
    i = pl.program_id(1)
    q = q_ref[0]
    kvp = kvp_ref[0]
    kvc = kvc_ref[0]
    kcat = jnp.concatenate([kvp[:, :SW_KV_W], kvc[:, :SW_KV_W]], axis=0).astype(BF16)
    vcat = jnp.concatenate([kvp[:, SW_KV_W:], kvc[:, SW_KV_W:]], axis=0).astype(BF16)
    row = lax.broadcasted_iota(jnp.int32, (WINDOW, 2 * WINDOW), 0)
    col = lax.broadcasted_iota(jnp.int32, (WINDOW, 2 * WINDOW), 1)
    valid = (col > row) & (col <= row + WINDOW) & ((col >= WINDOW) | (i > 0))
    for qh in range(SW_QHEADS):
        kv = qh // SW_GROUP
        qs = q[:, qh * SW_HD:(qh + 1) * SW_HD].astype(BF16)
        ks = kcat[:, kv * SW_HD:(kv + 1) * SW_HD]
        vs = vcat[:, kv * SW_HD:(kv + 1) * SW_HD]
        s = _dot_nt(qs, ks) * (SW_HD ** -0.5)
        s = jnp.where(valid, s, NEG_INF)
        sink = sink_ref[qh]
        m = jnp.maximum(jnp.max(s, axis=-1, keepdims=True), sink)
        p = jnp.exp(s - m)
        den = jnp.sum(p, axis=-1, keepdims=True) + jnp.exp(sink - m)
        o = _dot(p.astype(BF16), vs) / den
        o_ref[0, :, qh * SW_HD:(qh + 1) * SW_HD] = o.astype(o_ref.dtype)


def _swa_prompt(sq, skv, sinks):
    bsz, seq, _ = sq.shape
    nblk = seq // WINDOW
    return pl.pallas_call(
        _swa_prompt_kernel,
        grid=(bsz, nblk),
        in_specs=[
            pl.BlockSpec(memory_space=pltpu.SMEM),
            pl.BlockSpec((1, WINDOW, SW_Q_W), lambda b, i: (b, i, 0)),
            pl.BlockSpec((1, WINDOW, 2 * SW_KV_W), lambda b, i: (b, jnp.maximum(i - 1, 0), 0)),
            pl.BlockSpec((1, WINDOW, 2 * SW_KV_W), lambda b, i: (b, i, 0)),
        ],
        out_specs=pl.BlockSpec((1, WINDOW, SW_Q_W), lambda b, i: (b, i, 0)),
        out_shape=jax.ShapeDtypeStruct((bsz, seq, SW_Q_W), F32),
        compiler_params=_cparams(("arbitrary", "arbitrary")),
        name="swa_prompt",
    )(sinks, sq, skv, skv)


def _swa_decode_kernel(sink_ref, q_ref, kvn_ref, kb_ref, vb_ref, o_ref, ko_ref, vo_ref, *, n_new):
    q = q_ref[...]
    kvn = kvn_ref[...]
    kb = kb_ref[...]
    vb = vb_ref[...]
    nb = q.shape[0]
    p_len = kb.shape[1]
    kn = kvn[:, :, :SW_KV_W]
    vn = kvn[:, :, SW_KV_W:]
    ko_ref[:, :p_len - n_new, :] = kb[:, n_new:, :]
    ko_ref[:, p_len - n_new:, :] = kn[:, :n_new, :]
    vo_ref[:, :p_len - n_new, :] = vb[:, n_new:, :]
    vo_ref[:, p_len - n_new:, :] = vn[:, :n_new, :]
    qi = lax.broadcasted_iota(jnp.int32, (nb, DEC_PAD, p_len), 1)
    kj = lax.broadcasted_iota(jnp.int32, (nb, DEC_PAD, p_len), 2)
    valid_buf = (p_len + qi - kj) < WINDOW
    qi2 = lax.broadcasted_iota(jnp.int32, (nb, DEC_PAD, DEC_PAD), 1)
    kj2 = lax.broadcasted_iota(jnp.int32, (nb, DEC_PAD, DEC_PAD), 2)
    valid_new = (kj2 <= qi2) & (kj2 < n_new)
    for qh in range(SW_QHEADS):
        kv = qh // SW_GROUP
        hs = slice(kv * SW_HD, (kv + 1) * SW_HD)
        qs = q[:, :, qh * SW_HD:(qh + 1) * SW_HD]
        sb = jnp.einsum("bqd,bkd->bqk", qs, kb[:, :, hs], preferred_element_type=F32) * (SW_HD ** -0.5)
        sn = jnp.einsum("bqd,bkd->bqk", qs, kn[:, :, hs], preferred_element_type=F32) * (SW_HD ** -0.5)
        sb = jnp.where(valid_buf, sb, NEG_INF)
        sn = jnp.where(valid_new, sn, NEG_INF)
        sink = sink_ref[qh]
        m = jnp.maximum(jnp.maximum(jnp.max(sb, axis=-1, keepdims=True),
                                    jnp.max(sn, axis=-1, keepdims=True)), sink)
        pb = jnp.exp(sb - m)
        pn = jnp.exp(sn - m)
        den = jnp.sum(pb, axis=-1, keepdims=True) + jnp.sum(pn, axis=-1, keepdims=True) + jnp.exp(sink - m)
        o = (jnp.einsum("bqk,bkd->bqd", pb, vb[:, :, hs], preferred_element_type=F32)
             + jnp.einsum("bqk,bkd->bqd", pn, vn[:, :, hs], preferred_element_type=F32))
        o_ref[:, :, qh * SW_HD:(qh + 1) * SW_HD] = (o / den).astype(o_ref.dtype)


def _swa_decode(sq, skv, k_buf, v_buf, sinks, *, nb_blk, n_new):
    bsz = sq.shape[0]
    p_len = k_buf.shape[1]
    tok = lambda w: pl.BlockSpec((nb_blk, DEC_PAD, w), lambda i: (i, 0, 0))
    cache = pl.BlockSpec((nb_blk, p_len, SW_KV_W), lambda i: (i, 0, 0))
    return pl.pallas_call(
        functools.partial(_swa_decode_kernel, n_new=n_new),
        grid=(bsz // nb_blk,),
        in_specs=[pl.BlockSpec(memory_space=pltpu.SMEM), tok(SW_Q_W), tok(2 * SW_KV_W), cache, cache],
        out_specs=[tok(SW_Q_W), cache, cache],
        out_shape=[
            jax.ShapeDtypeStruct((bsz, DEC_PAD, SW_Q_W), F32),
            jax.ShapeDtypeStruct((bsz, p_len, SW_KV_W), F32),
            jax.ShapeDtypeStruct((bsz, p_len, SW_KV_W), F32),
        ],
        compiler_params=_cparams(("arbitrary",)),
        name="swa_decode",
    )(sinks, sq, skv, k_buf, v_buf)


def _outproj_kernel(x_ref, odn_ref, osw_ref, g1_ref, sh2_ref, sc2_ref, n2_ref, w_ref, xo_ref, h2_ref,
                    *, tiles_per_group):
    mix = (_dot(odn_ref[...].astype(BF16), w_ref[:DN_V_W, :])
           + _dot(osw_ref[...].astype(BF16), w_ref[DN_V_W:, :]))
    x = x_ref[...] + _mod_rows(g1_ref, tiles_per_group) * mix
    xo_ref[...] = x
    h2 = (_rms(x, n2_ref[...]) * (1.0 + _mod_rows(sc2_ref, tiles_per_group))
          + _mod_rows(sh2_ref, tiles_per_group))
    h2_ref[...] = h2.astype(h2_ref.dtype)


def _outproj(x, o_dn, o_sw, ms, norm_g, w_out, tt):
    t = x.shape[0]
    return pl.pallas_call(
        functools.partial(_outproj_kernel, tiles_per_group=ms.tiles_per_group),
        grid=(t // tt,),
        in_specs=[
            pl.BlockSpec((tt, D_MODEL), lambda i: (i, 0)),
            pl.BlockSpec((tt, DN_V_W), lambda i: (i, 0)),
            pl.BlockSpec((tt, SW_Q_W), lambda i: (i, 0)),
            _mod_spec(2, ms, tt),
            _mod_spec(3, ms, tt),
            _mod_spec(4, ms, tt),
            pl.BlockSpec((1, D_MODEL), lambda i: (0, 0)),
            pl.BlockSpec((None, D_MODEL, D_MODEL), lambda i: (ms.layer, 0, 0)),
        ],
        out_specs=[pl.BlockSpec((tt, D_MODEL), lambda i: (i, 0))] * 2,
        out_shape=[jax.ShapeDtypeStruct((t, D_MODEL), F32), jax.ShapeDtypeStruct((t, D_MODEL), BF16)],
        compiler_params=_cparams(("arbitrary",)),
        name="outproj",
    )(x, o_dn, o_sw, ms.arr, ms.arr, ms.arr, norm_g.reshape(1, D_MODEL), w_out)


def _top_values(work, n, with_rank=False):
    out = []
    rank = jnp.full(work.shape, float(n), F32) if with_rank else None
    for it in range(n):
        m = jnp.max(work, axis=0, keepdims=True)
        out.append(m)
        if it + 1 < n or with_rank:
            hit = work == m
            if with_rank:
                rank = jnp.where(hit, float(it), rank)
            work = jnp.where(hit, NEG_INF, work)
    return (out, rank) if with_rank else out


def _peer_kernel(h2_ref, x_ref, g2_ref, wq_ref, keys_ref, u_ref, vt_ref, fg_ref, o_ref,
                 s_s, cnt_s, e1_s, rk_s, e2_s, tv_s, acc, hb_s, *, tt, et, final_norm, tiles_per_group):
    j = pl.program_id(1)
    lg_n = tt // LANES
    n_rr = et // N_KEYS
    bf16_rows = 2 * SUBLANES

    def row_bf16(row):
        one = jnp.broadcast_to(row, (bf16_rows, LANES)).astype(BF16)
        return jnp.concatenate([one] * (N_KEYS // bf16_rows), axis=0)

    @pl.when(j == 0)
    def _():
        acc[...] = jnp.zeros_like(acc)
        tv_s[...] = jnp.full(tv_s.shape, NEG_INF, F32)
        h2 = h2_ref[...]
        hb_s[...] = h2.astype(F32).T.astype(BF16)
        q = _dot(h2, wq_ref[...]).astype(BF16)
        for hp in range(2 * PK_HEADS):
            s_t = _dot_nt(keys_ref[hp], q[:, hp * PK_HALF:(hp + 1) * PK_HALF])
            for lg in range(lg_n):
                s_s[hp, lg] = s_t[:, lg * LANES:(lg + 1) * LANES]

        def head_body(idx, carry):
            h = idx // lg_n
            lg = idx % lg_n
            s1 = s_s[2 * h, lg]
            s2 = s_s[2 * h + 1, lg]
            t1 = _top_values(s1, N_TOP)
            t2, rank2 = _top_values(s2, N_TOP, with_rank=True)
            for it in range(N_TOP):
                tv_s[it:it + 1, :] = t2[it]
            for it in range(TOP_SPLIT, N_TOP):
                tv_s[TOP_T1_ROW + it - TOP_SPLIT:TOP_T1_ROW + it - TOP_SPLIT + 1, :] = t1[it]
            rid = lax.broadcasted_iota(jnp.int32, (SUBLANES, LANES), 0)
            cands = []

            def add_blocks(row, base, count):
                for r0 in range(0, count, SUBLANES):
                    blk = row + tv_s[base + r0:base + r0 + SUBLANES, :]
                    if count - r0 < SUBLANES:
                        blk = jnp.where(rid < count - r0, blk, NEG_INF)
                    cands.append(blk)

            for i in range(TOP_SPLIT):
                add_blocks(t1[i], 0, N_TOP // (i + 1))
            for jx in range(N_TOP // (TOP_SPLIT + 1)):
                add_blocks(t2[jx], TOP_T1_ROW, N_TOP // (jx + 1) - TOP_SPLIT)
            cand = jnp.concatenate(cands, axis=0)
            tops = _top_values(cand, N_TOP)
            thr = 0.5 * (tops[PK_TOPK - 1] + tops[PK_TOPK])
            m1 = t1[0]
            m2 = t2[0]
            zsum = jnp.sum(jnp.where(cand >= thr, jnp.exp(cand - (m1 + m2)), 0.0), axis=0, keepdims=True)
            need = thr - s1
            cnt = jnp.zeros_like(s1)
            for it in range(N_TOP):
                cnt = cnt + jnp.where(t2[it] >= need, 1.0, 0.0)
            cnt_s[h, lg] = cnt
            rk_s[h, lg] = rank2.astype(BF16)
            e1_s[h, lg] = jnp.exp(s1 - m1) * (0.5 / zsum)
            e2_s[h, lg] = jnp.exp(s2 - m2).astype(BF16)
            return carry

        lax.fori_loop(0, PK_HEADS * lg_n, head_body, 0, unroll=2)

    a_t = _dot(u_ref[...], hb_s[...])
    a_b = a_t.astype(BF16)
    act = a_b * (1.0 + lax.erf(a_b * (2.0 ** -0.5)))
    blocks = []
    for rr in range(n_rr):
        r = j * n_rr + rr
        row_blocks = []
        for lg in range(lg_n):
            g = None
            for h in range(PK_HEADS):
                cnt_b = row_bf16(cnt_s[h, lg, pl.ds(r, 1), :])
                e1_b = row_bf16(e1_s[h, lg, pl.ds(r, 1), :])
                term = jnp.where(rk_s[h, lg] < cnt_b, e2_s[h, lg], jnp.zeros((), BF16)) * e1_b
                g = term if g is None else g + term
            row_blocks.append(act[rr * N_KEYS:(rr + 1) * N_KEYS, lg * LANES:(lg + 1) * LANES] * g)
        blocks.append(jnp.concatenate(row_blocks, axis=1) if lg_n > 1 else row_blocks[0])
    p_t = jnp.concatenate(blocks, axis=0) if len(blocks) > 1 else blocks[0]
    acc[...] += _dot(vt_ref[...], p_t)

    @pl.when(j == pl.num_programs(1) - 1)
    def _():
        x = x_ref[...] + _mod_rows(g2_ref, tiles_per_group) * acc[...].T
        if final_norm:
            x = _rms(x, fg_ref[...])
        o_ref[...] = x


def _peer(h2, x, ms, wq, keys, u_tab, vt_tab, final_g, *, tt, et, final_norm):
    t = x.shape[0]
    lg_n = tt // LANES
    n_j = N_EXPERTS // et
    kern = functools.partial(_peer_kernel, tt=tt, et=et, final_norm=final_norm,
                             tiles_per_group=ms.tiles_per_group)
    const = lambda shape: pl.BlockSpec(shape, lambda i, j: (0,) * len(shape), pipeline_mode=pl.Buffered(1))
    layer_const = lambda shape: pl.BlockSpec((None,) + shape, lambda i, j: (ms.layer,) + (0,) * len(shape),
                                             pipeline_mode=pl.Buffered(1))
    return pl.pallas_call(
        kern,
        grid=(t // tt, n_j),
        in_specs=[
            pl.BlockSpec((tt, D_MODEL), lambda i, j: (i, 0)),
            pl.BlockSpec((tt, D_MODEL), lambda i, j: (i, 0)),
            _mod_spec(5, ms, tt),
            layer_const((D_MODEL, PK_HEADS * PK_QDIM)),
            layer_const((2 * PK_HEADS, N_KEYS, PK_HALF)),
            pl.BlockSpec((None, et, D_MODEL), lambda i, j: (ms.layer, j, 0)),
            pl.BlockSpec((None, D_MODEL, et), lambda i, j: (ms.layer, 0, j)),
            const((1, D_MODEL)),
        ],
        out_specs=pl.BlockSpec((tt, D_MODEL), lambda i, j: (i, 0)),
        out_shape=jax.ShapeDtypeStruct((t, D_MODEL), F32),
        scratch_shapes=[
            pltpu.VMEM((2 * PK_HEADS, lg_n, N_KEYS, LANES), F32),
            pltpu.VMEM((PK_HEADS, lg_n, N_KEYS, LANES), F32),
            pltpu.VMEM((PK_HEADS, lg_n, N_KEYS, LANES), F32),
            pltpu.VMEM((PK_HEADS, lg_n, N_KEYS, LANES), BF16),
            pltpu.VMEM((PK_HEADS, lg_n, N_KEYS, LANES), BF16),
            pltpu.VMEM((TOP_ROWS, LANES), F32),
            pltpu.VMEM((D_MODEL, tt), F32),
            pltpu.VMEM((D_MODEL, tt), BF16),
        ],
        compiler_params=_cparams(("arbitrary", "arbitrary")),
        name="peer",
    )(h2, x, ms.arr, wq, keys, u_tab, vt_tab, final_g.reshape(1, D_MODEL))


def _pad_lanes(v):
    return jnp.pad(v, ((0, 0), (0, LANES - v.shape[-1])))


def _prep_w_in(w):
    return jnp.concatenate(
        [w[:, :OFF_B], _pad_lanes(w[:, OFF_B:OFF_A]), _pad_lanes(w[:, OFF_A:OFF_SQ]), w[:, OFF_SQ:]],
        axis=1).astype(BF16)


def _layer(l, x, ms, prior, p, *, seq, tt, final_norm, final_g):
    t = x.shape[0]
    bsz = t // seq
    qkv, z, bcol, acol, sq, skv = _inproj(x, ms, p["norm1_g"][l], p["w_in"], tt)
    r3 = lambda a: a.reshape(bsz, seq, a.shape[-1])
    qkv3 = r3(qkv)
    if prior is None:
        conv_buf = jnp.zeros((bsz, DN_CONV - 1, CONV_DIM), F32)
        s0 = jnp.zeros((bsz, DN_HEADS, DN_DK, DN_DV), F32)
        dn_cfg = dict(nb_blk=1, lb=256, chunk=DN_CHUNK, valid_len=256)
    else:
        s0, conv_buf, k_buf, v_buf, n_new = prior
        dn_cfg = dict(nb_blk=8, lb=DEC_PAD, chunk=DEC_PAD, valid_len=n_new)
        r3 = lambda a: jnp.pad(a.reshape(bsz, seq, a.shape[-1]), ((0, 0), (0, DEC_PAD - seq), (0, 0)))
    o_dn, s_new = _deltanet(r3(qkv), r3(z), r3(bcol), r3(acol), p["conv_w"][l], p["alog"][l], p["dtb"][l],
                            p["dn_norm_g"][l], conv_buf, s0, **dn_cfg)
    if prior is None:
        o_sw = _swa_prompt(r3(sq), r3(skv), p["sw_sinks"][l])
        new_conv = qkv3[:, seq - (DN_CONV - 1):, :]
        n_buf = WINDOW
        new_k = r3(skv)[:, seq - n_buf:, :SW_KV_W].reshape(bsz, n_buf, SW_KVHEADS, SW_HD)
        new_v = r3(skv)[:, seq - n_buf:, SW_KV_W:].reshape(bsz, n_buf, SW_KVHEADS, SW_HD)
    else:
        p_len = k_buf.shape[1]
        o_sw, new_k, new_v = _swa_decode(r3(sq), r3(skv), k_buf.reshape(bsz, p_len, SW_KV_W),
                                         v_buf.reshape(bsz, p_len, SW_KV_W), p["sw_sinks"][l],
                                         nb_blk=16, n_new=n_new)
        new_conv = jnp.concatenate([conv_buf, qkv3], axis=1)[:, seq:]
        new_k = new_k.reshape(bsz, p_len, SW_KVHEADS, SW_HD)
        new_v = new_v.reshape(bsz, p_len, SW_KVHEADS, SW_HD)
        o_dn, o_sw = o_dn[:, :seq], o_sw[:, :seq]
    x, h2 = _outproj(x, o_dn.reshape(t, DN_V_W), o_sw.reshape(t, SW_Q_W), ms,
                     p["norm2_g"][l], p["w_out"], tt)
    x = _peer(h2, x, ms, p["wq"], p["keys"], p["u"], p["vt"], final_g,
              tt=tt, et=2048, final_norm=final_norm)
    return x, (s_new, new_conv, new_k, new_v)


def _trunk(x, mod, row0, tiles_per_group, prior_stack, p, final_g, *, seq, tt):
    new = []
    for l in range(DEPTH):
        prior = None if prior_stack is None else tuple(s[l] for s in prior_stack[:4]) + (prior_stack[4],)
        x, st = _layer(l, x, ModSrc(mod, l, row0, tiles_per_group), prior, p, seq=seq, tt=tt,
                       final_norm=(l == DEPTH - 1), final_g=final_g)
        new.append(st)
    return x, [jnp.stack(zz) for zz in zip(*new)]


def kernel(x_prompt, x_sample, state_delta, state_conv, cache_swa_k, cache_swa_v, c_prompt, c_sample,
           norm1_g, norm2_g, final_norm_g, w_ada, b_ada, w_in, conv_w, dn_a_log, dn_dt_bias, dn_norm_g,
           sw_sinks, w_out, peer_wq, peer_keys, peer_u, peer_v):
    batch, seq, _ = x_prompt.shape
    dec_b, dec_l, _ = x_sample.shape
    tt = 512

    p = dict(
        norm1_g=norm1_g, norm2_g=norm2_g,
        w_in=jax.vmap(_prep_w_in)(w_in),
        conv_w=conv_w,
        alog=_pad_lanes(dn_a_log).reshape(DEPTH, 1, LANES),
        dtb=_pad_lanes(dn_dt_bias).reshape(DEPTH, 1, LANES),
        dn_norm_g=dn_norm_g.reshape(DEPTH, 1, DN_DV),
        sw_sinks=sw_sinks,
        w_out=w_out.astype(BF16),
        wq=peer_wq.astype(BF16),
        keys=peer_keys.reshape(DEPTH, 2 * PK_HEADS, N_KEYS, PK_HALF).astype(BF16),
        u=peer_u.astype(BF16),
        vt=jnp.swapaxes(peer_v, 1, 2).astype(BF16),
    )

    n_dec = dec_b * dec_l
    assert batch <= SUBLANES and n_dec % tt == 0 and DN_CONV - 1 <= dec_l <= DEC_PAD
    c_dec = jnp.repeat(c_sample, dec_l, axis=0)
    pad_rows = (-(n_dec + batch)) % SUBLANES
    c_all = jnp.concatenate([c_dec, c_prompt, jnp.zeros((pad_rows, D_MODEL), F32)], axis=0)
    mod = _ada_mod(c_all, w_ada, b_ada)

    xp = x_prompt.reshape(batch * seq, D_MODEL)
    yp, (delta_p, conv_p, k_p, v_p) = _trunk(xp, mod, n_dec, seq // tt, None, p, final_norm_g, seq=seq, tt=tt)

    xs = x_sample.reshape(n_dec, D_MODEL)
    prior = (state_delta, state_conv, cache_swa_k, cache_swa_v, dec_l)
    ys, (delta_s, conv_s, k_s, v_s) = _trunk(xs, mod, 0, None, prior, p, final_norm_g, seq=dec_l, tt=tt)

    y_prompt = yp.reshape(batch, seq, D_MODEL)
    y_sample = ys.reshape(dec_b, dec_l, D_MODEL)
    return (y_prompt, y_sample, delta_p, conv_p, k_p, v_p, delta_s, conv_s, k_s, v_s)
```

```python
import functools
from typing import NamedTuple

import jax
import jax.numpy as jnp
from jax import lax
from jax.experimental import pallas as pl
from jax.experimental.pallas import tpu as pltpu

F32 = jnp.float32
BF16 = jnp.bfloat16

D_MODEL = 1024
DEPTH = 4
DN_HEADS = 4
DN_DK = 128
DN_DV = 128
DN_CONV = 4
DN_CHUNK = 64
SW_QHEADS = 8
SW_KVHEADS = 2
SW_HD = 64
SW_GROUP = SW_QHEADS // SW_KVHEADS
WINDOW = 128
DN_QK_W = DN_HEADS * DN_DK
DN_V_W = DN_HEADS * DN_DV
CONV_DIM = 2 * DN_QK_W + DN_V_W
SW_Q_W = SW_QHEADS * SW_HD
SW_KV_W = SW_KVHEADS * SW_HD
OFF_Z = CONV_DIM
OFF_B = OFF_Z + DN_V_W
OFF_A = OFF_B + DN_HEADS
OFF_SQ = OFF_A + DN_HEADS
PK_HEADS = 8
N_KEYS = 128
N_EXPERTS = N_KEYS * N_KEYS
PK_QDIM = 256
PK_HALF = PK_QDIM // 2
PK_TOPK = 16
EPS = 1e-6

LANES = 128
SUBLANES = 8
VMEM_LIMIT = 56 * 1024 * 1024
PROJ_W = CONV_DIM + DN_V_W + 2 * LANES + SW_Q_W + 2 * SW_KV_W
DEC_PAD = SUBLANES
N_TOP = PK_TOPK + 1
PEER_MIX_R = 4
TOP_SPLIT = 4
TOP_T1_ROW = 3 * SUBLANES
TOP_ROWS = TOP_T1_ROW + 2 * SUBLANES
DN_INV_BLOCK = SUBLANES
NEG_INF = float("-inf")


def _cparams(sem):
    return pltpu.CompilerParams(dimension_semantics=sem, vmem_limit_bytes=VMEM_LIMIT)


def _silu(x):
    return x * jax.nn.sigmoid(x)


def _softplus(x):
    return jnp.maximum(x, 0.0) + jnp.log1p(jnp.exp(-jnp.abs(x)))


def _rms(x, g):
    return x * lax.rsqrt(jnp.mean(x * x, axis=-1, keepdims=True) + EPS) * g


def _dot(a, b):
    return jnp.dot(a, b, preferred_element_type=F32)


def _dot_nt(a, b):
    return lax.dot_general(a, b, (((1,), (1,)), ((), ())), preferred_element_type=F32)


def _dot_tn(a, b):
    return lax.dot_general(a, b, (((0,), (0,)), ((), ())), preferred_element_type=F32)


def _split_bf16(v):
    hi = v.astype(BF16)
    lo = (v - hi.astype(F32)).astype(BF16)
    return hi, lo


def _dot3(a, b, dot=_dot):
    a_hi, a_lo = _split_bf16(a)
    b_hi, b_lo = _split_bf16(b)
    return dot(a_hi, b_hi) + (dot(a_hi, b_lo) + dot(a_lo, b_hi))


def _ada_kernel(c_ref, w_ref, b_ref, o_ref):
    a = _silu(c_ref[...]).astype(BF16)
    o_ref[0] = _dot(a, w_ref[0].astype(BF16)) + b_ref[0]


def _ada_mod(c_all, w_ada, b_ada):
    rows = c_all.shape[0]
    tn = 1536
    return pl.pallas_call(
        _ada_kernel,
        grid=(DEPTH, 6 * D_MODEL // tn),
        in_specs=[
            pl.BlockSpec((rows, D_MODEL), lambda l, j: (0, 0)),
            pl.BlockSpec((1, D_MODEL, tn), lambda l, j: (l, 0, j)),
            pl.BlockSpec((1, 1, tn), lambda l, j: (l, 0, j)),
        ],
        out_specs=pl.BlockSpec((1, rows, tn), lambda l, j: (l, 0, j)),
        out_shape=jax.ShapeDtypeStruct((DEPTH, rows, 6 * D_MODEL), F32),
        compiler_params=_cparams(("arbitrary", "arbitrary")),
        name="ada_mod",
    )(c_all, w_ada, b_ada.reshape(DEPTH, 1, 6 * D_MODEL))


class ModSrc(NamedTuple):
    arr: jax.Array
    layer: int
    row0: int
    tiles_per_group: int | None


def _mod_spec(k, ms, tt):
    if ms.tiles_per_group is None:
        return pl.BlockSpec((1, tt, D_MODEL), lambda i, *_: (ms.layer, ms.row0 // tt + i, k))
    return pl.BlockSpec((1, SUBLANES, D_MODEL), lambda i, *_: (ms.layer, ms.row0 // SUBLANES, k))


def _mod_rows(ref, tiles_per_group):
    if tiles_per_group is None:
        return ref[0]
    return ref[0, pl.ds(pl.program_id(0) // tiles_per_group, 1), :]


def _inproj_kernel(x_ref, sh_ref, sc_ref, g_ref, w_ref,
                   qkv_ref, z_ref, b_ref, a_ref, sq_ref, skv_ref, *, tiles_per_group):
    sc = _mod_rows(sc_ref, tiles_per_group)
    sh = _mod_rows(sh_ref, tiles_per_group)
    h = _rms(x_ref[...], g_ref[...]) * (1.0 + sc) + sh
    p = _dot(h.astype(BF16), w_ref[...])
    o = 0
    for ref in (qkv_ref, z_ref, b_ref, a_ref, sq_ref, skv_ref):
        w = ref.shape[-1]
        ref[...] = p[:, o:o + w]
        o += w


def _inproj(x, ms, norm_g, w_r, tt):
    t = x.shape[0]
    widths = (CONV_DIM, DN_V_W, LANES, LANES, SW_Q_W, 2 * SW_KV_W)
    return pl.pallas_call(
        functools.partial(_inproj_kernel, tiles_per_group=ms.tiles_per_group),
        grid=(t // tt,),
        in_specs=[
            pl.BlockSpec((tt, D_MODEL), lambda i: (i, 0)),
            _mod_spec(0, ms, tt),
            _mod_spec(1, ms, tt),
            pl.BlockSpec((1, D_MODEL), lambda i: (0, 0)),
            pl.BlockSpec((None, D_MODEL, PROJ_W), lambda i: (ms.layer, 0, 0)),
        ],
        out_specs=[pl.BlockSpec((tt, w), lambda i: (i, 0)) for w in widths],
        out_shape=[jax.ShapeDtypeStruct((t, w), F32) for w in widths],
        compiler_params=_cparams(("arbitrary",)),
        name="inproj",
    )(x, ms.arr, ms.arr, norm_g.reshape(1, D_MODEL), w_r)


def _dn_kernel(u_ref, z_ref, b_ref, a_ref, cw_ref, alog_ref, dtb_ref, ng_ref, cbuf_ref, s0_ref,
               o_ref, sfin_ref, ubuf, qkv, beta_s, gc_s, gct_s, state,
               *, nb_blk, lb, chunk, valid_len):
    l = pl.program_id(1)
    nc = lb // chunk
    n_scan = chunk.bit_length() - 1

    @pl.when(l == 0)
    def _():
        state[...] = s0_ref[...]
        ubuf[:, SUBLANES - (DN_CONV - 1):SUBLANES, :] = cbuf_ref[...]

    ubuf[:, SUBLANES:SUBLANES + lb, :] = u_ref[...]
    y = None
    for i in range(DN_CONV):
        start = SUBLANES - (DN_CONV - 1) + i
        term = ubuf[:, start:start + lb, :] * cw_ref[i:i + 1, :]
        y = term if y is None else y + term
    qkv[...] = _silu(y)
    ubuf[:, SUBLANES - (DN_CONV - 1):SUBLANES, :] = ubuf[:, lb + SUBLANES - (DN_CONV - 1):lb + SUBLANES, :]

    row = lax.broadcasted_iota(jnp.int32, (lb, LANES), 0)
    for nb in range(nb_blk):
        beta = jax.nn.sigmoid(b_ref[nb])
        g = -jnp.exp(alog_ref[...]) * _softplus(a_ref[nb] + dtb_ref[...])
        if valid_len < lb:
            beta = jnp.where(row < valid_len, beta, 0.0)
            g = jnp.where(row < valid_len, g, 0.0)
        gc = g
        for s in range(n_scan):
            sh = 1 << s
            gc = gc + jnp.where((row % chunk) >= sh, pltpu.roll(gc, sh, 0), 0.0)
        beta_s[nb] = beta
        gc_s[nb] = gc
        for r0 in range(0, lb, LANES):
            nrow = min(LANES, lb - r0)
            blk = gc[r0:r0 + nrow]
            if nrow < LANES:
                blk = jnp.concatenate([blk, jnp.zeros((LANES - nrow, LANES), F32)], axis=0)
            blk_t = blk.T
            for c0 in range(0, nrow, chunk):
                gct_s[nb, (r0 + c0) // chunk] = blk_t[:, c0:c0 + chunk]

    ii = lax.broadcasted_iota(jnp.int32, (chunk, chunk), 0)
    jj = lax.broadcasted_iota(jnp.int32, (chunk, chunk), 1)
    causal = ii >= jj
    strict = ii > jj
    eye = jnp.where(ii == jj, 1.0, 0.0)

    items = [(nb, c, h) for nb in range(nb_blk) for c in range(nc) for h in range(DN_HEADS)]

    def rows_of(c):
        return slice(c * chunk, (c + 1) * chunk)

    def l2n(x):
        return x * lax.rsqrt(jnp.sum(x * x, axis=-1, keepdims=True) + EPS)

    qs = [l2n(qkv[nb, rows_of(c), h * DN_DK:(h + 1) * DN_DK]) * (DN_DK ** -0.5) for nb, c, h in items]
    ks = [l2n(qkv[nb, rows_of(c), DN_QK_W + h * DN_DK:DN_QK_W + (h + 1) * DN_DK]) for nb, c, h in items]
    vs = [qkv[nb, rows_of(c), 2 * DN_QK_W + h * DN_DV:2 * DN_QK_W + (h + 1) * DN_DV] for nb, c, h in items]
    cols = [gc_s[nb, rows_of(c), h:h + 1] for nb, c, h in items]
    rws = [gct_s[nb, c, h:h + 1, :] for nb, c, h in items]
    bcols = [beta_s[nb, rows_of(c), h:h + 1] for nb, c, h in items]
    lasts = [gc_s[nb, (c + 1) * chunk - 1:(c + 1) * chunk, h:h + 1] for nb, c, h in items]
    decays = [jnp.where(causal, jnp.exp(jnp.where(causal, col - rw, 0.0)), 0.0) for col, rw in zip(cols, rws)]
    kks = [_dot3(k, k, _dot_nt) for k in ks]
    a_mats = [jnp.where(strict, bcol * kk * decay, 0.0) for bcol, kk, decay in zip(bcols, kks, decays)]
    base = min(DN_INV_BLOCK, chunk)
    same = lambda b: (ii // b) == (jj // b)
    p_mats = [jnp.where(same(base), -a, 0.0) for a in a_mats]
    t_mats = [eye + n_mat for n_mat in p_mats]
    for _ in range(base.bit_length() - 2):
        p_mats = [_dot3(p_mat, p_mat) for p_mat in p_mats]
        t_mats = [t_mat + _dot3(t_mat, p_mat) for t_mat, p_mat in zip(t_mats, p_mats)]
    b = base
    while b < chunk:
        offs = [jnp.where(same(2 * b) & jnp.logical_not(same(b)), a, 0.0) for a in a_mats]
        t_mats = [t_mat - _dot3(_dot3(t_mat, off), t_mat) for t_mat, off in zip(t_mats, offs)]
        b *= 2
    e_cols = [jnp.exp(col) for col in cols]
    sols = [_dot3(t_mat, jnp.concatenate([bcol * v, (bcol * e_col) * k], axis=-1))
            for t_mat, bcol, v, e_col, k in zip(t_mats, bcols, vs, e_cols, ks)]
    qks = [_dot_nt(q, k) * decay for q, k, decay in zip(qs, ks, decays)]
    wqs = [jnp.concatenate([sol[:, DN_DV:], q * e_col], axis=0) for sol, q, e_col in zip(sols, qs, e_cols)]
    k_decs = [k * jnp.exp(last - col) for k, last, col in zip(ks, lasts, cols)]

    for idx, (nb, c, h) in enumerate(items):
        s_old = state[nb, h]
        ws = _dot(wqs[idx], s_old)
        uu = sols[idx][:, :DN_DV] - ws[:chunk]
        o = ws[chunk:] + _dot(qks[idx], uu)
        state[nb, h] = jnp.exp(lasts[idx]) * s_old + _dot_tn(k_decs[idx], uu)
        o = _rms(o, ng_ref[...])
        zz = z_ref[nb, rows_of(c), h * DN_DV:(h + 1) * DN_DV]
        o_ref[nb, rows_of(c), h * DN_DV:(h + 1) * DN_DV] = (o * _silu(zz)).astype(o_ref.dtype)

    @pl.when(l == pl.num_programs(1) - 1)
    def _():
        sfin_ref[...] = state[...]


def _deltanet(u, z, b, a, conv_w, alog, dtb, norm_g, conv_buf, s0, *, nb_blk, lb, chunk, valid_len):
    bsz, seq, _ = u.shape
    nc = lb // chunk
    kern = functools.partial(_dn_kernel, nb_blk=nb_blk, lb=lb, chunk=chunk, valid_len=valid_len)
    tok = lambda w: pl.BlockSpec((nb_blk, lb, w), lambda i, l: (i, l, 0))
    full2 = lambda r, w: pl.BlockSpec((r, w), lambda i, l: (0, 0))
    return pl.pallas_call(
        kern,
        grid=(bsz // nb_blk, seq // lb),
        in_specs=[
            tok(CONV_DIM), tok(DN_V_W), tok(LANES), tok(LANES),
            full2(DN_CONV, CONV_DIM), full2(1, LANES), full2(1, LANES), full2(1, DN_DV),
            pl.BlockSpec((nb_blk, DN_CONV - 1, CONV_DIM), lambda i, l: (i, 0, 0)),
            pl.BlockSpec((nb_blk, DN_HEADS, DN_DK, DN_DV), lambda i, l: (i, 0, 0, 0)),
        ],
        out_specs=[
            tok(DN_V_W),
            pl.BlockSpec((nb_blk, DN_HEADS, DN_DK, DN_DV), lambda i, l: (i, 0, 0, 0)),
        ],
        out_shape=[
            jax.ShapeDtypeStruct((bsz, seq, DN_V_W), F32),
            jax.ShapeDtypeStruct((bsz, DN_HEADS, DN_DK, DN_DV), F32),
        ],
        scratch_shapes=[
            pltpu.VMEM((nb_blk, lb + SUBLANES, CONV_DIM), F32),
            pltpu.VMEM((nb_blk, lb, CONV_DIM), F32),
            pltpu.VMEM((nb_blk, lb, LANES), F32),
            pltpu.VMEM((nb_blk, lb, LANES), F32),
            pltpu.VMEM((nb_blk, nc, LANES, chunk), F32),
            pltpu.VMEM((nb_blk, DN_HEADS, DN_DK, DN_DV), F32),
        ],
        compiler_params=_cparams(("arbitrary", "arbitrary")),
        name="deltanet",
    )(u, z, b, a, conv_w, alog, dtb, norm_g, conv_buf, s0)


def _swa_prompt_kernel(sink_ref, q_ref, kvp_ref, kvc_ref, o_ref):
    i = pl.program_id(1)
    q = q_ref[0]
    kvp = kvp_ref[0]
    kvc = kvc_ref[0]
    kcat = jnp.concatenate([kvp[:, :SW_KV_W], kvc[:, :SW_KV_W]], axis=0).astype(BF16)
    vcat = jnp.concatenate([kvp[:, SW_KV_W:], kvc[:, SW_KV_W:]], axis=0).astype(BF16)
    row = lax.broadcasted_iota(jnp.int32, (WINDOW, 2 * WINDOW), 0)
    col = lax.broadcasted_iota(jnp.int32, (WINDOW, 2 * WINDOW), 1)
    valid = (col > row) & (col <= row + WINDOW) & ((col >= WINDOW) | (i > 0))
    for qh in range(SW_QHEADS):
        kv = qh // SW_GROUP
        qs = q[:, qh * SW_HD:(qh + 1) * SW_HD].astype(BF16)
        ks = kcat[:, kv * SW_HD:(kv + 1) * SW_HD]
        vs = vcat[:, kv * SW_HD:(kv + 1) * SW_HD]
        s = _dot_nt(qs, ks) * (SW_HD ** -0.5)
        s = jnp.where(valid, s, NEG_INF)
        sink = sink_ref[qh]
        m = jnp.maximum(jnp.max(s, axis=-1, keepdims=True), sink)
        p = jnp.exp(s - m)
        den = jnp.sum(p, axis=-1, keepdims=True) + jnp.exp(sink - m)
        o = _dot(p.astype(BF16), vs) / den
        o_ref[0, :, qh * SW_HD:(qh + 1) * SW_HD] = o.astype(o_ref.dtype)


def _swa_prompt(sq, skv, sinks):
    bsz, seq, _ = sq.shape
    nblk = seq // WINDOW
    return pl.pallas_call(
        _swa_prompt_kernel,
        grid=(bsz, nblk),
        in_specs=[
            pl.BlockSpec(memory_space=pltpu.SMEM),
            pl.BlockSpec((1, WINDOW, SW_Q_W), lambda b, i: (b, i, 0)),
            pl.BlockSpec((1, WINDOW, 2 * SW_KV_W), lambda b, i: (b, jnp.maximum(i - 1, 0), 0)),
            pl.BlockSpec((1, WINDOW, 2 * SW_KV_W), lambda b, i: (b, i, 0)),
        ],
        out_specs=pl.BlockSpec((1, WINDOW, SW_Q_W), lambda b, i: (b, i, 0)),
        out_shape=jax.ShapeDtypeStruct((bsz, seq, SW_Q_W), F32),
        compiler_params=_cparams(("arbitrary", "arbitrary")),
        name="swa_prompt",
    )(sinks, sq, skv, skv)


def _swa_decode_kernel(sink_ref, q_ref, kvn_ref, kb_ref, vb_ref, o_ref, ko_ref, vo_ref, *, n_new):
    q = q_ref[...]
    kvn = kvn_ref[...]
    kb = kb_ref[...]
    vb = vb_ref[...]
    nb = q.shape[0]
    p_len = kb.shape[1]
    kn = kvn[:, :, :SW_KV_W]
    vn = kvn[:, :, SW_KV_W:]
    ko_ref[:, :p_len - n_new, :] = kb[:, n_new:, :]
    ko_ref[:, p_len - n_new:, :] = kn[:, :n_new, :]
    vo_ref[:, :p_len - n_new, :] = vb[:, n_new:, :]
    vo_ref[:, p_len - n_new:, :] = vn[:, :n_new, :]
    qi = lax.broadcasted_iota(jnp.int32, (nb, DEC_PAD, p_len), 1)
    kj = lax.broadcasted_iota(jnp.int32, (nb, DEC_PAD, p_len), 2)
    valid_buf = (p_len + qi - kj) < WINDOW
    qi2 = lax.broadcasted_iota(jnp.int32, (nb, DEC_PAD, DEC_PAD), 1)
    kj2 = lax.broadcasted_iota(jnp.int32, (nb, DEC_PAD, DEC_PAD), 2)
    valid_new = (kj2 <= qi2) & (kj2 < n_new)
    for qh in range(SW_QHEADS):
        kv = qh // SW_GROUP
        hs = slice(kv * SW_HD, (kv + 1) * SW_HD)
        qs = q[:, :, qh * SW_HD:(qh + 1) * SW_HD]
        sb = jnp.einsum("bqd,bkd->bqk", qs, kb[:, :, hs], preferred_element_type=F32) * (SW_HD ** -0.5)
        sn = jnp.einsum("bqd,bkd->bqk", qs, kn[:, :, hs], preferred_element_type=F32) * (SW_HD ** -0.5)
        sb = jnp.where(valid_buf, sb, NEG_INF)
        sn = jnp.where(valid_new, sn, NEG_INF)
        sink = sink_ref[qh]
        m = jnp.maximum(jnp.maximum(jnp.max(sb, axis=-1, keepdims=True),
                                    jnp.max(sn, axis=-1, keepdims=True)), sink)
        pb = jnp.exp(sb - m)
        pn = jnp.exp(sn - m)
        den = jnp.sum(pb, axis=-1, keepdims=True) + jnp.sum(pn, axis=-1, keepdims=True) + jnp.exp(sink - m)
        o = (jnp.einsum("bqk,bkd->bqd", pb, vb[:, :, hs], preferred_element_type=F32)
             + jnp.einsum("bqk,bkd->bqd", pn, vn[:, :, hs], preferred_element_type=F32))
        o_ref[:, :, qh * SW_HD:(qh + 1) * SW_HD] = (o / den).astype(o_ref.dtype)


def _swa_decode(sq, skv, k_buf, v_buf, sinks, *, nb_blk, n_new):
    bsz = sq.shape[0]
    p_len = k_buf.shape[1]
    tok = lambda w: pl.BlockSpec((nb_blk, DEC_PAD, w), lambda i: (i, 0, 0))
    cache = pl.BlockSpec((nb_blk, p_len, SW_KV_W), lambda i: (i, 0, 0))
    return pl.pallas_call(
        functools.partial(_swa_decode_kernel, n_new=n_new),
        grid=(bsz // nb_blk,),
        in_specs=[pl.BlockSpec(memory_space=pltpu.SMEM), tok(SW_Q_W), tok(2 * SW_KV_W), cache, cache],
        out_specs=[tok(SW_Q_W), cache, cache],
        out_shape=[
            jax.ShapeDtypeStruct((bsz, DEC_PAD, SW_Q_W), F32),
            jax.ShapeDtypeStruct((bsz, p_len, SW_KV_W), F32),
            jax.ShapeDtypeStruct((bsz, p_len, SW_KV_W), F32),
        ],
        compiler_params=_cparams(("arbitrary",)),
        name="swa_decode",
    )(sinks, sq, skv, k_buf, v_buf)


def _outproj_kernel(x_ref, odn_ref, osw_ref, g1_ref, sh2_ref, sc2_ref, n2_ref, w_ref, xo_ref, h2_ref,
                    *, tiles_per_group):
    mix = (_dot(odn_ref[...].astype(BF16), w_ref[:DN_V_W, :])
           + _dot(osw_ref[...].astype(BF16), w_ref[DN_V_W:, :]))
    x = x_ref[...] + _mod_rows(g1_ref, tiles_per_group) * mix
    xo_ref[...] = x
    h2 = (_rms(x, n2_ref[...]) * (1.0 + _mod_rows(sc2_ref, tiles_per_group))
          + _mod_rows(sh2_ref, tiles_per_group))
    h2_ref[...] = h2.astype(h2_ref.dtype)


def _outproj(x, o_dn, o_sw, ms, norm_g, w_out, tt):
    t = x.shape[0]
    return pl.pallas_call(
        functools.partial(_outproj_kernel, tiles_per_group=ms.tiles_per_group),
        grid=(t // tt,),
        in_specs=[
            pl.BlockSpec((tt, D_MODEL), lambda i: (i, 0)),
            pl.BlockSpec((tt, DN_V_W), lambda i: (i, 0)),
            pl.BlockSpec((tt, SW_Q_W), lambda i: (i, 0)),
            _mod_spec(2, ms, tt),
            _mod_spec(3, ms, tt),
            _mod_spec(4, ms, tt),
            pl.BlockSpec((1, D_MODEL), lambda i: (0, 0)),
            pl.BlockSpec((None, D_MODEL, D_MODEL), lambda i: (ms.layer, 0, 0)),
        ],
        out_specs=[pl.BlockSpec((tt, D_MODEL), lambda i: (i, 0))] * 2,
        out_shape=[jax.ShapeDtypeStruct((t, D_MODEL), F32), jax.ShapeDtypeStruct((t, D_MODEL), BF16)],
        compiler_params=_cparams(("arbitrary",)),
        name="outproj",
    )(x, o_dn, o_sw, ms.arr, ms.arr, ms.arr, norm_g.reshape(1, D_MODEL), w_out)


def _top_values(work, n, with_rank=False):
    out = []
    rank = jnp.full(work.shape, float(n), F32) if with_rank else None
    for it in range(n):
        m = jnp.max(work, axis=0, keepdims=True)
        out.append(m)
        if it + 1 < n or with_rank:
            hit = work == m
            if with_rank:
                rank = jnp.where(hit, float(it), rank)
            work = jnp.where(hit, NEG_INF, work)
    return (out, rank) if with_rank else out


def _peer_kernel(h2_ref, x_ref, g2_ref, wq_ref, keys_ref, u_ref, vt_ref, fg_ref, o_ref,
                 s_s, cnt_s, e1_s, rk_s, e2_s, tv_s, acc, hb_s, *, tt, et, final_norm, tiles_per_group):
    j = pl.program_id(1)
    lg_n = tt // LANES
    n_rr = et // N_KEYS
    bf16_rows = 2 * SUBLANES

    def row_bf16(row):
        one = jnp.broadcast_to(row, (bf16_rows, LANES)).astype(BF16)
        return jnp.concatenate([one] * (N_KEYS // bf16_rows), axis=0)

    @pl.when(j == 0)
    def _():
        acc[...] = jnp.zeros_like(acc)
        tv_s[...] = jnp.full(tv_s.shape, NEG_INF, F32)
        h2 = h2_ref[...]
        hb_s[...] = h2.astype(F32).T.astype(BF16)
        q = _dot(h2, wq_ref[...]).astype(BF16)
        for hp in range(2 * PK_HEADS):
            s_t = _dot_nt(keys_ref[hp], q[:, hp * PK_HALF:(hp + 1) * PK_HALF])
            for lg in range(lg_n):
                s_s[hp, lg] = s_t[:, lg * LANES:(lg + 1) * LANES]

        def head_body(idx, carry):
            h = idx // lg_n
            lg = idx % lg_n
            s1 = s_s[2 * h, lg]
            s2 = s_s[2 * h + 1, lg]
            t1 = _top_values(s1, N_TOP)
            t2, rank2 = _top_values(s2, N_TOP, with_rank=True)
            for it in range(N_TOP):
                tv_s[it:it + 1, :] = t2[it]
            for it in range(TOP_SPLIT, N_TOP):
                tv_s[TOP_T1_ROW + it - TOP_SPLIT:TOP_T1_ROW + it - TOP_SPLIT + 1, :] = t1[it]
            rid = lax.broadcasted_iota(jnp.int32, (SUBLANES, LANES), 0)
            cands = []

            def add_blocks(row, base, count):
                for r0 in range(0, count, SUBLANES):
                    blk = row + tv_s[base + r0:base + r0 + SUBLANES, :]
                    if count - r0 < SUBLANES:
                        blk = jnp.where(rid < count - r0, blk, NEG_INF)
                    cands.append(blk)

            for i in range(TOP_SPLIT):
                add_blocks(t1[i], 0, N_TOP // (i + 1))
            for jx in range(N_TOP // (TOP_SPLIT + 1)):
                add_blocks(t2[jx], TOP_T1_ROW, N_TOP // (jx + 1) - TOP_SPLIT)
            cand = jnp.concatenate(cands, axis=0)
            tops = _top_values(cand, N_TOP)
            thr = 0.5 * (tops[PK_TOPK - 1] + tops[PK_TOPK])
            m1 = t1[0]
            m2 = t2[0]
            zsum = jnp.sum(jnp.where(cand >= thr, jnp.exp(cand - (m1 + m2)), 0.0), axis=0, keepdims=True)
            need = thr - s1
            cnt = jnp.zeros_like(s1)
            for it in range(N_TOP):
                cnt = cnt + jnp.where(t2[it] >= need, 1.0, 0.0)
            cnt_s[h, lg] = cnt
            rk_s[h, lg] = rank2.astype(BF16)
            e1_s[h, lg] = jnp.exp(s1 - m1) * (0.5 / zsum)
            e2_s[h, lg] = jnp.exp(s2 - m2).astype(BF16)
            return carry

        lax.fori_loop(0, PK_HEADS * lg_n, head_body, 0, unroll=2)

    a_t = _dot(u_ref[...], hb_s[...])
    a_b = a_t.astype(BF16)
    act = a_b * (1.0 + lax.erf(a_b * (2.0 ** -0.5)))
    total = None
    for r0 in range(0, n_rr, PEER_MIX_R):
        blocks = []
        for rr in range(r0, min(r0 + PEER_MIX_R, n_rr)):
            r = j * n_rr + rr
            row_blocks = []
            for lg in range(lg_n):
                g = None
                for h in range(PK_HEADS):
                    cnt_b = row_bf16(cnt_s[h, lg, pl.ds(r, 1), :])
                    e1_b = row_bf16(e1_s[h, lg, pl.ds(r, 1), :])
                    term = jnp.where(rk_s[h, lg] < cnt_b, e2_s[h, lg], jnp.zeros((), BF16)) * e1_b
                    g = term if g is None else g + term
                row_blocks.append(act[rr * N_KEYS:(rr + 1) * N_KEYS, lg * LANES:(lg + 1) * LANES] * g)
            blocks.append(jnp.concatenate(row_blocks, axis=1) if lg_n > 1 else row_blocks[0])
        p_t = jnp.concatenate(blocks, axis=0) if len(blocks) > 1 else blocks[0]
        cols = slice(r0 * N_KEYS, min(r0 + PEER_MIX_R, n_rr) * N_KEYS)
        part = _dot(vt_ref[:, cols], p_t)
        total = part if total is None else total + part
    acc[...] += total

    @pl.when(j == pl.num_programs(1) - 1)
    def _():
        x = x_ref[...] + _mod_rows(g2_ref, tiles_per_group) * acc[...].T
        if final_norm:
            x = _rms(x, fg_ref[...])
        o_ref[...] = x


def _peer(h2, x, ms, wq, keys, u_tab, vt_tab, final_g, *, tt, et, final_norm):
    t = x.shape[0]
    lg_n = tt // LANES
    n_j = N_EXPERTS // et
    kern = functools.partial(_peer_kernel, tt=tt, et=et, final_norm=final_norm,
                             tiles_per_group=ms.tiles_per_group)
    const = lambda shape: pl.BlockSpec(shape, lambda i, j: (0,) * len(shape), pipeline_mode=pl.Buffered(1))
    layer_const = lambda shape: pl.BlockSpec((None,) + shape, lambda i, j: (ms.layer,) + (0,) * len(shape),
                                             pipeline_mode=pl.Buffered(1))
    return pl.pallas_call(
        kern,
        grid=(t // tt, n_j),
        in_specs=[
            pl.BlockSpec((tt, D_MODEL), lambda i, j: (i, 0)),
            pl.BlockSpec((tt, D_MODEL), lambda i, j: (i, 0)),
            _mod_spec(5, ms, tt),
            layer_const((D_MODEL, PK_HEADS * PK_QDIM)),
            layer_const((2 * PK_HEADS, N_KEYS, PK_HALF)),
            pl.BlockSpec((None, et, D_MODEL), lambda i, j: (ms.layer, j, 0)),
            pl.BlockSpec((None, D_MODEL, et), lambda i, j: (ms.layer, 0, j)),
            const((1, D_MODEL)),
        ],
        out_specs=pl.BlockSpec((tt, D_MODEL), lambda i, j: (i, 0)),
        out_shape=jax.ShapeDtypeStruct((t, D_MODEL), F32),
        scratch_shapes=[
            pltpu.VMEM((2 * PK_HEADS, lg_n, N_KEYS, LANES), F32),
            pltpu.VMEM((PK_HEADS, lg_n, N_KEYS, LANES), F32),
            pltpu.VMEM((PK_HEADS, lg_n, N_KEYS, LANES), F32),
            pltpu.VMEM((PK_HEADS, lg_n, N_KEYS, LANES), BF16),
            pltpu.VMEM((PK_HEADS, lg_n, N_KEYS, LANES), BF16),
            pltpu.VMEM((TOP_ROWS, LANES), F32),
            pltpu.VMEM((D_MODEL, tt), F32),
            pltpu.VMEM((D_MODEL, tt), BF16),
        ],
        compiler_params=_cparams(("arbitrary", "arbitrary")),
        name="peer",
    )(h2, x, ms.arr, wq, keys, u_tab, vt_tab, final_g.reshape(1, D_MODEL))


def _pad_lanes(v):
    return jnp.pad(v, ((0, 0), (0, LANES - v.shape[-1])))


def _prep_w_in(w):
    return jnp.concatenate(
        [w[:, :OFF_B], _pad_lanes(w[:, OFF_B:OFF_A]), _pad_lanes(w[:, OFF_A:OFF_SQ]), w[:, OFF_SQ:]],
        axis=1).astype(BF16)


def _layer(l, x, ms, prior, p, *, seq, tt, final_norm, final_g):
    t = x.shape[0]
    bsz = t // seq
    qkv, z, bcol, acol, sq, skv = _inproj(x, ms, p["norm1_g"][l], p["w_in"], tt)
    r3 = lambda a: a.reshape(bsz, seq, a.shape[-1])
    qkv3 = r3(qkv)
    if prior is None:
        conv_buf = jnp.zeros((bsz, DN_CONV - 1, CONV_DIM), F32)
        s0 = jnp.zeros((bsz, DN_HEADS, DN_DK, DN_DV), F32)
        dn_cfg = dict(nb_blk=1, lb=256, chunk=DN_CHUNK, valid_len=256)
    else:
        s0, conv_buf, k_buf, v_buf, n_new = prior
        dn_cfg = dict(nb_blk=8, lb=DEC_PAD, chunk=DEC_PAD, valid_len=n_new)
        r3 = lambda a: jnp.pad(a.reshape(bsz, seq, a.shape[-1]), ((0, 0), (0, DEC_PAD - seq), (0, 0)))
    o_dn, s_new = _deltanet(r3(qkv), r3(z), r3(bcol), r3(acol), p["conv_w"][l], p["alog"][l], p["dtb"][l],
                            p["dn_norm_g"][l], conv_buf, s0, **dn_cfg)
    if prior is None:
        o_sw = _swa_prompt(r3(sq), r3(skv), p["sw_sinks"][l])
        new_conv = qkv3[:, seq - (DN_CONV - 1):, :]
        n_buf = WINDOW
        new_k = r3(skv)[:, seq - n_buf:, :SW_KV_W].reshape(bsz, n_buf, SW_KVHEADS, SW_HD)
        new_v = r3(skv)[:, seq - n_buf:, SW_KV_W:].reshape(bsz, n_buf, SW_KVHEADS, SW_HD)
    else:
        p_len = k_buf.shape[1]
        o_sw, new_k, new_v = _swa_decode(r3(sq), r3(skv), k_buf.reshape(bsz, p_len, SW_KV_W),
                                         v_buf.reshape(bsz, p_len, SW_KV_W), p["sw_sinks"][l],
                                         nb_blk=16, n_new=n_new)
        new_conv = jnp.concatenate([conv_buf, qkv3], axis=1)[:, seq:]
        new_k = new_k.reshape(bsz, p_len, SW_KVHEADS, SW_HD)
        new_v = new_v.reshape(bsz, p_len, SW_KVHEADS, SW_HD)
        o_dn, o_sw = o_dn[:, :seq], o_sw[:, :seq]
    x, h2 = _outproj(x, o_dn.reshape(t, DN_V_W), o_sw.reshape(t, SW_Q_W), ms,
                     p["norm2_g"][l], p["w_out"], tt)
    x = _peer(h2, x, ms, p["wq"], p["keys"], p["u"], p["vt"], final_g,
              tt=tt, et=2048, final_norm=final_norm)
    return x, (s_new, new_conv, new_k, new_v)


def _trunk(x, mod, row0, tiles_per_group, prior_stack, p, final_g, *, seq, tt):
    new = []
    for l in range(DEPTH):
        prior = None if prior_stack is None else tuple(s[l] for s in prior_stack[:4]) + (prior_stack[4],)
        x, st = _layer(l, x, ModSrc(mod, l, row0, tiles_per_group), prior, p, seq=seq, tt=tt,
                       final_norm=(l == DEPTH - 1), final_g=final_g)
        new.append(st)
    return x, [jnp.stack(zz) for zz in zip(*new)]


def kernel(x_prompt, x_sample, state_delta, state_conv, cache_swa_k, cache_swa_v, c_prompt, c_sample,
           norm1_g, norm2_g, final_norm_g, w_ada, b_ada, w_in, conv_w, dn_a_log, dn_dt_bias, dn_norm_g,
           sw_sinks, w_out, peer_wq, peer_keys, peer_u, peer_v):
    batch, seq, _ = x_prompt.shape
    dec_b, dec_l, _ = x_sample.shape
    tt = 512

    p = dict(
        norm1_g=norm1_g, norm2_g=norm2_g,
        w_in=jax.vmap(_prep_w_in)(w_in),
        conv_w=conv_w,
        alog=_pad_lanes(dn_a_log).reshape(DEPTH, 1, LANES),
        dtb=_pad_lanes(dn_dt_bias).reshape(DEPTH, 1, LANES),
        dn_norm_g=dn_norm_g.reshape(DEPTH, 1, DN_DV),
        sw_sinks=sw_sinks,
        w_out=w_out.astype(BF16),
        wq=peer_wq.astype(BF16),
        keys=peer_keys.reshape(DEPTH, 2 * PK_HEADS, N_KEYS, PK_HALF).astype(BF16),
        u=peer_u.astype(BF16),
        vt=jnp.swapaxes(peer_v, 1, 2).astype(BF16),
    )

    n_dec = dec_b * dec_l
    assert batch <= SUBLANES and n_dec % tt == 0 and DN_CONV - 1 <= dec_l <= DEC_PAD
    c_dec = jnp.repeat(c_sample, dec_l, axis=0)
    pad_rows = (-(n_dec + batch)) % SUBLANES
    c_all = jnp.concatenate([c_dec, c_prompt, jnp.zeros((pad_rows, D_MODEL), F32)], axis=0)
    mod = _ada_mod(c_all, w_ada, b_ada)

    xp = x_prompt.reshape(batch * seq, D_MODEL)
    yp, (delta_p, conv_p, k_p, v_p) = _trunk(xp, mod, n_dec, seq // tt, None, p, final_norm_g, seq=seq, tt=tt)

    xs = x_sample.reshape(n_dec, D_MODEL)
    prior = (state_delta, state_conv, cache_swa_k, cache_swa_v, dec_l)
    ys, (delta_s, conv_s, k_s, v_s) = _trunk(xs, mod, 0, None, prior, p, final_norm_g, seq=dec_l, tt=tt)

    y_prompt = yp.reshape(batch, seq, D_MODEL)
    y_sample = ys.reshape(dec_b, dec_l, D_MODEL)
    return (y_prompt, y_sample, delta_p, conv_p, k_p, v_p, delta_s, conv_s, k_s, v_s)
```

```python
import functools
from typing import NamedTuple

import jax
import jax.numpy as jnp
from jax import lax
from jax.experimental import pallas as pl
from jax.experimental.pallas import tpu as pltpu

F32 = jnp.float32
BF16 = jnp.bfloat16

D_MODEL = 1024
DEPTH = 4
DN_HEADS = 4
DN_DK = 128
DN_DV = 128
DN_CONV = 4
DN_CHUNK = 64
SW_QHEADS = 8
SW_KVHEADS = 2
SW_HD = 64
SW_GROUP = SW_QHEADS // SW_KVHEADS
WINDOW = 128
DN_QK_W = DN_HEADS * DN_DK
DN_V_W = DN_HEADS * DN_DV
CONV_DIM = 2 * DN_QK_W + DN_V_W
SW_Q_W = SW_QHEADS * SW_HD
SW_KV_W = SW_KVHEADS * SW_HD
OFF_Z = CONV_DIM
OFF_B = OFF_Z + DN_V_W
OFF_A = OFF_B + DN_HEADS
OFF_SQ = OFF_A + DN_HEADS
PK_HEADS = 8
N_KEYS = 128
N_EXPERTS = N_KEYS * N_KEYS
PK_QDIM = 256
PK_HALF = PK_QDIM // 2
PK_TOPK = 16
EPS = 1e-6

LANES = 128
SUBLANES = 8
VMEM_LIMIT = 56 * 1024 * 1024
PROJ_W = CONV_DIM + DN_V_W + 2 * LANES + SW_Q_W + 2 * SW_KV_W
DEC_PAD = SUBLANES
N_TOP = PK_TOPK + 1
PEER_MIX_R = 8
TOP_SPLIT = 4
TOP_T1_ROW = 3 * SUBLANES
TOP_ROWS = TOP_T1_ROW + 2 * SUBLANES
DN_INV_BLOCK = SUBLANES
NEG_INF = float("-inf")


def _cparams(sem):
    return pltpu.CompilerParams(dimension_semantics=sem, vmem_limit_bytes=VMEM_LIMIT)


def _silu(x):
    return x * jax.nn.sigmoid(x)


def _softplus(x):
    return jnp.maximum(x, 0.0) + jnp.log1p(jnp.exp(-jnp.abs(x)))


def _rms(x, g):
    return x * lax.rsqrt(jnp.mean(x * x, axis=-1, keepdims=True) + EPS) * g


def _dot(a, b):
    return jnp.dot(a, b, preferred_element_type=F32)


def _dot_nt(a, b):
    return lax.dot_general(a, b, (((1,), (1,)), ((), ())), preferred_element_type=F32)


def _dot_tn(a, b):
    return lax.dot_general(a, b, (((0,), (0,)), ((), ())), preferred_element_type=F32)


def _split_bf16(v):
    hi = v.astype(BF16)
    lo = (v - hi.astype(F32)).astype(BF16)
    return hi, lo


def _dot3(a, b, dot=_dot):
    a_hi, a_lo = _split_bf16(a)
    b_hi, b_lo = _split_bf16(b)
    return dot(a_hi, b_hi) + (dot(a_hi, b_lo) + dot(a_lo, b_hi))


def _ada_kernel(c_ref, w_ref, b_ref, o_ref):
    a = _silu(c_ref[...]).astype(BF16)
    o_ref[0] = _dot(a, w_ref[0].astype(BF16)) + b_ref[0]


def _ada_mod(c_all, w_ada, b_ada):
    rows = c_all.shape[0]
    tn = 1536
    return pl.pallas_call(
        _ada_kernel,
        grid=(DEPTH, 6 * D_MODEL // tn),
        in_specs=[
            pl.BlockSpec((rows, D_MODEL), lambda l, j: (0, 0)),
            pl.BlockSpec((1, D_MODEL, tn), lambda l, j: (l, 0, j)),
            pl.BlockSpec((1, 1, tn), lambda l, j: (l, 0, j)),
        ],
        out_specs=pl.BlockSpec((1, rows, tn), lambda l, j: (l, 0, j)),
        out_shape=jax.ShapeDtypeStruct((DEPTH, rows, 6 * D_MODEL), F32),
        compiler_params=_cparams(("arbitrary", "arbitrary")),
        name="ada_mod",
    )(c_all, w_ada, b_ada.reshape(DEPTH, 1, 6 * D_MODEL))


class ModSrc(NamedTuple):
    arr: jax.Array
    layer: int
    row0: int
    tiles_per_group: int | None


def _mod_spec(k, ms, tt):
    if ms.tiles_per_group is None:
        return pl.BlockSpec((1, tt, D_MODEL), lambda i, *_: (ms.layer, ms.row0 // tt + i, k))
    return pl.BlockSpec((1, SUBLANES, D_MODEL), lambda i, *_: (ms.layer, ms.row0 // SUBLANES, k))


def _mod_rows(ref, tiles_per_group):
    if tiles_per_group is None:
        return ref[0]
    return ref[0, pl.ds(pl.program_id(0) // tiles_per_group, 1), :]


def _inproj_kernel(x_ref, sh_ref, sc_ref, g_ref, w_ref,
                   qkv_ref, z_ref, b_ref, a_ref, sq_ref, skv_ref, *, tiles_per_group):
    sc = _mod_rows(sc_ref, tiles_per_group)
    sh = _mod_rows(sh_ref, tiles_per_group)
    h = _rms(x_ref[...], g_ref[...]) * (1.0 + sc) + sh
    p = _dot(h.astype(BF16), w_ref[...])
    o = 0
    for ref in (qkv_ref, z_ref, b_ref, a_ref, sq_ref, skv_ref):
        w = ref.shape[-1]
        ref[...] = p[:, o:o + w]
        o += w


def _inproj(x, ms, norm_g, w_r, tt):
    t = x.shape[0]
    widths = (CONV_DIM, DN_V_W, LANES, LANES, SW_Q_W, 2 * SW_KV_W)
    return pl.pallas_call(
        functools.partial(_inproj_kernel, tiles_per_group=ms.tiles_per_group),
        grid=(t // tt,),
        in_specs=[
            pl.BlockSpec((tt, D_MODEL), lambda i: (i, 0)),
            _mod_spec(0, ms, tt),
            _mod_spec(1, ms, tt),
            pl.BlockSpec((1, D_MODEL), lambda i: (0, 0)),
            pl.BlockSpec((None, D_MODEL, PROJ_W), lambda i: (ms.layer, 0, 0)),
        ],
        out_specs=[pl.BlockSpec((tt, w), lambda i: (i, 0)) for w in widths],
        out_shape=[jax.ShapeDtypeStruct((t, w), F32) for w in widths],
        compiler_params=_cparams(("arbitrary",)),
        name="inproj",
    )(x, ms.arr, ms.arr, norm_g.reshape(1, D_MODEL), w_r)


def _dn_kernel(u_ref, z_ref, b_ref, a_ref, cw_ref, alog_ref, dtb_ref, ng_ref, cbuf_ref, s0_ref,
               o_ref, sfin_ref, ubuf, qkv, beta_s, gc_s, gct_s, state,
               *, nb_blk, lb, chunk, valid_len):
    l = pl.program_id(1)
    nc = lb // chunk
    n_scan = chunk.bit_length() - 1

    @pl.when(l == 0)
    def _():
        state[...] = s0_ref[...]
        ubuf[:, SUBLANES - (DN_CONV - 1):SUBLANES, :] = cbuf_ref[...]

    ubuf[:, SUBLANES:SUBLANES + lb, :] = u_ref[...]
    y = None
    for i in range(DN_CONV):
        start = SUBLANES - (DN_CONV - 1) + i
        term = ubuf[:, start:start + lb, :] * cw_ref[i:i + 1, :]
        y = term if y is None else y + term
    qkv[...] = _silu(y)
    ubuf[:, SUBLANES - (DN_CONV - 1):SUBLANES, :] = ubuf[:, lb + SUBLANES - (DN_CONV - 1):lb + SUBLANES, :]

    row = lax.broadcasted_iota(jnp.int32, (lb, LANES), 0)
    for nb in range(nb_blk):
        beta = jax.nn.sigmoid(b_ref[nb])
        g = -jnp.exp(alog_ref[...]) * _softplus(a_ref[nb] + dtb_ref[...])
        if valid_len < lb:
            beta = jnp.where(row < valid_len, beta, 0.0)
            g = jnp.where(row < valid_len, g, 0.0)
        gc = g
        for s in range(n_scan):
            sh = 1 << s
            gc = gc + jnp.where((row % chunk) >= sh, pltpu.roll(gc, sh, 0), 0.0)
        beta_s[nb] = beta
        gc_s[nb] = gc
        for r0 in range(0, lb, LANES):
            nrow = min(LANES, lb - r0)
            blk = gc[r0:r0 + nrow]
            if nrow < LANES:
                blk = jnp.concatenate([blk, jnp.zeros((LANES - nrow, LANES), F32)], axis=0)
            blk_t = blk.T
            for c0 in range(0, nrow, chunk):
                gct_s[nb, (r0 + c0) // chunk] = blk_t[:, c0:c0 + chunk]

    ii = lax.broadcasted_iota(jnp.int32, (chunk, chunk), 0)
    jj = lax.broadcasted_iota(jnp.int32, (chunk, chunk), 1)
    causal = ii >= jj
    strict = ii > jj
    eye = jnp.where(ii == jj, 1.0, 0.0)

    items = [(nb, c, h) for nb in range(nb_blk) for c in range(nc) for h in range(DN_HEADS)]

    def rows_of(c):
        return slice(c * chunk, (c + 1) * chunk)

    def l2n(x):
        return x * lax.rsqrt(jnp.sum(x * x, axis=-1, keepdims=True) + EPS)

    qs = [l2n(qkv[nb, rows_of(c), h * DN_DK:(h + 1) * DN_DK]) * (DN_DK ** -0.5) for nb, c, h in items]
    ks = [l2n(qkv[nb, rows_of(c), DN_QK_W + h * DN_DK:DN_QK_W + (h + 1) * DN_DK]) for nb, c, h in items]
    vs = [qkv[nb, rows_of(c), 2 * DN_QK_W + h * DN_DV:2 * DN_QK_W + (h + 1) * DN_DV] for nb, c, h in items]
    cols = [gc_s[nb, rows_of(c), h:h + 1] for nb, c, h in items]
    rws = [gct_s[nb, c, h:h + 1, :] for nb, c, h in items]
    bcols = [beta_s[nb, rows_of(c), h:h + 1] for nb, c, h in items]
    lasts = [gc_s[nb, (c + 1) * chunk - 1:(c + 1) * chunk, h:h + 1] for nb, c, h in items]
    decays = [jnp.where(causal, jnp.exp(jnp.where(causal, col - rw, 0.0)), 0.0) for col, rw in zip(cols, rws)]
    kks = [_dot3(k, k, _dot_nt) for k in ks]
    a_mats = [jnp.where(strict, bcol * kk * decay, 0.0) for bcol, kk, decay in zip(bcols, kks, decays)]
    base = min(DN_INV_BLOCK, chunk)
    same = lambda b: (ii // b) == (jj // b)
    p_mats = [jnp.where(same(base), -a, 0.0) for a in a_mats]
    t_mats = [eye + n_mat for n_mat in p_mats]
    for _ in range(base.bit_length() - 2):
        p_mats = [_dot3(p_mat, p_mat) for p_mat in p_mats]
        t_mats = [t_mat + _dot3(t_mat, p_mat) for t_mat, p_mat in zip(t_mats, p_mats)]
    b = base
    while b < chunk:
        offs = [jnp.where(same(2 * b) & jnp.logical_not(same(b)), a, 0.0) for a in a_mats]
        t_mats = [t_mat - _dot3(_dot3(t_mat, off), t_mat) for t_mat, off in zip(t_mats, offs)]
        b *= 2
    e_cols = [jnp.exp(col) for col in cols]
    sols = [_dot3(t_mat, jnp.concatenate([bcol * v, (bcol * e_col) * k], axis=-1))
            for t_mat, bcol, v, e_col, k in zip(t_mats, bcols, vs, e_cols, ks)]
    qks = [_dot_nt(q, k) * decay for q, k, decay in zip(qs, ks, decays)]
    wqs = [jnp.concatenate([sol[:, DN_DV:], q * e_col], axis=0) for sol, q, e_col in zip(sols, qs, e_cols)]
    k_decs = [k * jnp.exp(last - col) for k, last, col in zip(ks, lasts, cols)]

    for idx, (nb, c, h) in enumerate(items):
        s_old = state[nb, h]
        ws = _dot(wqs[idx], s_old)
        uu = sols[idx][:, :DN_DV] - ws[:chunk]
        o = ws[chunk:] + _dot(qks[idx], uu)
        state[nb, h] = jnp.exp(lasts[idx]) * s_old + _dot_tn(k_decs[idx], uu)
        o = _rms(o, ng_ref[...])
        zz = z_ref[nb, rows_of(c), h * DN_DV:(h + 1) * DN_DV]
        o_ref[nb, rows_of(c), h * DN_DV:(h + 1) * DN_DV] = (o * _silu(zz)).astype(o_ref.dtype)

    @pl.when(l == pl.num_programs(1) - 1)
    def _():
        sfin_ref[...] = state[...]


def _deltanet(u, z, b, a, conv_w, alog, dtb, norm_g, conv_buf, s0, *, nb_blk, lb, chunk, valid_len):
    bsz, seq, _ = u.shape
    nc = lb // chunk
    kern = functools.partial(_dn_kernel, nb_blk=nb_blk, lb=lb, chunk=chunk, valid_len=valid_len)
    tok = lambda w: pl.BlockSpec((nb_blk, lb, w), lambda i, l: (i, l, 0))
    full2 = lambda r, w: pl.BlockSpec((r, w), lambda i, l: (0, 0))
    return pl.pallas_call(
        kern,
        grid=(bsz // nb_blk, seq // lb),
        in_specs=[
            tok(CONV_DIM), tok(DN_V_W), tok(LANES), tok(LANES),
            full2(DN_CONV, CONV_DIM), full2(1, LANES), full2(1, LANES), full2(1, DN_DV),
            pl.BlockSpec((nb_blk, DN_CONV - 1, CONV_DIM), lambda i, l: (i, 0, 0)),
            pl.BlockSpec((nb_blk, DN_HEADS, DN_DK, DN_DV), lambda i, l: (i, 0, 0, 0)),
        ],
        out_specs=[
            tok(DN_V_W),
            pl.BlockSpec((nb_blk, DN_HEADS, DN_DK, DN_DV), lambda i, l: (i, 0, 0, 0)),
        ],
        out_shape=[
            jax.ShapeDtypeStruct((bsz, seq, DN_V_W), F32),
            jax.ShapeDtypeStruct((bsz, DN_HEADS, DN_DK, DN_DV), F32),
        ],
        scratch_shapes=[
            pltpu.VMEM((nb_blk, lb + SUBLANES, CONV_DIM), F32),
            pltpu.VMEM((nb_blk, lb, CONV_DIM), F32),
            pltpu.VMEM((nb_blk, lb, LANES), F32),
            pltpu.VMEM((nb_blk, lb, LANES), F32),
            pltpu.VMEM((nb_blk, nc, LANES, chunk), F32),
            pltpu.VMEM((nb_blk, DN_HEADS, DN_DK, DN_DV), F32),
        ],
        compiler_params=_cparams(("arbitrary", "arbitrary")),
        name="deltanet",
    )(u, z, b, a, conv_w, alog, dtb, norm_g, conv_buf, s0)


def _swa_prompt_kernel(sink_ref, q_ref, kvp_ref, kvc_ref, o_ref):
    i = pl.program_id(1)
    q = q_ref[0]
    kvp = kvp_ref[0]
    kvc = kvc_ref[0]
    kcat = jnp.concatenate([kvp[:, :SW_KV_W], kvc[:, :SW_KV_W]], axis=0).astype(BF16)
    vcat = jnp.concatenate([kvp[:, SW_KV_W:], kvc[:, SW_KV_W:]], axis=0).astype(BF16)
    row = lax.broadcasted_iota(jnp.int32, (WINDOW, 2 * WINDOW), 0)
    col = lax.broadcasted_iota(jnp.int32, (WINDOW, 2 * WINDOW), 1)
    valid = (col > row) & (col <= row + WINDOW) & ((col >= WINDOW) | (i > 0))
    for qh in range(SW_QHEADS):
        kv = qh // SW_GROUP
        qs = q[:, qh * SW_HD:(qh + 1) * SW_HD].astype(BF16)
        ks = kcat[:, kv * SW_HD:(kv + 1) * SW_HD]
        vs = vcat[:, kv * SW_HD:(kv + 1) * SW_HD]
        s = _dot_nt(qs, ks) * (SW_HD ** -0.5)
        s = jnp.where(valid, s, NEG_INF)
        sink = sink_ref[qh]
        m = jnp.maximum(jnp.max(s, axis=-1, keepdims=True), sink)
        p = jnp.exp(s - m)
        den = jnp.sum(p, axis=-1, keepdims=True) + jnp.exp(sink - m)
        o = _dot(p.astype(BF16), vs) / den
        o_ref[0, :, qh * SW_HD:(qh + 1) * SW_HD] = o.astype(o_ref.dtype)


def _swa_prompt(sq, skv, sinks):
    bsz, seq, _ = sq.shape
    nblk = seq // WINDOW
    return pl.pallas_call(
        _swa_prompt_kernel,
        grid=(bsz, nblk),
        in_specs=[
            pl.BlockSpec(memory_space=pltpu.SMEM),
            pl.BlockSpec((1, WINDOW, SW_Q_W), lambda b, i: (b, i, 0)),
            pl.BlockSpec((1, WINDOW, 2 * SW_KV_W), lambda b, i: (b, jnp.maximum(i - 1, 0), 0)),
            pl.BlockSpec((1, WINDOW, 2 * SW_KV_W), lambda b, i: (b, i, 0)),
        ],
        out_specs=pl.BlockSpec((1, WINDOW, SW_Q_W), lambda b, i: (b, i, 0)),
        out_shape=jax.ShapeDtypeStruct((bsz, seq, SW_Q_W), F32),
        compiler_params=_cparams(("arbitrary", "arbitrary")),
        name="swa_prompt",
    )(sinks, sq, skv, skv)


def _swa_decode_kernel(sink_ref, q_ref, kvn_ref, kb_ref, vb_ref, o_ref, ko_ref, vo_ref, *, n_new):
    q = q_ref[...]
    kvn = kvn_ref[...]
    kb = kb_ref[...]
    vb = vb_ref[...]
    nb = q.shape[0]
    p_len = kb.shape[1]
    kn = kvn[:, :, :SW_KV_W]
    vn = kvn[:, :, SW_KV_W:]
    ko_ref[:, :p_len - n_new, :] = kb[:, n_new:, :]
    ko_ref[:, p_len - n_new:, :] = kn[:, :n_new, :]
    vo_ref[:, :p_len - n_new, :] = vb[:, n_new:, :]
    vo_ref[:, p_len - n_new:, :] = vn[:, :n_new, :]
    qi = lax.broadcasted_iota(jnp.int32, (nb, DEC_PAD, p_len), 1)
    kj = lax.broadcasted_iota(jnp.int32, (nb, DEC_PAD, p_len), 2)
    valid_buf = (p_len + qi - kj) < WINDOW
    qi2 = lax.broadcasted_iota(jnp.int32, (nb, DEC_PAD, DEC_PAD), 1)
    kj2 = lax.broadcasted_iota(jnp.int32, (nb, DEC_PAD, DEC_PAD), 2)
    valid_new = (kj2 <= qi2) & (kj2 < n_new)
    for qh in range(SW_QHEADS):
        kv = qh // SW_GROUP
        hs = slice(kv * SW_HD, (kv + 1) * SW_HD)
        qs = q[:, :, qh * SW_HD:(qh + 1) * SW_HD]
        sb = jnp.einsum("bqd,bkd->bqk", qs, kb[:, :, hs], preferred_element_type=F32) * (SW_HD ** -0.5)
        sn = jnp.einsum("bqd,bkd->bqk", qs, kn[:, :, hs], preferred_element_type=F32) * (SW_HD ** -0.5)
        sb = jnp.where(valid_buf, sb, NEG_INF)
        sn = jnp.where(valid_new, sn, NEG_INF)
        sink = sink_ref[qh]
        m = jnp.maximum(jnp.maximum(jnp.max(sb, axis=-1, keepdims=True),
                                    jnp.max(sn, axis=-1, keepdims=True)), sink)
        pb = jnp.exp(sb - m)
        pn = jnp.exp(sn - m)
        den = jnp.sum(pb, axis=-1, keepdims=True) + jnp.sum(pn, axis=-1, keepdims=True) + jnp.exp(sink - m)
        o = (jnp.einsum("bqk,bkd->bqd", pb, vb[:, :, hs], preferred_element_type=F32)
             + jnp.einsum("bqk,bkd->bqd", pn, vn[:, :, hs], preferred_element_type=F32))
        o_ref[:, :, qh * SW_HD:(qh + 1) * SW_HD] = (o / den).astype(o_ref.dtype)


def _swa_decode(sq, skv, k_buf, v_buf, sinks, *, nb_blk, n_new):
    bsz = sq.shape[0]
    p_len = k_buf.shape[1]
    tok = lambda w: pl.BlockSpec((nb_blk, DEC_PAD, w), lambda i: (i, 0, 0))
    cache = pl.BlockSpec((nb_blk, p_len, SW_KV_W), lambda i: (i, 0, 0))
    return pl.pallas_call(
        functools.partial(_swa_decode_kernel, n_new=n_new),
        grid=(bsz // nb_blk,),
        in_specs=[pl.BlockSpec(memory_space=pltpu.SMEM), tok(SW_Q_W), tok(2 * SW_KV_W), cache, cache],
        out_specs=[tok(SW_Q_W), cache, cache],
        out_shape=[
            jax.ShapeDtypeStruct((bsz, DEC_PAD, SW_Q_W), F32),
            jax.ShapeDtypeStruct((bsz, p_len, SW_KV_W), F32),
            jax.ShapeDtypeStruct((bsz, p_len, SW_KV_W), F32),
        ],
        compiler_params=_cparams(("arbitrary",)),
        name="swa_decode",
    )(sinks, sq, skv, k_buf, v_buf)


def _outproj_kernel(x_ref, odn_ref, osw_ref, g1_ref, sh2_ref, sc2_ref, n2_ref, w_ref, xo_ref, h2_ref,
                    *, tiles_per_group):
    mix = (_dot(odn_ref[...].astype(BF16), w_ref[:DN_V_W, :])
           + _dot(osw_ref[...].astype(BF16), w_ref[DN_V_W:, :]))
    x = x_ref[...] + _mod_rows(g1_ref, tiles_per_group) * mix
    xo_ref[...] = x
    h2 = (_rms(x, n2_ref[...]) * (1.0 + _mod_rows(sc2_ref, tiles_per_group))
          + _mod_rows(sh2_ref, tiles_per_group))
    h2_ref[...] = h2.astype(h2_ref.dtype)


def _outproj(x, o_dn, o_sw, ms, norm_g, w_out, tt):
    t = x.shape[0]
    return pl.pallas_call(
        functools.partial(_outproj_kernel, tiles_per_group=ms.tiles_per_group),
        grid=(t // tt,),
        in_specs=[
            pl.BlockSpec((tt, D_MODEL), lambda i: (i, 0)),
            pl.BlockSpec((tt, DN_V_W), lambda i: (i, 0)),
            pl.BlockSpec((tt, SW_Q_W), lambda i: (i, 0)),
            _mod_spec(2, ms, tt),
            _mod_spec(3, ms, tt),
            _mod_spec(4, ms, tt),
            pl.BlockSpec((1, D_MODEL), lambda i: (0, 0)),
            pl.BlockSpec((None, D_MODEL, D_MODEL), lambda i: (ms.layer, 0, 0)),
        ],
        out_specs=[pl.BlockSpec((tt, D_MODEL), lambda i: (i, 0))] * 2,
        out_shape=[jax.ShapeDtypeStruct((t, D_MODEL), F32), jax.ShapeDtypeStruct((t, D_MODEL), BF16)],
        compiler_params=_cparams(("arbitrary",)),
        name="outproj",
    )(x, o_dn, o_sw, ms.arr, ms.arr, ms.arr, norm_g.reshape(1, D_MODEL), w_out)


def _top_values(work, n, with_rank=False):
    out = []
    rank = jnp.full(work.shape, float(n), F32) if with_rank else None
    for it in range(n):
        m = jnp.max(work, axis=0, keepdims=True)
        out.append(m)
        if it + 1 < n or with_rank:
            hit = work == m
            if with_rank:
                rank = jnp.where(hit, float(it), rank)
            work = jnp.where(hit, NEG_INF, work)
    return (out, rank) if with_rank else out


def _peer_kernel(h2_ref, x_ref, g2_ref, wq_ref, keys_ref, u_ref, vt_ref, fg_ref, o_ref,
                 s_s, cnt_s, e1_s, rk_s, e2_s, tv_s, acc, hb_s, *, tt, et, final_norm, tiles_per_group):
    j = pl.program_id(1)
    lg_n = tt // LANES
    n_rr = et // N_KEYS
    bf16_rows = 2 * SUBLANES

    def row_bf16(row):
        one = jnp.broadcast_to(row, (bf16_rows, LANES)).astype(BF16)
        return jnp.concatenate([one] * (N_KEYS // bf16_rows), axis=0)

    @pl.when(j == 0)
    def _():
        acc[...] = jnp.zeros_like(acc)
        tv_s[...] = jnp.full(tv_s.shape, NEG_INF, F32)
        h2 = h2_ref[...]
        hb_s[...] = h2.astype(F32).T.astype(BF16)
        q = _dot(h2, wq_ref[...]).astype(BF16)
        for hp in range(2 * PK_HEADS):
            s_t = _dot_nt(keys_ref[hp], q[:, hp * PK_HALF:(hp + 1) * PK_HALF])
            for lg in range(lg_n):
                s_s[hp, lg] = s_t[:, lg * LANES:(lg + 1) * LANES]

        def head_body(idx, carry):
            h = idx // lg_n
            lg = idx % lg_n
            s1 = s_s[2 * h, lg]
            s2 = s_s[2 * h + 1, lg]
            t1 = _top_values(s1, N_TOP)
            t2, rank2 = _top_values(s2, N_TOP, with_rank=True)
            for it in range(N_TOP):
                tv_s[it:it + 1, :] = t2[it]
            for it in range(TOP_SPLIT, N_TOP):
                tv_s[TOP_T1_ROW + it - TOP_SPLIT:TOP_T1_ROW + it - TOP_SPLIT + 1, :] = t1[it]
            rid = lax.broadcasted_iota(jnp.int32, (SUBLANES, LANES), 0)
            cands = []

            def add_blocks(row, base, count):
                for r0 in range(0, count, SUBLANES):
                    blk = row + tv_s[base + r0:base + r0 + SUBLANES, :]
                    if count - r0 < SUBLANES:
                        blk = jnp.where(rid < count - r0, blk, NEG_INF)
                    cands.append(blk)

            for i in range(TOP_SPLIT):
                add_blocks(t1[i], 0, N_TOP // (i + 1))
            for jx in range(N_TOP // (TOP_SPLIT + 1)):
                add_blocks(t2[jx], TOP_T1_ROW, N_TOP // (jx + 1) - TOP_SPLIT)
            cand = jnp.concatenate(cands, axis=0)
            tops = _top_values(cand, N_TOP)
            thr = 0.5 * (tops[PK_TOPK - 1] + tops[PK_TOPK])
            m1 = t1[0]
            m2 = t2[0]
            zsum = jnp.sum(jnp.where(cand >= thr, jnp.exp(cand - (m1 + m2)), 0.0), axis=0, keepdims=True)
            need = thr - s1
            cnt = jnp.zeros_like(s1)
            for it in range(N_TOP):
                cnt = cnt + jnp.where(t2[it] >= need, 1.0, 0.0)
            cnt_s[h, lg] = cnt
            rk_s[h, lg] = rank2.astype(BF16)
            e1_s[h, lg] = jnp.exp(s1 - m1) * (0.5 / zsum)
            e2_s[h, lg] = jnp.exp(s2 - m2).astype(BF16)
            return carry

        lax.fori_loop(0, PK_HEADS * lg_n, head_body, 0, unroll=2)

    a_t = _dot(u_ref[...], hb_s[...])
    a_b = a_t.astype(BF16)
    act = a_b * (1.0 + lax.erf(a_b * (2.0 ** -0.5)))
    total = None
    for r0 in range(0, n_rr, PEER_MIX_R):
        blocks = []
        for rr in range(r0, min(r0 + PEER_MIX_R, n_rr)):
            r = j * n_rr + rr
            row_blocks = []
            for lg in range(lg_n):
                g = None
                for h in range(PK_HEADS):
                    cnt_b = row_bf16(cnt_s[h, lg, pl.ds(r, 1), :])
                    e1_b = row_bf16(e1_s[h, lg, pl.ds(r, 1), :])
                    term = jnp.where(rk_s[h, lg] < cnt_b, e2_s[h, lg], jnp.zeros((), BF16)) * e1_b
                    g = term if g is None else g + term
                row_blocks.append(act[rr * N_KEYS:(rr + 1) * N_KEYS, lg * LANES:(lg + 1) * LANES] * g)
            blocks.append(jnp.concatenate(row_blocks, axis=1) if lg_n > 1 else row_blocks[0])
        p_t = jnp.concatenate(blocks, axis=0) if len(blocks) > 1 else blocks[0]
        cols = slice(r0 * N_KEYS, min(r0 + PEER_MIX_R, n_rr) * N_KEYS)
        part = _dot(vt_ref[:, cols], p_t)
        total = part if total is None else total + part
    acc[...] += total

    @pl.when(j == pl.num_programs(1) - 1)
    def _():
        x = x_ref[...] + _mod_rows(g2_ref, tiles_per_group) * acc[...].T
        if final_norm:
            x = _rms(x, fg_ref[...])
        o_ref[...] = x


def _peer(h2, x, ms, wq, keys, u_tab, vt_tab, final_g, *, tt, et, final_norm):
    t = x.shape[0]
    lg_n = tt // LANES
    n_j = N_EXPERTS // et
    kern = functools.partial(_peer_kernel, tt=tt, et=et, final_norm=final_norm,
                             tiles_per_group=ms.tiles_per_group)
    const = lambda shape: pl.BlockSpec(shape, lambda i, j: (0,) * len(shape), pipeline_mode=pl.Buffered(1))
    layer_const = lambda shape: pl.BlockSpec((None,) + shape, lambda i, j: (ms.layer,) + (0,) * len(shape),
                                             pipeline_mode=pl.Buffered(1))
    return pl.pallas_call(
        kern,
        grid=(t // tt, n_j),
        in_specs=[
            pl.BlockSpec((tt, D_MODEL), lambda i, j: (i, 0)),
            pl.BlockSpec((tt, D_MODEL), lambda i, j: (i, 0)),
            _mod_spec(5, ms, tt),
            layer_const((D_MODEL, PK_HEADS * PK_QDIM)),
            layer_const((2 * PK_HEADS, N_KEYS, PK_HALF)),
            pl.BlockSpec((None, et, D_MODEL), lambda i, j: (ms.layer, j, 0)),
            pl.BlockSpec((None, D_MODEL, et), lambda i, j: (ms.layer, 0, j)),
            const((1, D_MODEL)),
        ],
        out_specs=pl.BlockSpec((tt, D_MODEL), lambda i, j: (i, 0)),
        out_shape=jax.ShapeDtypeStruct((t, D_MODEL), F32),
        scratch_shapes=[
            pltpu.VMEM((2 * PK_HEADS, lg_n, N_KEYS, LANES), F32),
            pltpu.VMEM((PK_HEADS, lg_n, N_KEYS, LANES), F32),
            pltpu.VMEM((PK_HEADS, lg_n, N_KEYS, LANES), F32),
            pltpu.VMEM((PK_HEADS, lg_n, N_KEYS, LANES), BF16),
            pltpu.VMEM((PK_HEADS, lg_n, N_KEYS, LANES), BF16),
            pltpu.VMEM((TOP_ROWS, LANES), F32),
            pltpu.VMEM((D_MODEL, tt), F32),
            pltpu.VMEM((D_MODEL, tt), BF16),
        ],
        compiler_params=_cparams(("arbitrary", "arbitrary")),
        name="peer",
    )(h2, x, ms.arr, wq, keys, u_tab, vt_tab, final_g.reshape(1, D_MODEL))


def _pad_lanes(v):
    return jnp.pad(v, ((0, 0), (0, LANES - v.shape[-1])))


def _prep_w_in(w):
    return jnp.concatenate(
        [w[:, :OFF_B], _pad_lanes(w[:, OFF_B:OFF_A]), _pad_lanes(w[:, OFF_A:OFF_SQ]), w[:, OFF_SQ:]],
        axis=1).astype(BF16)


def _layer(l, x, ms, prior, p, *, seq, tt, final_norm, final_g):
    t = x.shape[0]
    bsz = t // seq
    qkv, z, bcol, acol, sq, skv = _inproj(x, ms, p["norm1_g"][l], p["w_in"], tt)
    r3 = lambda a: a.reshape(bsz, seq, a.shape[-1])
    qkv3 = r3(qkv)
    if prior is None:
        conv_buf = jnp.zeros((bsz, DN_CONV - 1, CONV_DIM), F32)
        s0 = jnp.zeros((bsz, DN_HEADS, DN_DK, DN_DV), F32)
        dn_cfg = dict(nb_blk=1, lb=256, chunk=DN_CHUNK, valid_len=256)
    else:
        s0, conv_buf, k_buf, v_buf, n_new = prior
        dn_cfg = dict(nb_blk=8, lb=DEC_PAD, chunk=DEC_PAD, valid_len=n_new)
        r3 = lambda a: jnp.pad(a.reshape(bsz, seq, a.shape[-1]), ((0, 0), (0, DEC_PAD - seq), (0, 0)))
    o_dn, s_new = _deltanet(r3(qkv), r3(z), r3(bcol), r3(acol), p["conv_w"][l], p["alog"][l], p["dtb"][l],
                            p["dn_norm_g"][l], conv_buf, s0, **dn_cfg)
    if prior is None:
        o_sw = _swa_prompt(r3(sq), r3(skv), p["sw_sinks"][l])
        new_conv = qkv3[:, seq - (DN_CONV - 1):, :]
        n_buf = WINDOW
        new_k = r3(skv)[:, seq - n_buf:, :SW_KV_W].reshape(bsz, n_buf, SW_KVHEADS, SW_HD)
        new_v = r3(skv)[:, seq - n_buf:, SW_KV_W:].reshape(bsz, n_buf, SW_KVHEADS, SW_HD)
    else:
        p_len = k_buf.shape[1]
        o_sw, new_k, new_v = _swa_decode(r3(sq), r3(skv), k_buf.reshape(bsz, p_len, SW_KV_W),
                                         v_buf.reshape(bsz, p_len, SW_KV_W), p["sw_sinks"][l],
                                         nb_blk=16, n_new=n_new)
        new_conv = jnp.concatenate([conv_buf, qkv3], axis=1)[:, seq:]
        new_k = new_k.reshape(bsz, p_len, SW_KVHEADS, SW_HD)
        new_v = new_v.reshape(bsz, p_len, SW_KVHEADS, SW_HD)
        o_dn, o_sw = o_dn[:, :seq], o_sw[:, :seq]
    x, h2 = _outproj(x, o_dn.reshape(t, DN_V_W), o_sw.reshape(t, SW_Q_W), ms,
                     p["norm2_g"][l], p["w_out"], tt)
    x = _peer(h2, x, ms, p["wq"], p["keys"], p["u"], p["vt"], final_g,
              tt=tt, et=2048, final_norm=final_norm)
    return x, (s_new, new_conv, new_k, new_v)


def _trunk(x, mod, row0, tiles_per_group, prior_stack, p, final_g, *, seq, tt):
    new = []
    for l in range(DEPTH):
        prior = None if prior_stack is None else tuple(s[l] for s in prior_stack[:4]) + (prior_stack[4],)
        x, st = _layer(l, x, ModSrc(mod, l, row0, tiles_per_group), prior, p, seq=seq, tt=tt,
                       final_norm=(l == DEPTH - 1), final_g=final_g)
        new.append(st)
    return x, [jnp.stack(zz) for zz in zip(*new)]


def kernel(x_prompt, x_sample, state_delta, state_conv, cache_swa_k, cache_swa_v, c_prompt, c_sample,
           norm1_g, norm2_g, final_norm_g, w_ada, b_ada, w_in, conv_w, dn_a_log, dn_dt_bias, dn_norm_g,
           sw_sinks, w_out, peer_wq, peer_keys, peer_u, peer_v):
    batch, seq, _ = x_prompt.shape
    dec_b, dec_l, _ = x_sample.shape
    tt = 512

    p = dict(
        norm1_g=norm1_g, norm2_g=norm2_g,
        w_in=jax.vmap(_prep_w_in)(w_in),
        conv_w=conv_w,
        alog=_pad_lanes(dn_a_log).reshape(DEPTH, 1, LANES),
        dtb=_pad_lanes(dn_dt_bias).reshape(DEPTH, 1, LANES),
        dn_norm_g=dn_norm_g.reshape(DEPTH, 1, DN_DV),
        sw_sinks=sw_sinks,
        w_out=w_out.astype(BF16),
        wq=peer_wq.astype(BF16),
        keys=peer_keys.reshape(DEPTH, 2 * PK_HEADS, N_KEYS, PK_HALF).astype(BF16),
        u=peer_u.astype(BF16),
        vt=jnp.swapaxes(peer_v, 1, 2).astype(BF16),
    )

    n_dec = dec_b * dec_l
    assert batch <= SUBLANES and n_dec % tt == 0 and DN_CONV - 1 <= dec_l <= DEC_PAD
    c_dec = jnp.repeat(c_sample, dec_l, axis=0)
    pad_rows = (-(n_dec + batch)) % SUBLANES
    c_all = jnp.concatenate([c_dec, c_prompt, jnp.zeros((pad_rows, D_MODEL), F32)], axis=0)
    mod = _ada_mod(c_all, w_ada, b_ada)

    xp = x_prompt.reshape(batch * seq, D_MODEL)
    yp, (delta_p, conv_p, k_p, v_p) = _trunk(xp, mod, n_dec, seq // tt, None, p, final_norm_g, seq=seq, tt=tt)

    xs = x_sample.reshape(n_dec, D_MODEL)
    prior = (state_delta, state_conv, cache_swa_k, cache_swa_v, dec_l)
    ys, (delta_s, conv_s, k_s, v_s) = _trunk(xs, mod, 0, None, prior, p, final_norm_g, seq=dec_l, tt=tt)

    y_prompt = yp.reshape(batch, seq, D_MODEL)
    y_sample = ys.reshape(dec_b, dec_l, D_MODEL)
    return (y_prompt, y_sample, delta_p, conv_p, k_p, v_p, delta_s, conv_s, k_s, v_s)
```
